```python
import math
import jax
import jax.numpy as jnp
from jax import lax
import numpy as np

D_MODEL = 4096
BATCH = 1
SEQ = 8192
DEPTH = 1

S5_WIDTH = D_MODEL
S5_GROUP = 16
S5_GROUPS = S5_WIDTH // S5_GROUP
S5_STATE = 64
S5_CHUNK = 128

M2_EXPAND = 2
M2_D_INNER = M2_EXPAND * D_MODEL
M2_HEADDIM = 64
M2_HEADS = M2_D_INNER // M2_HEADDIM
M2_GROUPS = 8
M2_HPG = M2_HEADS // M2_GROUPS
M2_STATE = 128
M2_CONV = 4
M2_CHUNK = 128
M2_CONV_DIM = M2_D_INNER + 2 * M2_GROUPS * M2_STATE

N_BRANCHES = 2
IN_COLS = 2 * S5_WIDTH + M2_D_INNER + M2_CONV_DIM + M2_HEADS + N_BRANCHES * D_MODEL
EPS = 1e-6

kernel_name = "hybrid_s5_ssd_gated_block"


def rms_norm(x, g):
    xf = x.astype(jnp.float32)
    xf = xf * lax.rsqrt(jnp.mean(xf * xf, axis=-1, keepdims=True) + EPS)
    return xf.astype(x.dtype) * g


def s5_combine(e1, e2):
    a1r, a1i, b1r, b1i = e1
    a2r, a2i, b2r, b2i = e2
    return (a2r * a1r - a2i * a1i, a2r * a1i + a2i * a1r,
            a2r * b1r - a2i * b1i + b2r, a2r * b1i + a2i * b1r + b2i)


def s5_mixer(u, lam_re, lam_im, log_step, b_re, b_im, c_re, c_im, d_skip):
    f32 = jnp.float32
    bsz, seqlen, _ = u.shape
    step = jnp.exp(log_step.astype(f32))[:, None]
    lre, lim = lam_re.astype(f32), lam_im.astype(f32)
    mag = jnp.exp(lre * step)
    lb_re, lb_im = mag * jnp.cos(lim * step), mag * jnp.sin(lim * step)
    den = lre * lre + lim * lim
    nr, ni = lb_re - 1.0, lb_im
    coef_re = (nr * lre + ni * lim) / den
    coef_im = (ni * lre - nr * lim) / den
    bre, bim = b_re.astype(f32), b_im.astype(f32)
    bb_re = coef_re[..., None] * bre - coef_im[..., None] * bim
    bb_im = coef_re[..., None] * bim + coef_im[..., None] * bre
    cre, cim = c_re.astype(f32), c_im.astype(f32)
    a_re = jnp.broadcast_to(lb_re, (bsz, S5_CHUNK, S5_GROUPS, S5_STATE))
    a_im = jnp.broadcast_to(lb_im, (bsz, S5_CHUNK, S5_GROUPS, S5_STATE))

    n_chunks = seqlen // S5_CHUNK
    uf = u.astype(f32)
    uc = uf.reshape(bsz, n_chunks, S5_CHUNK, S5_GROUPS, S5_GROUP).swapaxes(0, 1)

    def chunk_step(carry, u_chunk):
        h_re, h_im = carry
        bu_re = jnp.einsum('btgp,gnp->btgn', u_chunk, bb_re)
        bu_im = jnp.einsum('btgp,gnp->btgn', u_chunk, bb_im)
        bu_re = bu_re.at[:, 0].add(lb_re * h_re - lb_im * h_im)
        bu_im = bu_im.at[:, 0].add(lb_re * h_im + lb_im * h_re)
        _, _, s_re, s_im = lax.associative_scan(s5_combine, (a_re, a_im, bu_re, bu_im), axis=1)
        y = jnp.einsum('btgn,gpn->btgp', s_re, cre) - jnp.einsum('btgn,gpn->btgp', s_im, cim)
        return (s_re[:, -1], s_im[:, -1]), y

    h0 = jnp.zeros((bsz, S5_GROUPS, S5_STATE), f32)
    _, y = lax.scan(chunk_step, (h0, h0), uc)
    y = y.swapaxes(0, 1).reshape(bsz, seqlen, S5_WIDTH)
    return (y + d_skip.astype(f32) * uf).astype(u.dtype)


def causal_depthwise_conv(x, w, b):
    k_width, seqlen = w.shape[0], x.shape[1]
    xp = jnp.pad(x, ((0, 0), (k_width - 1, 0), (0, 0)))
    y = b
    for k in range(k_width):
        y = y + xp[:, k:k + seqlen] * w[k]
    return y


def ssd_chunked(xs, dt, a, bmat, cmat):
    f32 = jnp.float32
    bsz, seqlen = xs.shape[:2]
    nc = seqlen // M2_CHUNK

    def to_chunks(t):
        return t.reshape(bsz, nc, M2_CHUNK, *t.shape[2:]).swapaxes(0, 1)

    x5 = xs.astype(f32).reshape(bsz, seqlen, M2_GROUPS, M2_HPG, M2_HEADDIM)
    dt4 = dt.astype(f32).reshape(bsz, seqlen, M2_GROUPS, M2_HPG)
    a2 = a.astype(f32).reshape(M2_GROUPS, M2_HPG)
    causal = jnp.tril(jnp.ones((M2_CHUNK, M2_CHUNK), dtype=bool))[None, :, :, None, None]

    def chunk_step(state, inp):
        xc, dtc, bc, cc = inp
        acum = jnp.cumsum(dtc * a2, axis=1)
        seg = acum[:, :, None] - acum[:, None, :]
        decay = jnp.exp(jnp.where(causal, seg, -jnp.inf))
        xdt = xc * dtc[..., None]
        scores = jnp.einsum('btgn,bsgn->btsg', cc, bc)
        y_diag = jnp.einsum('btsgr,bsgrp->btgrp', scores[..., None] * decay, xdt)
        y_off = jnp.einsum('btgn,bgrpn->btgrp', cc, state) * jnp.exp(acum)[..., None]
        to_end = jnp.exp(acum[:, -1:] - acum)
        new_state = state * jnp.exp(acum[:, -1])[..., None, None] + jnp.einsum(
            'btgn,btgrp->bgrpn', bc, xdt * to_end[..., None])
        return new_state, y_diag + y_off

    init = jnp.zeros((bsz, M2_GROUPS, M2_HPG, M2_HEADDIM, M2_STATE), f32)
    _, y = lax.scan(chunk_step, init,
                    (to_chunks(x5), to_chunks(dt4), to_chunks(bmat.astype(f32)), to_chunks(cmat.astype(f32))))
    return y.swapaxes(0, 1).reshape(bsz, seqlen, M2_HEADS, M2_HEADDIM)


def mamba2_mixer(z, xbc, dt_raw, conv_w, conv_b, dt_bias, a_log, d_skip, norm_g):
    f32 = jnp.float32
    bsz, seqlen, _ = z.shape
    xbc = jax.nn.silu(causal_depthwise_conv(xbc, conv_w, conv_b))
    gn = M2_GROUPS * M2_STATE
    xs, bmat, cmat = jnp.split(xbc, [M2_D_INNER, M2_D_INNER + gn], axis=-1)
    xs = xs.reshape(bsz, seqlen, M2_HEADS, M2_HEADDIM)
    bmat = bmat.reshape(bsz, seqlen, M2_GROUPS, M2_STATE)
    cmat = cmat.reshape(bsz, seqlen, M2_GROUPS, M2_STATE)
    dt = jax.nn.softplus(dt_raw.astype(f32) + dt_bias.astype(f32))
    a = -jnp.exp(a_log.astype(f32))
    y = ssd_chunked(xs, dt, a, bmat, cmat)
    y = y + d_skip.astype(f32)[:, None] * xs.astype(f32)
    y = y.reshape(bsz, seqlen, M2_D_INNER) * jax.nn.silu(z.astype(f32))
    yg = y.reshape(bsz, seqlen, M2_GROUPS, M2_D_INNER // M2_GROUPS)
    yg = yg * lax.rsqrt(jnp.mean(yg * yg, axis=-1, keepdims=True) + EPS)
    return yg.reshape(bsz, seqlen, M2_D_INNER).astype(z.dtype) * norm_g


def setup_inputs(seed: int = 0) -> dict:
    key = jax.random.key(seed)
    ks = jax.random.split(key, 28)
    f32 = jnp.float32
    L = DEPTH

    def nrm(k, shape, scale):
        return scale * jax.random.normal(k, shape, f32)

    n_idx = jnp.arange(S5_STATE, dtype=f32)
    lam_re = -0.5 + nrm(ks[6], (L, S5_GROUPS, S5_STATE), 0.01)
    lam_im = math.pi * n_idx + nrm(ks[7], (L, S5_GROUPS, S5_STATE), 0.01)
    log_step = jax.random.uniform(ks[8], (L, S5_GROUPS), f32, math.log(1e-3), math.log(1e-1))
    dt0 = jnp.exp(jax.random.uniform(ks[17], (L, M2_HEADS), f32, math.log(1e-3), math.log(1e-1)))
    dt_bias = dt0 + jnp.log(-jnp.expm1(-dt0))
    a_log = jnp.log(jax.random.uniform(ks[18], (L, M2_HEADS), f32, 1.0, 16.0))
    return {
        'x': nrm(ks[0], (BATCH, SEQ, D_MODEL), 1.0),
        'c': nrm(ks[1], (BATCH, D_MODEL), 1.0),
        'w_ada': nrm(ks[2], (L, D_MODEL, 3 * D_MODEL), 0.5 * D_MODEL ** -0.5),
        'b_ada': nrm(ks[3], (L, 3 * D_MODEL), 0.01),
        'norm_g': 1.0 + nrm(ks[4], (L, D_MODEL), 0.01),
        'w_in': nrm(ks[5], (L, D_MODEL, IN_COLS), D_MODEL ** -0.5),
        's5_lambda_re': lam_re,
        's5_lambda_im': lam_im,
        's5_log_step': log_step,
        's5_b_re': nrm(ks[9], (L, S5_GROUPS, S5_STATE, S5_GROUP), (2 * S5_GROUP) ** -0.5),
        's5_b_im': nrm(ks[10], (L, S5_GROUPS, S5_STATE, S5_GROUP), (2 * S5_GROUP) ** -0.5),
        's5_c_re': nrm(ks[11], (L, S5_GROUPS, S5_GROUP, S5_STATE), S5_STATE ** -0.5),
        's5_c_im': nrm(ks[12], (L, S5_GROUPS, S5_GROUP, S5_STATE), S5_STATE ** -0.5),
        's5_d': nrm(ks[13], (L, S5_WIDTH), 1.0),
        's5_w_glu': nrm(ks[14], (L, S5_WIDTH, 2 * S5_WIDTH), S5_WIDTH ** -0.5),
        's5_b_glu': nrm(ks[15], (L, 2 * S5_WIDTH), 0.01),
        'm2_conv_w': nrm(ks[16], (L, M2_CONV, M2_CONV_DIM), M2_CONV ** -0.5),
        'm2_conv_b': nrm(ks[19], (L, M2_CONV_DIM), 0.01),
        'm2_dt_bias': dt_bias,
        'm2_a_log': a_log,
        'm2_d': 1.0 + nrm(ks[20], (L, M2_HEADS), 0.1),
        'm2_norm_g': 1.0 + nrm(ks[21], (L, M2_D_INNER), 0.01),
        'w_br_s5': nrm(ks[22], (L, S5_WIDTH, D_MODEL), S5_WIDTH ** -0.5),
        'w_br_m2': nrm(ks[23], (L, M2_D_INNER, D_MODEL), M2_D_INNER ** -0.5),
        'w_out': nrm(ks[24], (L, D_MODEL, D_MODEL), D_MODEL ** -0.5),
        'final_g': 1.0 + nrm(ks[25], (D_MODEL,), 0.01),
    }


def reference(x, c, w_ada, b_ada, norm_g, w_in, s5_lambda_re, s5_lambda_im, s5_log_step,
              s5_b_re, s5_b_im, s5_c_re, s5_c_im, s5_d, s5_w_glu, s5_b_glu,
              m2_conv_w, m2_conv_b, m2_dt_bias, m2_a_log, m2_d, m2_norm_g,
              w_br_s5, w_br_m2, w_out, final_g):
    widths = [S5_WIDTH, S5_WIDTH, M2_D_INNER, M2_CONV_DIM, M2_HEADS, D_MODEL]
    offsets = [int(o) for o in np.cumsum(widths)]
    for layer in range(DEPTH):
        mod = c @ w_ada[layer] + b_ada[layer]
        shift, scale, gate = jnp.split(mod, 3, axis=-1)
        h = rms_norm(x, norm_g[layer]) * (1.0 + scale[:, None]) + shift[:, None]
        proj = h @ w_in[layer]
        s5_u, s5_z, m2_z, m2_xbc, m2_dt, g_s5, g_m2 = jnp.split(proj, offsets, axis=-1)
        y_s5 = s5_mixer(s5_u, s5_lambda_re[layer], s5_lambda_im[layer], s5_log_step[layer],
                        s5_b_re[layer], s5_b_im[layer], s5_c_re[layer], s5_c_im[layer], s5_d[layer])
        glu = jax.nn.gelu(y_s5) @ s5_w_glu[layer] + s5_b_glu[layer]
        glu_a, glu_b = jnp.split(glu, 2, axis=-1)
        y_s5 = glu_a * jax.nn.sigmoid(glu_b) * jax.nn.silu(s5_z)
        y_m2 = mamba2_mixer(m2_z, m2_xbc, m2_dt, m2_conv_w[layer], m2_conv_b[layer],
                            m2_dt_bias[layer], m2_a_log[layer], m2_d[layer], m2_norm_g[layer])
        merged = (jax.nn.sigmoid(g_s5) * (y_s5 @ w_br_s5[layer])
                  + jax.nn.sigmoid(g_m2) * (y_m2 @ w_br_m2[layer]))
        x = x + gate[:, None] * (merged @ w_out[layer])
    return rms_norm(x, final_g)
```

```python
import functools
import math

import jax
import jax.numpy as jnp
from jax import lax
from jax.experimental import pallas as pl
from jax.experimental.pallas import tpu as pltpu

F32 = jnp.float32
BF16 = jnp.bfloat16

LANE = 128
SUBLANE = 8
VMEM_LIMIT_BYTES = 56 * 1024 * 1024

D_MODEL = 4096
EPS = 1e-6

S5_GROUP = 16
S5_STATE = 64
S5_Q = 16
S5_TILE_GROUPS = LANE // S5_GROUP
S5_TILE_STATE = S5_TILE_GROUPS * S5_STATE
S5_TILES = D_MODEL // LANE

M2_D_INNER = 2 * D_MODEL
M2_HEADDIM = 64
M2_HEADS = M2_D_INNER // M2_HEADDIM
M2_GROUPS = 8
M2_HPG = M2_HEADS // M2_GROUPS
M2_STATE = 128
M2_CONV = 4
M2_CHUNK = 128
M2_GW = M2_HPG * M2_HEADDIM
M2_GN = M2_GROUPS * M2_STATE
M2_CONV_DIM = M2_D_INNER + 2 * M2_GN

COL_S5U = 0
COL_S5Z = COL_S5U + D_MODEL
COL_M2Z = COL_S5Z + D_MODEL
COL_XBC = COL_M2Z + M2_D_INNER
COL_GS5 = COL_XBC + M2_CONV_DIM
COL_GM2 = COL_GS5 + D_MODEL
P_COLS = COL_GM2 + D_MODEL
W_IN_DT = COL_GS5


def _params(*sem):
    return pltpu.CompilerParams(dimension_semantics=sem, vmem_limit_bytes=VMEM_LIMIT_BYTES)


MOD_TN = 512


def _mod_body(cb_ref, w_ref, b_ref, o_ref):
    cb = cb_ref[...]
    for q in range(MOD_TN // LANE):
        sl = slice(q * LANE, (q + 1) * LANE)
        o_ref[:, sl] = jnp.sum(w_ref[:, sl] * cb, axis=0, keepdims=True) + b_ref[:, sl]


def _ada_mod(c, w_ada, b_ada):
    n = w_ada.shape[1]
    cb = jnp.broadcast_to(c.reshape(D_MODEL, 1), (D_MODEL, LANE))
    return pl.pallas_call(
        _mod_body,
        out_shape=jax.ShapeDtypeStruct((1, n), F32),
        grid=(n // MOD_TN,),
        in_specs=[pl.BlockSpec((D_MODEL, LANE), lambda j: (0, 0)),
                  pl.BlockSpec((D_MODEL, MOD_TN), lambda j: (0, j)),
                  pl.BlockSpec((1, MOD_TN), lambda j: (0, j))],
        out_specs=pl.BlockSpec((1, MOD_TN), lambda j: (0, j)),
        compiler_params=_params("arbitrary"),
        name="mod",
    )(cb, w_ada, b_ada.reshape(1, n))


def _rms(x):
    return x * lax.rsqrt(jnp.mean(x * x, axis=-1, keepdims=True) + EPS)


def _norm_body(x_ref, g_ref, shift_ref, scale_ref, o_ref):
    h = _rms(x_ref[...]) * g_ref[...] * (1.0 + scale_ref[...]) + shift_ref[...]
    o_ref[...] = h.astype(BF16)


def _norm_mod(x2, g, mod, tm):
    seq = x2.shape[0]
    return pl.pallas_call(
        _norm_body,
        out_shape=jax.ShapeDtypeStruct((seq, D_MODEL), BF16),
        grid=(seq // tm,),
        in_specs=[pl.BlockSpec((tm, D_MODEL), lambda i: (i, 0)),
                  pl.BlockSpec((1, D_MODEL), lambda i: (0, 0)),
                  pl.BlockSpec((1, D_MODEL), lambda i: (0, 0)),
                  pl.BlockSpec((1, D_MODEL), lambda i: (0, 1))],
        out_specs=pl.BlockSpec((tm, D_MODEL), lambda i: (i, 0)),
        compiler_params=_params("arbitrary"),
        name="norm",
    )(x2, g, mod, mod)


def _mm_body(a_ref, w_ref, o_ref):
    o_ref[...] = jnp.dot(a_ref[...], w_ref[...], preferred_element_type=F32).astype(o_ref.dtype)


def _inproj(h, w, tm, tn):
    seq, n = h.shape[0], w.shape[1]
    return pl.pallas_call(
        _mm_body,
        out_shape=jax.ShapeDtypeStruct((seq, n), BF16),
        grid=(seq // tm, n // tn),
        in_specs=[pl.BlockSpec((tm, D_MODEL), lambda i, j: (i, 0)),
                  pl.BlockSpec((D_MODEL, tn), lambda i, j: (0, j))],
        out_specs=pl.BlockSpec((tm, tn), lambda i, j: (i, j)),
        compiler_params=_params("arbitrary", "arbitrary"),
        name="inproj",
    )(h, w)


def _dt_body(wt_ref, a_ref, o_ref):
    o_ref[...] = lax.dot_general(wt_ref[...], a_ref[...], (((1,), (1,)), ((), ())),
                                 preferred_element_type=F32)


def _dt_proj(h, w_dt_t, tm):
    seq = h.shape[0]
    return pl.pallas_call(
        _dt_body,
        out_shape=jax.ShapeDtypeStruct((M2_HEADS, seq), F32),
        grid=(seq // tm,),
        in_specs=[pl.BlockSpec((M2_HEADS, D_MODEL), lambda i: (0, 0)),
                  pl.BlockSpec((tm, D_MODEL), lambda i: (i, 0))],
        out_specs=pl.BlockSpec((M2_HEADS, tm), lambda i: (0, i)),
        compiler_params=_params("arbitrary"),
        name="dtproj",
    )(w_dt_t, h)


def _s5prep_body(lre_ref, lim_ref, st_ref, btr_ref, bti_ref, ctr_ref, cti_ref,
                 wb_ref, wct_ref, w2_ref):
    ns = S5_TILE_STATE
    lre, lim, step = lre_ref[...], lim_ref[...], jnp.exp(st_ref[...])
    mag = jnp.exp(lre * step)
    lbr, lbi = mag * jnp.cos(lim * step), mag * jnp.sin(lim * step)
    den = lre * lre + lim * lim
    nr, ni = lbr - 1.0, lbi
    cr = (nr * lre + ni * lim) / den
    ci = (ni * lre - nr * lim) / den
    row = jnp.right_shift(lax.broadcasted_iota(jnp.int32, (LANE, ns), 0), int(math.log2(S5_GROUP)))
    col = jnp.right_shift(lax.broadcasted_iota(jnp.int32, (LANE, ns), 1), int(math.log2(S5_STATE)))
    same = row == col
    btr = jnp.where(same, btr_ref[...], 0.0)
    bti = jnp.where(same, bti_ref[...], 0.0)
    ctr = jnp.where(same, ctr_ref[...], 0.0)
    cti = jnp.where(same, cti_ref[...], 0.0)
    bbr = cr * btr - ci * bti
    bbi = cr * bti + ci * btr
    pows = [(jnp.ones_like(lbr), jnp.zeros_like(lbr))]
    for _ in range(S5_Q):
        pr, pi = pows[-1]
        pows.append((pr * lbr - pi * lbi, pr * lbi + pi * lbr))
    for s in range(S5_Q):
        pr, pi = pows[S5_Q - 1 - s]
        rows = slice(s * LANE, (s + 1) * LANE)
        wb_ref[rows, :ns] = (pr * bbr - pi * bbi).astype(BF16)
        wb_ref[rows, ns:] = (pr * bbi + pi * bbr).astype(BF16)
    for t in range(S5_Q):
        pr, pi = pows[t + 1]
        rows = slice(t * LANE, (t + 1) * LANE)
        wct_ref[rows, :ns] = (ctr * pr - cti * pi).astype(BF16)
        wct_ref[rows, ns:] = (-(ctr * pi + cti * pr)).astype(BF16)
    ccat = jnp.concatenate([ctr, -cti], axis=1).astype(BF16)
    kall = lax.dot_general(wb_ref[...], ccat, (((1,), (1,)), ((), ())),
                           preferred_element_type=F32)
    w2_ref[LANE:, :LANE] = jnp.zeros((LANE, LANE), BF16)
    for tau in range(S5_Q):
        s = S5_Q - 1 - tau
        k_tau = kall[s * LANE:(s + 1) * LANE, :].astype(BF16)
        w2_ref[:LANE, tau * LANE:(tau + 1) * LANE] = k_tau
        if tau + 1 < S5_Q:
            w2_ref[LANE:, (tau + 1) * LANE:(tau + 2) * LANE] = k_tau


def _s5_prep(lam_re, lam_im, log_step, b_re, b_im, c_re, c_im):
    ns = S5_TILE_STATE

    def rowform(a):
        return a.astype(F32).reshape(S5_TILES, 1, ns)

    def tiled(a):
        a = a.astype(F32).reshape(S5_TILES, LANE, S5_STATE)
        return jnp.tile(a, (1, 1, S5_TILE_GROUPS))

    step = jnp.broadcast_to(log_step.astype(F32)[:, None], lam_re.shape)
    row_spec = pl.BlockSpec((None, 1, ns), lambda j: (j, 0, 0))
    mat_spec = pl.BlockSpec((None, LANE, ns), lambda j: (j, 0, 0))
    return pl.pallas_call(
        _s5prep_body,
        out_shape=(jax.ShapeDtypeStruct((S5_TILES, S5_Q * LANE, 2 * ns), BF16),
                   jax.ShapeDtypeStruct((S5_TILES, S5_Q * LANE, 2 * ns), BF16),
                   jax.ShapeDtypeStruct((S5_TILES, 2 * LANE, S5_Q * LANE), BF16)),
        grid=(S5_TILES,),
        in_specs=[row_spec, row_spec, row_spec, mat_spec, mat_spec, mat_spec, mat_spec],
        out_specs=(pl.BlockSpec((None, S5_Q * LANE, 2 * ns), lambda j: (j, 0, 0)),
                   pl.BlockSpec((None, S5_Q * LANE, 2 * ns), lambda j: (j, 0, 0)),
                   pl.BlockSpec((None, 2 * LANE, S5_Q * LANE), lambda j: (j, 0, 0))),
        compiler_params=_params("arbitrary"),
        name="s5prep",
    )(rowform(lam_re), rowform(lam_im), rowform(step),
      tiled(b_re.swapaxes(1, 2)), tiled(b_im.swapaxes(1, 2)), tiled(c_re), tiled(c_im))


def _slab_specs(nc, n_lead):
    pb = P_COLS // LANE

    def spec(s):
        if n_lead == 1:
            return pl.BlockSpec((nc, LANE), lambda j: (0, s * pb + COL_S5U // LANE + j))
        return pl.BlockSpec((nc, LANE), lambda j, t: (0, s * pb + COL_S5U // LANE + j))

    return [spec(s) for s in range(S5_Q)]


def _s5in_body(*refs):
    slabs, wb_ref, ere_ref, eim_ref = refs[:S5_Q], refs[S5_Q], refs[S5_Q + 1], refs[S5_Q + 2]
    ucat = jnp.concatenate([r[...] for r in slabs], axis=1)
    e = jnp.dot(ucat, wb_ref[...], preferred_element_type=F32)
    ere_ref[...] = e[:, :S5_TILE_STATE]
    eim_ref[...] = e[:, S5_TILE_STATE:]


def _s5_in(pv, wb, nc):
    ns = S5_TILE_STATE
    out = jax.ShapeDtypeStruct((nc, S5_TILES * ns), F32)
    return pl.pallas_call(
        _s5in_body,
        out_shape=(out, out),
        grid=(S5_TILES,),
        in_specs=_slab_specs(nc, 1) + [pl.BlockSpec((None, S5_Q * LANE, 2 * ns), lambda j: (j, 0, 0))],
        out_specs=(pl.BlockSpec((nc, ns), lambda j: (0, j)), pl.BlockSpec((nc, ns), lambda j: (0, j))),
        compiler_params=_params("arbitrary"),
        name="s5in",
    )(*([pv] * S5_Q), wb)


S5_REC_BLOCK = 32


def _s5rec_body(lre_ref, lim_ref, st_ref, ere_ref, eim_ref, hre_ref, him_ref, a_scr, h_scr):
    @pl.when(pl.program_id(0) == 0)
    def _():
        lre, lim, step = lre_ref[...], lim_ref[...], jnp.exp(st_ref[...])
        mag = jnp.exp(lre * step)
        ar, ai = mag * jnp.cos(lim * step), mag * jnp.sin(lim * step)
        for _ in range(int(math.log2(S5_Q))):
            ar, ai = ar * ar - ai * ai, 2.0 * ar * ai
        a_scr[0] = ar
        a_scr[1] = ai
        h_scr[...] = jnp.zeros_like(h_scr)

    ar, ai = a_scr[0], a_scr[1]

    def step_fn(c, carry):
        hr, hi = carry
        hre_ref[c] = hr.astype(BF16)
        him_ref[c] = hi.astype(BF16)
        return (ar * hr - ai * hi + ere_ref[c], ar * hi + ai * hr + eim_ref[c])

    hr, hi = lax.fori_loop(0, ere_ref.shape[0], step_fn, (h_scr[0], h_scr[1]))
    h_scr[0] = hr
    h_scr[1] = hi


def _s5_rec(lam_re, lam_im, step, e_re, e_im, nc):
    rows = (lam_re.size) // LANE
    blk = min(S5_REC_BLOCK, nc)
    par = pl.BlockSpec((rows, LANE), lambda i: (0, 0))
    ev = pl.BlockSpec((blk, rows, LANE), lambda i: (i, 0, 0))
    out = jax.ShapeDtypeStruct((nc, rows, LANE), BF16)
    return pl.pallas_call(
        _s5rec_body,
        out_shape=(out, out),
        grid=(nc // blk,),
        in_specs=[par, par, par, ev, ev],
        out_specs=(ev, ev),
        scratch_shapes=[pltpu.VMEM((2, rows, LANE), F32), pltpu.VMEM((2, rows, LANE), F32)],
        compiler_params=_params("arbitrary"),
        name="s5rec",
    )(lam_re.reshape(rows, LANE), lam_im.reshape(rows, LANE), step.reshape(rows, LANE),
      e_re.reshape(nc, rows, LANE), e_im.reshape(nc, rows, LANE))


def _s5out_body(*refs):
    slabs = refs[:S5_Q]
    hre_ref, him_ref, wct_ref, w2_ref, d_ref, o_ref, acc, ybuf = refs[S5_Q:]
    t_id = pl.program_id(1)

    @pl.when(t_id == 0)
    def _():
        hcat = jnp.concatenate([hre_ref[...], him_ref[...]], axis=1)
        acc[...] = lax.dot_general(hcat, wct_ref[...], (((1,), (1,)), ((), ())),
                                   preferred_element_type=F32)
        for s in range(0, S5_Q, 2):
            pair = jnp.concatenate([slabs[s][...], slabs[s + 1][...]], axis=1)
            width = (S5_Q - s) * LANE
            acc[:, s * LANE:] += jnp.dot(pair, w2_ref[:, :width], preferred_element_type=F32)
        d = d_ref[...]
        for t in range(S5_Q):
            y = acc[:, t * LANE:(t + 1) * LANE] + d * slabs[t][...].astype(F32)
            ybuf[t] = jax.nn.gelu(y).astype(BF16)

    o_ref[...] = ybuf[t_id]


def _s5_out(pv, h_re, h_im, wct, w2, d_skip, nc):
    ns = S5_TILE_STATE
    return pl.pallas_call(
        _s5out_body,
        out_shape=jax.ShapeDtypeStruct((S5_Q, nc, D_MODEL), BF16),
        grid=(S5_TILES, S5_Q),
        in_specs=_slab_specs(nc, 2) + [
            pl.BlockSpec((nc, ns), lambda j, t: (0, j)),
            pl.BlockSpec((nc, ns), lambda j, t: (0, j)),
            pl.BlockSpec((None, S5_Q * LANE, 2 * ns), lambda j, t: (j, 0, 0)),
            pl.BlockSpec((None, 2 * LANE, S5_Q * LANE), lambda j, t: (j, 0, 0)),
            pl.BlockSpec((1, LANE), lambda j, t: (0, j))],
        out_specs=pl.BlockSpec((None, nc, LANE), lambda j, t: (t, 0, j)),
        scratch_shapes=[pltpu.VMEM((nc, S5_Q * LANE), F32), pltpu.VMEM((S5_Q, nc, LANE), BF16)],
        compiler_params=_params("arbitrary", "arbitrary"),
        name="s5out",
    )(*([pv] * S5_Q), h_re.reshape(nc, -1), h_im.reshape(nc, -1), wct, w2,
      d_skip.astype(F32).reshape(1, D_MODEL))


def _glu_body(a_ref, wa_ref, wb_ref, ba_ref, bb_ref, z_ref, o_ref):
    a = a_ref[...]
    ga = jnp.dot(a, wa_ref[...], preferred_element_type=F32) + ba_ref[...]
    gb = jnp.dot(a, wb_ref[...], preferred_element_type=F32) + bb_ref[...]
    o_ref[...] = (ga * jax.nn.sigmoid(gb) * jax.nn.silu(z_ref[...].astype(F32))).astype(BF16)


def _glu(yp, w_glu, b_glu, pv, nc, rb, tn):
    nb = D_MODEL // tn
    pb = P_COLS // tn
    return pl.pallas_call(
        _glu_body,
        out_shape=jax.ShapeDtypeStruct((S5_Q, nc, D_MODEL), BF16),
        grid=(S5_Q, nc // rb, nb),
        in_specs=[pl.BlockSpec((None, rb, D_MODEL), lambda s, i, j: (s, i, 0)),
                  pl.BlockSpec((D_MODEL, tn), lambda s, i, j: (0, j)),
                  pl.BlockSpec((D_MODEL, tn), lambda s, i, j: (0, nb + j)),
                  pl.BlockSpec((1, tn), lambda s, i, j: (0, j)),
                  pl.BlockSpec((1, tn), lambda s, i, j: (0, nb + j)),
                  pl.BlockSpec((rb, tn), lambda s, i, j: (i, s * pb + COL_S5Z // tn + j))],
        out_specs=pl.BlockSpec((None, rb, tn), lambda s, i, j: (s, i, j)),
        compiler_params=_params("arbitrary", "arbitrary", "arbitrary"),
        name="glu",
    )(yp, w_glu, w_glu, b_glu, b_glu, pv)


def _split3(x):
    hi = x.astype(BF16)
    r1 = x - hi.astype(F32)
    mid = r1.astype(BF16)
    lo = (r1 - mid.astype(F32)).astype(BF16)
    return hi, mid, lo


def _causal_conv_silu(x, prev, w, b):
    y = b + w[M2_CONV - 1:M2_CONV, :] * x
    rid = lax.broadcasted_iota(jnp.int32, (SUBLANE, x.shape[1]), 0)
    for d in range(1, M2_CONV):
        xr = pltpu.roll(x, d, 0)
        head = jnp.where(rid < d, pltpu.roll(prev, d, 0), xr[:SUBLANE])
        xd = jnp.concatenate([head, xr[SUBLANE:]], axis=0)
        y = y + w[M2_CONV - 1 - d:M2_CONV - d, :] * xd
    return jax.nn.silu(y)


def _pair_expand(cm, col0, n_rows):
    left = lax.broadcasted_iota(jnp.int32, (n_rows, LANE), 1) < M2_HEADDIM
    blocks = []
    for p in range(M2_HPG // 2):
        a = jnp.broadcast_to(cm[:n_rows, col0 + 2 * p:col0 + 2 * p + 1], (n_rows, LANE))
        b = jnp.broadcast_to(cm[:n_rows, col0 + 2 * p + 1:col0 + 2 * p + 2], (n_rows, LANE))
        blocks.append(jnp.where(left, a, b))
    return jnp.concatenate(blocks, axis=1)


def _ssd_body(x_ref, b_ref, c_ref, z_ref, wx_ref, wb_ref, wc_ref, bx_ref, bb_ref, bc_ref,
              dt_ref, dtb_ref, alog_ref, dskip_ref, ng_ref, tri_ref, o_ref,
              state, px, pb, pc):
    T = M2_CHUNK

    @pl.when(pl.program_id(1) == 0)
    def _():
        state[...] = jnp.zeros_like(state)
        px[...] = jnp.zeros_like(px)
        pb[...] = jnp.zeros_like(pb)
        pc[...] = jnp.zeros_like(pc)

    xraw = x_ref[...].astype(F32)
    braw = b_ref[...].astype(F32)
    craw = c_ref[...].astype(F32)
    xs = _causal_conv_silu(xraw, px[...], wx_ref[...], bx_ref[...])
    bm = _causal_conv_silu(braw, pb[...], wb_ref[...], bb_ref[...])
    cm_ = _causal_conv_silu(craw, pc[...], wc_ref[...], bc_ref[...])
    px[...] = xraw[T - SUBLANE:]
    pb[...] = braw[T - SUBLANE:]
    pc[...] = craw[T - SUBLANE:]

    dt = jax.nn.softplus(dt_ref[...] + dtb_ref[...])
    dta = dt * (-jnp.exp(alog_ref[...]))
    hi, mid, lo = _split3(dta)
    tri = tri_ref[...]
    acum = (jnp.dot(hi, tri, preferred_element_type=F32) + jnp.dot(mid, tri, preferred_element_type=F32)
            + jnp.dot(lo, tri, preferred_element_type=F32))
    rem = acum[:, T - 1:T] - acum
    pad = jnp.zeros((T - 3 * M2_HPG, T), F32)
    colm = jnp.concatenate([dt, acum, rem, pad], axis=0).T
    dt_e = _pair_expand(colm, 0, T)
    rem_e = _pair_expand(colm, 2 * M2_HPG, T)
    last_e = _pair_expand(colm[T - SUBLANE:], M2_HPG, SUBLANE)[SUBLANE - 1:SUBLANE]

    xdt = xs * dt_e
    xdt_b = xdt.astype(BF16)
    bm_b, cm_b = bm.astype(BF16), cm_.astype(BF16)
    scores = lax.dot_general(cm_b, bm_b, (((1,), (1,)), ((), ())), preferred_element_type=F32)
    st_b = state[...].astype(BF16)

    causal = (lax.broadcasted_iota(jnp.int32, (T, T), 0) >= lax.broadcasted_iota(jnp.int32, (T, T), 1))
    left = lax.broadcasted_iota(jnp.int32, (T, LANE), 1) < M2_HEADDIM
    zero = jnp.zeros((T, LANE), BF16)
    ys = []
    for p in range(M2_HPG // 2):
        sl = slice(p * LANE, (p + 1) * LANE)
        lhs = []
        for h in (2 * p, 2 * p + 1):
            acol = jnp.broadcast_to(colm[:, M2_HPG + h:M2_HPG + h + 1], (T, T))
            arow = jnp.broadcast_to(acum[h:h + 1, :], (T, T))
            decay = jnp.exp(jnp.where(causal, acol - arow, -1e30))
            lhs.append((scores * decay).astype(BF16))
            lhs.append((cm_ * jnp.exp(acol)).astype(BF16))
        xp, sp = xdt_b[:, sl], st_b[:, sl]
        rhs = jnp.concatenate([jnp.where(left, xp, zero), jnp.where(left, sp, zero),
                               jnp.where(left, zero, xp), jnp.where(left, zero, sp)], axis=0)
        ys.append(jnp.dot(jnp.concatenate(lhs, axis=1), rhs, preferred_element_type=F32))
    y = jnp.concatenate(ys, axis=1) + dskip_ref[...] * xs

    xw = (xdt * jnp.exp(rem_e)).astype(BF16)
    state[...] = state[...] * jnp.exp(last_e) + jnp.dot(bm.T.astype(BF16), xw, preferred_element_type=F32)

    y = y * jax.nn.silu(z_ref[...].astype(F32))
    o_ref[...] = (_rms(y) * ng_ref[...]).astype(BF16)


def _ssd(p, dt_t, conv_w, conv_b, dt_bias, a_log, d_skip, norm_g):
    seq = p.shape[0]
    T, gw, n = M2_CHUNK, M2_GW, M2_STATE
    xb = COL_XBC // gw
    bb = (COL_XBC + M2_D_INNER) // n
    cb = bb + M2_GROUPS
    lanes = lambda v: jnp.broadcast_to(v.astype(F32)[:, None], (M2_HEADS, T))
    tri = (jnp.arange(T)[:, None] <= jnp.arange(T)[None, :]).astype(BF16)
    cw, cbias = conv_w.astype(F32), conv_b.astype(F32).reshape(1, M2_CONV_DIM)
    wb0, wc0 = M2_D_INNER // n, M2_D_INNER // n + M2_GROUPS
    return pl.pallas_call(
        _ssd_body,
        out_shape=jax.ShapeDtypeStruct((seq, M2_D_INNER), BF16),
        grid=(M2_GROUPS, seq // T),
        in_specs=[pl.BlockSpec((T, gw), lambda g, c: (c, xb + g)),
                  pl.BlockSpec((T, n), lambda g, c: (c, bb + g)),
                  pl.BlockSpec((T, n), lambda g, c: (c, cb + g)),
                  pl.BlockSpec((T, gw), lambda g, c: (c, COL_M2Z // gw + g)),
                  pl.BlockSpec((M2_CONV, gw), lambda g, c: (0, g)),
                  pl.BlockSpec((M2_CONV, n), lambda g, c: (0, wb0 + g)),
                  pl.BlockSpec((M2_CONV, n), lambda g, c: (0, wc0 + g)),
                  pl.BlockSpec((1, gw), lambda g, c: (0, g)),
                  pl.BlockSpec((1, n), lambda g, c: (0, wb0 + g)),
                  pl.BlockSpec((1, n), lambda g, c: (0, wc0 + g)),
                  pl.BlockSpec((M2_HPG, T), lambda g, c: (g, c)),
                  pl.BlockSpec((M2_HPG, T), lambda g, c: (g, 0)),
                  pl.BlockSpec((M2_HPG, T), lambda g, c: (g, 0)),
                  pl.BlockSpec((1, gw), lambda g, c: (0, g)),
                  pl.BlockSpec((1, gw), lambda g, c: (0, g)),
                  pl.BlockSpec((T, T), lambda g, c: (0, 0))],
        out_specs=pl.BlockSpec((T, gw), lambda g, c: (c, g)),
        scratch_shapes=[pltpu.VMEM((n, gw), F32), pltpu.VMEM((SUBLANE, gw), F32),
                        pltpu.VMEM((SUBLANE, n), F32), pltpu.VMEM((SUBLANE, n), F32)],
        compiler_params=_params("arbitrary", "arbitrary"),
        name="ssd",
    )(p, p, p, p, cw, cw, cw, cbias, cbias, cbias, dt_t, lanes(dt_bias), lanes(a_log),
      jnp.repeat(d_skip.astype(F32), M2_HEADDIM).reshape(1, M2_D_INNER),
      norm_g.astype(F32).reshape(1, M2_D_INNER), tri)


def _merge_body(a1_ref, w1_ref, a2_ref, w2_ref, g1_ref, g2_ref, o_ref):
    m1 = jnp.dot(a1_ref[...], w1_ref[...], preferred_element_type=F32)
    m2 = jnp.dot(a2_ref[...], w2_ref[...], preferred_element_type=F32)
    o_ref[...] = (jax.nn.sigmoid(g1_ref[...].astype(F32)) * m1
                  + jax.nn.sigmoid(g2_ref[...].astype(F32)) * m2).astype(BF16)


def _merge(y1, w1, y2v, w2, pv, nc, rb, tn):
    pb = P_COLS // tn
    return pl.pallas_call(
        _merge_body,
        out_shape=jax.ShapeDtypeStruct((S5_Q, nc, D_MODEL), BF16),
        grid=(S5_Q, nc // rb, D_MODEL // tn),
        in_specs=[pl.BlockSpec((None, rb, D_MODEL), lambda s, i, j: (s, i, 0)),
                  pl.BlockSpec((D_MODEL, tn), lambda s, i, j: (0, j)),
                  pl.BlockSpec((rb, M2_D_INNER), lambda s, i, j: (i, s)),
                  pl.BlockSpec((M2_D_INNER, tn), lambda s, i, j: (0, j)),
                  pl.BlockSpec((rb, tn), lambda s, i, j: (i, s * pb + COL_GS5 // tn + j)),
                  pl.BlockSpec((rb, tn), lambda s, i, j: (i, s * pb + COL_GM2 // tn + j))],
        out_specs=pl.BlockSpec((None, rb, tn), lambda s, i, j: (s, i, j)),
        compiler_params=_params("arbitrary", "arbitrary", "arbitrary"),
        name="merge",
    )(y1, w1, y2v, w2, pv, pv)


def _out_body(a_ref, w_ref, x_ref, gate_ref, o_ref):
    m = jnp.dot(a_ref[...], w_ref[...], preferred_element_type=F32)
    o_ref[...] = x_ref[...] + gate_ref[...] * m


def _out_proj(merged, w_out, xv, mod, nc, rb, tn):
    nb = D_MODEL // tn
    return pl.pallas_call(
        _out_body,
        out_shape=jax.ShapeDtypeStruct((nc, S5_Q * D_MODEL), F32),
        grid=(S5_Q, nc // rb, nb),
        in_specs=[pl.BlockSpec((None, rb, D_MODEL), lambda s, i, j: (s, i, 0)),
                  pl.BlockSpec((D_MODEL, tn), lambda s, i, j: (0, j)),
                  pl.BlockSpec((rb, tn), lambda s, i, j: (i, s * nb + j)),
                  pl.BlockSpec((1, tn), lambda s, i, j: (0, 2 * nb + j))],
        out_specs=pl.BlockSpec((rb, tn), lambda s, i, j: (i, s * nb + j)),
        compiler_params=_params("arbitrary", "arbitrary", "arbitrary"),
        name="outproj",
    )(merged, w_out, xv, mod)


def _fnorm_body(x_ref, g_ref, o_ref):
    o_ref[...] = _rms(x_ref[...]) * g_ref[...]


def _final_norm(x2, g, tm):
    seq = x2.shape[0]
    return pl.pallas_call(
        _fnorm_body,
        out_shape=jax.ShapeDtypeStruct((seq, D_MODEL), F32),
        grid=(seq // tm,),
        in_specs=[pl.BlockSpec((tm, D_MODEL), lambda i: (i, 0)),
                  pl.BlockSpec((1, D_MODEL), lambda i: (0, 0))],
        out_specs=pl.BlockSpec((tm, D_MODEL), lambda i: (i, 0)),
        compiler_params=_params("arbitrary"),
        name="fnorm",
    )(x2, g)


def _tiles(seq):
    nc = seq // S5_Q
    return dict(nc=nc, rb=min(512, nc), tm=min(1024, seq), tm_norm=min(256, seq),
                tn_in=1024, tn=512)


def kernel(x, c, w_ada, b_ada, norm_g, w_in, s5_lambda_re, s5_lambda_im, s5_log_step, s5_b_re, s5_b_im,
           s5_c_re, s5_c_im, s5_d, s5_w_glu, s5_b_glu, m2_conv_w, m2_conv_b, m2_dt_bias, m2_a_log, m2_d,
           m2_norm_g, w_br_s5, w_br_m2, w_out, final_g):
    bsz, seq, _ = x.shape
    assert bsz == 1 and w_ada.shape[0] == 1 and seq % (S5_Q * SUBLANE) == 0 and seq % M2_CHUNK == 0
    tl = _tiles(seq)
    nc, rb = tl["nc"], tl["rb"]
    x2 = x.reshape(seq, D_MODEL)

    w_main = jnp.concatenate([w_in[0][:, :W_IN_DT], w_in[0][:, W_IN_DT + M2_HEADS:]], axis=1).astype(BF16)
    w_dt_t = w_in[0][:, W_IN_DT:W_IN_DT + M2_HEADS].T.astype(BF16)
    w_glu = s5_w_glu[0].astype(BF16)
    w1, w2, wo = w_br_s5[0].astype(BF16), w_br_m2[0].astype(BF16), w_out[0].astype(BF16)

    mod = _ada_mod(c, w_ada[0], b_ada[0])
    h = _norm_mod(x2, norm_g[0].reshape(1, D_MODEL), mod, tl["tm_norm"])
    p = _inproj(h, w_main, tl["tm"], tl["tn_in"])
    dt_t = _dt_proj(h, w_dt_t, tl["tm"])
    pv = p.reshape(nc, S5_Q * P_COLS)

    wb, wct, w2t = _s5_prep(s5_lambda_re[0], s5_lambda_im[0], s5_log_step[0],
                            s5_b_re[0], s5_b_im[0], s5_c_re[0], s5_c_im[0])
    e_re, e_im = _s5_in(pv, wb, nc)
    step = jnp.broadcast_to(s5_log_step[0].astype(F32)[:, None], s5_lambda_re[0].shape)
    h_re, h_im = _s5_rec(s5_lambda_re[0].astype(F32), s5_lambda_im[0].astype(F32), step, e_re, e_im, nc)
    yp = _s5_out(pv, h_re, h_im, wct, w2t, s5_d[0], nc)
    y1 = _glu(yp, w_glu, s5_b_glu[0].astype(F32).reshape(1, -1), pv, nc, rb, tl["tn"])

    y2 = _ssd(p, dt_t, m2_conv_w[0], m2_conv_b[0], m2_dt_bias[0], m2_a_log[0], m2_d[0], m2_norm_g[0])

    merged = _merge(y1, w1, y2.reshape(nc, S5_Q * M2_D_INNER), w2, pv, nc, rb, tl["tn"])
    xn = _out_proj(merged, wo, x2.reshape(nc, S5_Q * D_MODEL), mod, nc, rb, tl["tn"])
    out = _final_norm(xn.reshape(seq, D_MODEL), final_g.astype(F32).reshape(1, D_MODEL), tl["tm_norm"])
    return out.reshape(bsz, seq, D_MODEL)
```

```python
import math

import jax
import jax.numpy as jnp
from jax import lax
from jax.experimental import pallas as pl
from jax.experimental.pallas import tpu as pltpu

F32 = jnp.float32
BF16 = jnp.bfloat16

LANE = 128
SUBLANE = 8
VMEM_LIMIT_BYTES = 56 * 1024 * 1024

D_MODEL = 4096
EPS = 1e-6

S5_GROUP = 16
S5_STATE = 64
S5_Q = 16
S5_TILE_GROUPS = LANE // S5_GROUP
S5_TILE_STATE = S5_TILE_GROUPS * S5_STATE
S5_TILES = D_MODEL // LANE

M2_D_INNER = 2 * D_MODEL
M2_HEADDIM = 64
M2_HEADS = M2_D_INNER // M2_HEADDIM
M2_GROUPS = 8
M2_HPG = M2_HEADS // M2_GROUPS
M2_STATE = 128
M2_CONV = 4
M2_CHUNK = 128
M2_GW = M2_HPG * M2_HEADDIM
M2_GN = M2_GROUPS * M2_STATE
M2_CONV_DIM = M2_D_INNER + 2 * M2_GN

WIN_S5Z = D_MODEL
WIN_DT = 2 * D_MODEL + M2_D_INNER + M2_CONV_DIM
WIN_GATES = WIN_DT + M2_HEADS
P_S5Z = 0
P_M2Z = P_S5Z + D_MODEL
P_XBC = P_M2Z + M2_D_INNER
P_COLS = P_XBC + M2_CONV_DIM


def _params(*sem):
    return pltpu.CompilerParams(dimension_semantics=sem, vmem_limit_bytes=VMEM_LIMIT_BYTES)


MOD_TN = 512


def _mod_body(cb_ref, w_ref, b_ref, o_ref):
    cb = cb_ref[...]
    for q in range(MOD_TN // LANE):
        sl = slice(q * LANE, (q + 1) * LANE)
        o_ref[:, sl] = jnp.sum(w_ref[:, sl] * cb, axis=0, keepdims=True) + b_ref[:, sl]


def _ada_mod(c, w_ada, b_ada):
    n = w_ada.shape[1]
    cb = jnp.broadcast_to(c.reshape(D_MODEL, 1), (D_MODEL, LANE))
    return pl.pallas_call(
        _mod_body,
        out_shape=jax.ShapeDtypeStruct((1, n), F32),
        grid=(n // MOD_TN,),
        in_specs=[pl.BlockSpec((D_MODEL, LANE), lambda j: (0, 0)),
                  pl.BlockSpec((D_MODEL, MOD_TN), lambda j: (0, j)),
                  pl.BlockSpec((1, MOD_TN), lambda j: (0, j))],
        out_specs=pl.BlockSpec((1, MOD_TN), lambda j: (0, j)),
        compiler_params=_params("arbitrary"),
        name="mod",
    )(cb, w_ada, b_ada.reshape(1, n))


def _rms(x):
    return x * lax.rsqrt(jnp.mean(x * x, axis=-1, keepdims=True) + EPS)


def _norm_body(x_ref, g_ref, shift_ref, scale_ref, o_ref):
    h = _rms(x_ref[...]) * g_ref[...] * (1.0 + scale_ref[...]) + shift_ref[...]
    o_ref[...] = h.astype(BF16)


def _norm_mod(x2, g, mod, tm):
    seq = x2.shape[0]
    return pl.pallas_call(
        _norm_body,
        out_shape=jax.ShapeDtypeStruct((seq, D_MODEL), BF16),
        grid=(seq // tm,),
        in_specs=[pl.BlockSpec((tm, D_MODEL), lambda i: (i, 0)),
                  pl.BlockSpec((1, D_MODEL), lambda i: (0, 0)),
                  pl.BlockSpec((1, D_MODEL), lambda i: (0, 0)),
                  pl.BlockSpec((1, D_MODEL), lambda i: (0, 1))],
        out_specs=pl.BlockSpec((tm, D_MODEL), lambda i: (i, 0)),
        compiler_params=_params("arbitrary"),
        name="norm",
    )(x2, g, mod, mod)


def _mm_body(a_ref, w_ref, o_ref, wbf):
    @pl.when(pl.program_id(1) == 0)
    def _():
        wbf[...] = w_ref[...].astype(BF16)

    o_ref[...] = jnp.dot(a_ref[...], wbf[...], preferred_element_type=F32).astype(o_ref.dtype)


def _proj(h, w3, col0, n, out_dtype, tm, tn, name):
    seq, k = h.shape
    assert col0 % tn == 0 and n % tn == 0
    return pl.pallas_call(
        _mm_body,
        out_shape=jax.ShapeDtypeStruct((seq, n), out_dtype),
        grid=(n // tn, seq // tm),
        in_specs=[pl.BlockSpec((tm, k), lambda j, i: (i, 0)),
                  pl.BlockSpec((None, k, tn), lambda j, i: (0, 0, col0 // tn + j))],
        out_specs=pl.BlockSpec((tm, tn), lambda j, i: (i, j)),
        scratch_shapes=[pltpu.VMEM((k, tn), BF16)],
        compiler_params=_params("arbitrary", "arbitrary"),
        name=name,
    )(h, w3)


def _dt_body(wt_ref, a_ref, o_ref):
    o_ref[...] = lax.dot_general(wt_ref[...], a_ref[...], (((1,), (1,)), ((), ())),
                                 preferred_element_type=F32)


def _dt_proj(h, w_dt_t, tm):
    seq = h.shape[0]
    return pl.pallas_call(
        _dt_body,
        out_shape=jax.ShapeDtypeStruct((M2_HEADS, seq), F32),
        grid=(seq // tm,),
        in_specs=[pl.BlockSpec((M2_HEADS, D_MODEL), lambda i: (0, 0)),
                  pl.BlockSpec((tm, D_MODEL), lambda i: (i, 0))],
        out_specs=pl.BlockSpec((M2_HEADS, tm), lambda i: (0, i)),
        compiler_params=_params("arbitrary"),
        name="dtproj",
    )(w_dt_t, h)


def _lbar(lre, lim, log_step):
    step = jnp.exp(log_step)
    mag = jnp.exp(lre * step)
    return mag * jnp.cos(lim * step), mag * jnp.sin(lim * step)


def _s5prep_body(lre_ref, lim_ref, st_ref, btr_ref, bti_ref, ctr_ref, cti_ref,
                 wb_ref, wct_ref, w2_ref):
    ns = S5_TILE_STATE
    lre, lim = lre_ref[...], lim_ref[...]
    lbr, lbi = _lbar(lre, lim, st_ref[...])
    den = lre * lre + lim * lim
    nr, ni = lbr - 1.0, lbi
    cr = (nr * lre + ni * lim) / den
    ci = (ni * lre - nr * lim) / den
    row = jnp.right_shift(lax.broadcasted_iota(jnp.int32, (LANE, ns), 0), int(math.log2(S5_GROUP)))
    col = jnp.right_shift(lax.broadcasted_iota(jnp.int32, (LANE, ns), 1), int(math.log2(S5_STATE)))
    same = row == col
    btr = jnp.where(same, btr_ref[...], 0.0)
    bti = jnp.where(same, bti_ref[...], 0.0)
    ctr = jnp.where(same, ctr_ref[...], 0.0)
    cti = jnp.where(same, cti_ref[...], 0.0)
    bbr = cr * btr - ci * bti
    bbi = cr * bti + ci * btr
    pows = [(jnp.ones_like(lbr), jnp.zeros_like(lbr))]
    for _ in range(S5_Q):
        pr, pi = pows[-1]
        pows.append((pr * lbr - pi * lbi, pr * lbi + pi * lbr))
    for s in range(S5_Q):
        pr, pi = pows[S5_Q - 1 - s]
        rows = slice(s * LANE, (s + 1) * LANE)
        wb_ref[rows, :ns] = (pr * bbr - pi * bbi).astype(BF16)
        wb_ref[rows, ns:] = (pr * bbi + pi * bbr).astype(BF16)
    for t in range(S5_Q):
        pr, pi = pows[t + 1]
        rows = slice(t * LANE, (t + 1) * LANE)
        wct_ref[rows, :ns] = (ctr * pr - cti * pi).astype(BF16)
        wct_ref[rows, ns:] = (-(ctr * pi + cti * pr)).astype(BF16)
    ccat = jnp.concatenate([ctr, -cti], axis=1).astype(BF16)
    kall = lax.dot_general(wb_ref[...], ccat, (((1,), (1,)), ((), ())),
                           preferred_element_type=F32)
    w2_ref[LANE:, :LANE] = jnp.zeros((LANE, LANE), BF16)
    for tau in range(S5_Q):
        s = S5_Q - 1 - tau
        k_tau = kall[s * LANE:(s + 1) * LANE, :].astype(BF16)
        w2_ref[:LANE, tau * LANE:(tau + 1) * LANE] = k_tau
        if tau + 1 < S5_Q:
            w2_ref[LANE:, (tau + 1) * LANE:(tau + 2) * LANE] = k_tau


def _s5_rows(lam_re, lam_im, log_step):
    row = lambda a: a.astype(F32).reshape(S5_TILES, 1, S5_TILE_STATE)
    return row(lam_re), row(lam_im), row(jnp.broadcast_to(log_step[:, None], lam_re.shape))


def _s5_prep(rows, b_re, b_im, c_re, c_im):
    ns = S5_TILE_STATE

    def tiled(a):
        a = a.astype(F32).reshape(S5_TILES, LANE, S5_STATE)
        return jnp.tile(a, (1, 1, S5_TILE_GROUPS))

    row_spec = pl.BlockSpec((None, 1, ns), lambda j: (j, 0, 0))
    mat_spec = pl.BlockSpec((None, LANE, ns), lambda j: (j, 0, 0))
    return pl.pallas_call(
        _s5prep_body,
        out_shape=(jax.ShapeDtypeStruct((S5_TILES, S5_Q * LANE, 2 * ns), BF16),
                   jax.ShapeDtypeStruct((S5_TILES, S5_Q * LANE, 2 * ns), BF16),
                   jax.ShapeDtypeStruct((S5_TILES, 2 * LANE, S5_Q * LANE), BF16)),
        grid=(S5_TILES,),
        in_specs=[row_spec, row_spec, row_spec, mat_spec, mat_spec, mat_spec, mat_spec],
        out_specs=(pl.BlockSpec((None, S5_Q * LANE, 2 * ns), lambda j: (j, 0, 0)),
                   pl.BlockSpec((None, S5_Q * LANE, 2 * ns), lambda j: (j, 0, 0)),
                   pl.BlockSpec((None, 2 * LANE, S5_Q * LANE), lambda j: (j, 0, 0))),
        compiler_params=_params("arbitrary"),
        name="s5prep",
    )(*rows, tiled(b_re.swapaxes(1, 2)), tiled(b_im.swapaxes(1, 2)), tiled(c_re), tiled(c_im))


S5_SCAN_UNROLL = 8


def _s5_body(u_ref, lre_ref, lim_ref, st_ref, wb_ref, wct_ref, w2_ref, d_ref, o_ref, e_scr, acc, ynat):
    ns = S5_TILE_STATE
    nc = e_scr.shape[0]
    slabs = [u_ref[pl.ds(s, nc, stride=S5_Q), :] for s in range(S5_Q)]
    sb = [x.astype(BF16) for x in slabs]

    e_scr[...] = jnp.dot(jnp.concatenate(sb, axis=1), wb_ref[...], preferred_element_type=F32)
    ar, ai = _lbar(lre_ref[...], lim_ref[...], st_ref[...])
    for _ in range(int(math.log2(S5_Q))):
        ar, ai = ar * ar - ai * ai, 2.0 * ar * ai

    def step(c, carry):
        hr, hi = carry
        er, ei = e_scr[pl.ds(c, 1), :ns], e_scr[pl.ds(c, 1), ns:]
        e_scr[pl.ds(c, 1), :ns] = hr
        e_scr[pl.ds(c, 1), ns:] = hi
        return ar * hr - ai * hi + er, ar * hi + ai * hr + ei

    zero = jnp.zeros((1, ns), F32)
    lax.fori_loop(0, nc, step, (zero, zero), unroll=S5_SCAN_UNROLL)

    acc[...] = lax.dot_general(e_scr[...].astype(BF16), wct_ref[...], (((1,), (1,)), ((), ())),
                               preferred_element_type=F32)
    for s in range(0, S5_Q, 2):
        pair = jnp.concatenate([sb[s], sb[s + 1]], axis=1)
        width = (S5_Q - s) * LANE
        acc[:, s * LANE:] += jnp.dot(pair, w2_ref[:, :width], preferred_element_type=F32)
    d = d_ref[...]
    for t in range(S5_Q):
        y = acc[:, t * LANE:(t + 1) * LANE] + d * slabs[t]
        ynat[pl.ds(t, nc, stride=S5_Q), :] = jax.nn.gelu(y)
    o_ref[...] = ynat[...].astype(BF16)


def _s5(u, rows, wb, wct, w2, d_skip):
    seq = u.shape[0]
    nc, ns = seq // S5_Q, S5_TILE_STATE
    row_spec = pl.BlockSpec((None, 1, ns), lambda j: (j, 0, 0))
    return pl.pallas_call(
        _s5_body,
        out_shape=jax.ShapeDtypeStruct((seq, D_MODEL), BF16),
        grid=(S5_TILES,),
        in_specs=[pl.BlockSpec((seq, LANE), lambda j: (0, j)), row_spec, row_spec, row_spec,
                  pl.BlockSpec((None, S5_Q * LANE, 2 * ns), lambda j: (j, 0, 0)),
                  pl.BlockSpec((None, S5_Q * LANE, 2 * ns), lambda j: (j, 0, 0)),
                  pl.BlockSpec((None, 2 * LANE, S5_Q * LANE), lambda j: (j, 0, 0)),
                  pl.BlockSpec((1, LANE), lambda j: (0, j))],
        out_specs=pl.BlockSpec((seq, LANE), lambda j: (0, j)),
        scratch_shapes=[pltpu.VMEM((nc, 2 * ns), F32), pltpu.VMEM((nc, S5_Q * LANE), F32),
                        pltpu.VMEM((seq, LANE), F32)],
        compiler_params=_params("arbitrary"),
        name="s5",
    )(u, *rows, wb, wct, w2, d_skip.astype(F32).reshape(1, D_MODEL))


def _glu_body(a_ref, wa_ref, wb_ref, ba_ref, bb_ref, z_ref, o_ref, wbf):
    tn = wa_ref.shape[1]

    @pl.when(pl.program_id(1) == 0)
    def _():
        wbf[:, :tn] = wa_ref[...].astype(BF16)
        wbf[:, tn:] = wb_ref[...].astype(BF16)

    g = jnp.dot(a_ref[...], wbf[...], preferred_element_type=F32)
    ga = g[:, :tn] + ba_ref[...]
    gb = g[:, tn:] + bb_ref[...]
    o_ref[...] = (ga * jax.nn.sigmoid(gb) * jax.nn.silu(z_ref[...].astype(F32))).astype(BF16)


def _glu(ys5, w_glu3, b_glu, p, tm, tn):
    seq = ys5.shape[0]
    nb = D_MODEL // tn
    return pl.pallas_call(
        _glu_body,
        out_shape=jax.ShapeDtypeStruct((seq, D_MODEL), BF16),
        grid=(nb, seq // tm),
        in_specs=[pl.BlockSpec((tm, D_MODEL), lambda j, i: (i, 0)),
                  pl.BlockSpec((None, D_MODEL, tn), lambda j, i: (0, 0, j)),
                  pl.BlockSpec((None, D_MODEL, tn), lambda j, i: (0, 0, nb + j)),
                  pl.BlockSpec((1, tn), lambda j, i: (0, j)),
                  pl.BlockSpec((1, tn), lambda j, i: (0, nb + j)),
                  pl.BlockSpec((tm, tn), lambda j, i: (i, P_S5Z // tn + j))],
        out_specs=pl.BlockSpec((tm, tn), lambda j, i: (i, j)),
        scratch_shapes=[pltpu.VMEM((D_MODEL, 2 * tn), BF16)],
        compiler_params=_params("arbitrary", "arbitrary"),
        name="glu",
    )(ys5, w_glu3, w_glu3, b_glu, b_glu, p)


def _split3(x):
    hi = x.astype(BF16)
    r1 = x - hi.astype(F32)
    mid = r1.astype(BF16)
    lo = (r1 - mid.astype(F32)).astype(BF16)
    return hi, mid, lo


def _causal_conv_silu(x, prev, w, b):
    y = b + w[M2_CONV - 1:M2_CONV, :] * x
    rid = lax.broadcasted_iota(jnp.int32, (SUBLANE, x.shape[1]), 0)
    for d in range(1, M2_CONV):
        xr = pltpu.roll(x, d, 0)
        head = jnp.where(rid < d, pltpu.roll(prev, d, 0), xr[:SUBLANE])
        xd = jnp.concatenate([head, xr[SUBLANE:]], axis=0)
        y = y + w[M2_CONV - 1 - d:M2_CONV - d, :] * xd
    return jax.nn.silu(y)


def _pair_expand(cm, col0, n_rows):
    left = lax.broadcasted_iota(jnp.int32, (n_rows, LANE), 1) < M2_HEADDIM
    blocks = []
    for p in range(M2_HPG // 2):
        a = jnp.broadcast_to(cm[:n_rows, col0 + 2 * p:col0 + 2 * p + 1], (n_rows, LANE))
        b = jnp.broadcast_to(cm[:n_rows, col0 + 2 * p + 1:col0 + 2 * p + 2], (n_rows, LANE))
        blocks.append(jnp.where(left, a, b))
    return jnp.concatenate(blocks, axis=1)


def _ssd_body(x_ref, b_ref, c_ref, z_ref, wx_ref, wb_ref, wc_ref, bx_ref, bb_ref, bc_ref,
              dt_ref, dtb_ref, alog_ref, dskip_ref, ng_ref, tri_ref, o_ref,
              state, px, pb, pc):
    T = M2_CHUNK

    @pl.when(pl.program_id(1) == 0)
    def _():
        state[...] = jnp.zeros_like(state)
        px[...] = jnp.zeros_like(px)
        pb[...] = jnp.zeros_like(pb)
        pc[...] = jnp.zeros_like(pc)

    xraw = x_ref[...].astype(F32)
    braw = b_ref[...].astype(F32)
    craw = c_ref[...].astype(F32)
    xs = _causal_conv_silu(xraw, px[...], wx_ref[...], bx_ref[...])
    bm = _causal_conv_silu(braw, pb[...], wb_ref[...], bb_ref[...])
    cm_ = _causal_conv_silu(craw, pc[...], wc_ref[...], bc_ref[...])
    px[...] = xraw[T - SUBLANE:]
    pb[...] = braw[T - SUBLANE:]
    pc[...] = craw[T - SUBLANE:]

    dt = jax.nn.softplus(dt_ref[...] + dtb_ref[...])
    dta = dt * (-jnp.exp(alog_ref[...]))
    hi, mid, lo = _split3(dta)
    tri = tri_ref[...]
    acum = (jnp.dot(hi, tri, preferred_element_type=F32) + jnp.dot(mid, tri, preferred_element_type=F32)
            + jnp.dot(lo, tri, preferred_element_type=F32))
    rem = acum[:, T - 1:T] - acum
    pad = jnp.zeros((T - 3 * M2_HPG, T), F32)
    colm = jnp.concatenate([dt, acum, rem, pad], axis=0).T
    dt_e = _pair_expand(colm, 0, T)
    rem_e = _pair_expand(colm, 2 * M2_HPG, T)
    last_e = _pair_expand(colm[T - SUBLANE:], M2_HPG, SUBLANE)[SUBLANE - 1:SUBLANE]

    xdt = xs * dt_e
    xdt_b = xdt.astype(BF16)
    bm_b, cm_b = bm.astype(BF16), cm_.astype(BF16)
    scores = lax.dot_general(cm_b, bm_b, (((1,), (1,)), ((), ())), preferred_element_type=F32)
    st_b = state[...].astype(BF16)

    causal = (lax.broadcasted_iota(jnp.int32, (T, T), 0) >= lax.broadcasted_iota(jnp.int32, (T, T), 1))
    left = lax.broadcasted_iota(jnp.int32, (T, LANE), 1) < M2_HEADDIM
    zero = jnp.zeros((T, LANE), BF16)
    ys = []
    for p in range(M2_HPG // 2):
        sl = slice(p * LANE, (p + 1) * LANE)
        lhs = []
        for h in (2 * p, 2 * p + 1):
            acol = jnp.broadcast_to(colm[:, M2_HPG + h:M2_HPG + h + 1], (T, T))
            arow = jnp.broadcast_to(acum[h:h + 1, :], (T, T))
            decay = jnp.exp(jnp.where(causal, acol - arow, -1e30))
            lhs.append((scores * decay).astype(BF16))
            lhs.append((cm_ * jnp.exp(acol)).astype(BF16))
        xp, sp = xdt_b[:, sl], st_b[:, sl]
        rhs = jnp.concatenate([jnp.where(left, xp, zero), jnp.where(left, sp, zero),
                               jnp.where(left, zero, xp), jnp.where(left, zero, sp)], axis=0)
        ys.append(jnp.dot(jnp.concatenate(lhs, axis=1), rhs, preferred_element_type=F32))
    y = jnp.concatenate(ys, axis=1) + dskip_ref[...] * xs

    xw = (xdt * jnp.exp(rem_e)).astype(BF16)
    state[...] = state[...] * jnp.exp(last_e) + jnp.dot(bm.T.astype(BF16), xw, preferred_element_type=F32)

    y = y * jax.nn.silu(z_ref[...].astype(F32))
    o_ref[...] = (_rms(y) * ng_ref[...]).astype(BF16)


def _ssd(p, dt_t, conv_w, conv_b, dt_bias, a_log, d_skip, norm_g):
    seq = p.shape[0]
    T, gw, n = M2_CHUNK, M2_GW, M2_STATE
    xb = P_XBC // gw
    bb = (P_XBC + M2_D_INNER) // n
    cb = bb + M2_GROUPS
    lanes = lambda v: jnp.broadcast_to(v.astype(F32)[:, None], (M2_HEADS, T))
    tri = (jnp.arange(T)[:, None] <= jnp.arange(T)[None, :]).astype(BF16)
    cw, cbias = conv_w.astype(F32), conv_b.astype(F32).reshape(1, M2_CONV_DIM)
    wb0, wc0 = M2_D_INNER // n, M2_D_INNER // n + M2_GROUPS
    return pl.pallas_call(
        _ssd_body,
        out_shape=jax.ShapeDtypeStruct((seq, M2_D_INNER), BF16),
        grid=(M2_GROUPS, seq // T),
        in_specs=[pl.BlockSpec((T, gw), lambda g, c: (c, xb + g)),
                  pl.BlockSpec((T, n), lambda g, c: (c, bb + g)),
                  pl.BlockSpec((T, n), lambda g, c: (c, cb + g)),
                  pl.BlockSpec((T, gw), lambda g, c: (c, P_M2Z // gw + g)),
                  pl.BlockSpec((M2_CONV, gw), lambda g, c: (0, g)),
                  pl.BlockSpec((M2_CONV, n), lambda g, c: (0, wb0 + g)),
                  pl.BlockSpec((M2_CONV, n), lambda g, c: (0, wc0 + g)),
                  pl.BlockSpec((1, gw), lambda g, c: (0, g)),
                  pl.BlockSpec((1, n), lambda g, c: (0, wb0 + g)),
                  pl.BlockSpec((1, n), lambda g, c: (0, wc0 + g)),
                  pl.BlockSpec((M2_HPG, T), lambda g, c: (g, c)),
                  pl.BlockSpec((M2_HPG, T), lambda g, c: (g, 0)),
                  pl.BlockSpec((M2_HPG, T), lambda g, c: (g, 0)),
                  pl.BlockSpec((1, gw), lambda g, c: (0, g)),
                  pl.BlockSpec((1, gw), lambda g, c: (0, g)),
                  pl.BlockSpec((T, T), lambda g, c: (0, 0))],
        out_specs=pl.BlockSpec((T, gw), lambda g, c: (c, g)),
        scratch_shapes=[pltpu.VMEM((n, gw), F32), pltpu.VMEM((SUBLANE, gw), F32),
                        pltpu.VMEM((SUBLANE, n), F32), pltpu.VMEM((SUBLANE, n), F32)],
        compiler_params=_params("arbitrary", "arbitrary"),
        name="ssd",
    )(p, p, p, p, cw, cw, cw, cbias, cbias, cbias, dt_t, lanes(dt_bias), lanes(a_log),
      jnp.repeat(d_skip.astype(F32), M2_HEADDIM).reshape(1, M2_D_INNER),
      norm_g.astype(F32).reshape(1, M2_D_INNER), tri)


def _merge_body(a1_ref, w1_ref, a2_ref, w2_ref, g1_ref, g2_ref, o_ref):
    m1 = jnp.dot(a1_ref[...], w1_ref[...], preferred_element_type=F32)
    m2 = jnp.dot(a2_ref[...], w2_ref[...], preferred_element_type=F32)
    o_ref[...] = (jax.nn.sigmoid(g1_ref[...].astype(F32)) * m1
                  + jax.nn.sigmoid(g2_ref[...].astype(F32)) * m2).astype(BF16)


def _merge(y1, w1, y2, w2, gates, tm, tn):
    seq = y1.shape[0]
    nb = D_MODEL // tn
    return pl.pallas_call(
        _merge_body,
        out_shape=jax.ShapeDtypeStruct((seq, D_MODEL), BF16),
        grid=(seq // tm, nb),
        in_specs=[pl.BlockSpec((tm, D_MODEL), lambda i, j: (i, 0)),
                  pl.BlockSpec((D_MODEL, tn), lambda i, j: (0, j)),
                  pl.BlockSpec((tm, M2_D_INNER), lambda i, j: (i, 0)),
                  pl.BlockSpec((M2_D_INNER, tn), lambda i, j: (0, j)),
                  pl.BlockSpec((tm, tn), lambda i, j: (i, j)),
                  pl.BlockSpec((tm, tn), lambda i, j: (i, nb + j))],
        out_specs=pl.BlockSpec((tm, tn), lambda i, j: (i, j)),
        compiler_params=_params("arbitrary", "arbitrary"),
        name="merge",
    )(y1, w1, y2, w2, gates, gates)


def _out_body(a_ref, w_ref, x_ref, gate_ref, o_ref, wbf):
    @pl.when(pl.program_id(1) == 0)
    def _():
        wbf[...] = w_ref[...].astype(BF16)

    m = jnp.dot(a_ref[...], wbf[...], preferred_element_type=F32)
    o_ref[...] = x_ref[...] + gate_ref[...] * m


def _out_proj(merged, w_out3, x2, mod, tm, tn):
    seq = merged.shape[0]
    nb = D_MODEL // tn
    return pl.pallas_call(
        _out_body,
        out_shape=jax.ShapeDtypeStruct((seq, D_MODEL), F32),
        grid=(nb, seq // tm),
        in_specs=[pl.BlockSpec((tm, D_MODEL), lambda j, i: (i, 0)),
                  pl.BlockSpec((None, D_MODEL, tn), lambda j, i: (0, 0, j)),
                  pl.BlockSpec((tm, tn), lambda j, i: (i, j)),
                  pl.BlockSpec((1, tn), lambda j, i: (0, 2 * nb + j))],
        out_specs=pl.BlockSpec((tm, tn), lambda j, i: (i, j)),
        scratch_shapes=[pltpu.VMEM((D_MODEL, tn), BF16)],
        compiler_params=_params("arbitrary", "arbitrary"),
        name="outproj",
    )(merged, w_out3, x2, mod)


def _fnorm_body(x_ref, g_ref, o_ref):
    o_ref[...] = _rms(x_ref[...]) * g_ref[...]


def _final_norm(x2, g, tm):
    seq = x2.shape[0]
    return pl.pallas_call(
        _fnorm_body,
        out_shape=jax.ShapeDtypeStruct((seq, D_MODEL), F32),
        grid=(seq // tm,),
        in_specs=[pl.BlockSpec((tm, D_MODEL), lambda i: (i, 0)),
                  pl.BlockSpec((1, D_MODEL), lambda i: (0, 0))],
        out_specs=pl.BlockSpec((tm, D_MODEL), lambda i: (i, 0)),
        compiler_params=_params("arbitrary"),
        name="fnorm",
    )(x2, g)


def _tiles(seq):
    return dict(tm=min(1024, seq), tm_merge=min(512, seq), tm_norm=min(256, seq), tn=512, tn_glu=256)


def kernel(x, c, w_ada, b_ada, norm_g, w_in, s5_lambda_re, s5_lambda_im, s5_log_step, s5_b_re, s5_b_im,
           s5_c_re, s5_c_im, s5_d, s5_w_glu, s5_b_glu, m2_conv_w, m2_conv_b, m2_dt_bias, m2_a_log, m2_d,
           m2_norm_g, w_br_s5, w_br_m2, w_out, final_g):
    bsz, seq, _ = x.shape
    assert bsz == 1 and w_ada.shape[0] == 1 and seq % (S5_Q * SUBLANE) == 0 and seq % M2_CHUNK == 0
    tl = _tiles(seq)
    tm, tn = tl["tm"], tl["tn"]
    x2 = x.reshape(seq, D_MODEL)

    w_gates = w_in[0][:, WIN_GATES:].astype(BF16)
    w_dt_t = w_in[0][:, WIN_DT:WIN_GATES].T.astype(BF16)
    w1, w2 = w_br_s5[0].astype(BF16), w_br_m2[0].astype(BF16)

    mod = _ada_mod(c, w_ada[0], b_ada[0])
    h = _norm_mod(x2, norm_g[0].reshape(1, D_MODEL), mod, tl["tm_norm"])
    u = _proj(h, w_in, 0, D_MODEL, F32, tm, tn, "inproj_u")
    p = _proj(h, w_in, WIN_S5Z, P_COLS, BF16, tm, tn, "inproj_p")
    gates = _proj(h, w_gates[None], 0, 2 * D_MODEL, BF16, tm, tn, "inproj_g")
    dt_t = _dt_proj(h, w_dt_t, tm)

    rows = _s5_rows(s5_lambda_re[0], s5_lambda_im[0], s5_log_step[0])
    wb, wct, w2t = _s5_prep(rows, s5_b_re[0], s5_b_im[0], s5_c_re[0], s5_c_im[0])
    ys5 = _s5(u, rows, wb, wct, w2t, s5_d[0])
    y1 = _glu(ys5, s5_w_glu, s5_b_glu[0].astype(F32).reshape(1, -1), p, tm, tl["tn_glu"])

    y2 = _ssd(p, dt_t, m2_conv_w[0], m2_conv_b[0], m2_dt_bias[0], m2_a_log[0], m2_d[0], m2_norm_g[0])

    merged = _merge(y1, w1, y2, w2, gates, tl["tm_merge"], tn)
    xn = _out_proj(merged, w_out, x2, mod, tm, tn)
    out = _final_norm(xn, final_g.astype(F32).reshape(1, D_MODEL), tl["tm_norm"])
    return out.reshape(bsz, seq, D_MODEL)
```

```python
import functools
import math

import jax
import jax.numpy as jnp
from jax import lax
from jax.experimental import pallas as pl
from jax.experimental.pallas import tpu as pltpu

F32 = jnp.float32
BF16 = jnp.bfloat16

LANE = 128
SUBLANE = 8
VMEM_LIMIT_BYTES = 56 * 1024 * 1024

D_MODEL = 4096
EPS = 1e-6

S5_GROUP = 16
S5_STATE = 64
S5_Q = 16
S5_TILE_GROUPS = LANE // S5_GROUP
S5_TILE_STATE = S5_TILE_GROUPS * S5_STATE
S5_TILES = D_MODEL // LANE

M2_D_INNER = 2 * D_MODEL
M2_HEADDIM = 64
M2_HEADS = M2_D_INNER // M2_HEADDIM
M2_GROUPS = 8
M2_HPG = M2_HEADS // M2_GROUPS
M2_STATE = 128
M2_CONV = 4
M2_CHUNK = 128
M2_GW = M2_HPG * M2_HEADDIM
M2_GN = M2_GROUPS * M2_STATE
M2_CONV_DIM = M2_D_INNER + 2 * M2_GN

WIN_S5Z = D_MODEL
WIN_XBC = 2 * D_MODEL + M2_D_INNER
WIN_DT = WIN_XBC + M2_CONV_DIM
WIN_GATES = WIN_DT + M2_HEADS
Z_S5 = 0
Z_M2 = D_MODEL


def _params(*sem):
    return pltpu.CompilerParams(dimension_semantics=sem, vmem_limit_bytes=VMEM_LIMIT_BYTES)


MOD_TN = 512


def _mod_body(cb_ref, w_ref, b_ref, o_ref):
    cb = cb_ref[...]
    for q in range(MOD_TN // LANE):
        sl = slice(q * LANE, (q + 1) * LANE)
        o_ref[:, sl] = jnp.sum(w_ref[:, sl] * cb, axis=0, keepdims=True) + b_ref[:, sl]


def _ada_mod(c, w_ada, b_ada):
    n = w_ada.shape[1]
    cb = jnp.broadcast_to(c.reshape(D_MODEL, 1), (D_MODEL, LANE))
    return pl.pallas_call(
        _mod_body,
        out_shape=jax.ShapeDtypeStruct((1, n), F32),
        grid=(n // MOD_TN,),
        in_specs=[pl.BlockSpec((D_MODEL, LANE), lambda j: (0, 0)),
                  pl.BlockSpec((D_MODEL, MOD_TN), lambda j: (0, j)),
                  pl.BlockSpec((1, MOD_TN), lambda j: (0, j))],
        out_specs=pl.BlockSpec((1, MOD_TN), lambda j: (0, j)),
        compiler_params=_params("arbitrary"),
        name="mod",
    )(cb, w_ada, b_ada.reshape(1, n))


def _rms(x):
    return x * lax.rsqrt(jnp.mean(x * x, axis=-1, keepdims=True) + EPS)


def _norm_body(x_ref, g_ref, shift_ref, scale_ref, o_ref):
    h = _rms(x_ref[...]) * g_ref[...] * (1.0 + scale_ref[...]) + shift_ref[...]
    o_ref[...] = h.astype(BF16)


def _norm_mod(x2, g, mod, tm):
    seq = x2.shape[0]
    return pl.pallas_call(
        _norm_body,
        out_shape=jax.ShapeDtypeStruct((seq, D_MODEL), BF16),
        grid=(seq // tm,),
        in_specs=[pl.BlockSpec((tm, D_MODEL), lambda i: (i, 0)),
                  pl.BlockSpec((1, D_MODEL), lambda i: (0, 0)),
                  pl.BlockSpec((1, D_MODEL), lambda i: (0, 0)),
                  pl.BlockSpec((1, D_MODEL), lambda i: (0, 1))],
        out_specs=pl.BlockSpec((tm, D_MODEL), lambda i: (i, 0)),
        compiler_params=_params("arbitrary"),
        name="norm",
    )(x2, g, mod, mod)


def _causal_conv_silu(x, prev, w, b):
    k = w.shape[0]
    y = b + w[k - 1:k, :] * x
    rid = lax.broadcasted_iota(jnp.int32, (SUBLANE, x.shape[1]), 0)
    for d in range(1, k):
        xr = pltpu.roll(x, d, 0)
        head = jnp.where(rid < d, pltpu.roll(prev, d, 0), xr[:SUBLANE])
        xd = jnp.concatenate([head, xr[SUBLANE:]], axis=0)
        y = y + w[k - 1 - d:k - d, :] * xd
    return jax.nn.silu(y)


def _proj_body(*refs, n_w, act, conv):
    a_ref, w_refs, rest = refs[0], refs[1:1 + n_w], refs[1 + n_w:]
    if conv:
        cw_ref, cb_ref, o_ref, wbf, prev = rest
    else:
        o_ref, wbf = rest
    wsub = w_refs[0].shape[1]

    @pl.when(pl.program_id(1) == 0)
    def _():
        for q, w_ref in enumerate(w_refs):
            wbf[:, q * wsub:(q + 1) * wsub] = w_ref[...].astype(BF16)
        if conv:
            prev[...] = jnp.zeros_like(prev)

    r = jnp.dot(a_ref[...], wbf[...], preferred_element_type=F32)
    if conv:
        o_ref[...] = _causal_conv_silu(r, prev[...], cw_ref[...], cb_ref[...]).astype(o_ref.dtype)
        prev[...] = r[r.shape[0] - SUBLANE:]
    else:
        o_ref[...] = (act(r) if act else r).astype(o_ref.dtype)


def _proj(h, w3, col0, n, out_dtype, tm, tn, name, act=None, wsub=None, conv=None):
    seq, k = h.shape
    wsub = wsub or tn
    n_w = tn // wsub
    assert col0 % wsub == 0 and n % tn == 0 and tn % wsub == 0
    w_specs = [pl.BlockSpec((None, k, wsub), lambda j, i, q=q: (0, 0, col0 // wsub + j * n_w + q))
               for q in range(n_w)]
    extra, extra_specs, scratch = [], [], [pltpu.VMEM((k, tn), BF16)]
    if conv:
        extra = list(conv)
        extra_specs = [pl.BlockSpec((conv[0].shape[0], tn), lambda j, i: (0, j)),
                       pl.BlockSpec((1, tn), lambda j, i: (0, j))]
        scratch.append(pltpu.VMEM((SUBLANE, tn), F32))
    return pl.pallas_call(
        functools.partial(_proj_body, n_w=n_w, act=act, conv=bool(conv)),
        out_shape=jax.ShapeDtypeStruct((seq, n), out_dtype),
        grid=(n // tn, seq // tm),
        in_specs=[pl.BlockSpec((tm, k), lambda j, i: (i, 0))] + w_specs + extra_specs,
        out_specs=pl.BlockSpec((tm, tn), lambda j, i: (i, j)),
        scratch_shapes=scratch,
        compiler_params=_params("arbitrary", "arbitrary"),
        name=name,
    )(h, *([w3] * n_w), *extra)


def _dt_body(wt_ref, a_ref, o_ref):
    o_ref[...] = lax.dot_general(wt_ref[...], a_ref[...], (((1,), (1,)), ((), ())),
                                 preferred_element_type=F32)


def _dt_proj(h, w_dt_t, tm):
    seq = h.shape[0]
    return pl.pallas_call(
        _dt_body,
        out_shape=jax.ShapeDtypeStruct((M2_HEADS, seq), F32),
        grid=(seq // tm,),
        in_specs=[pl.BlockSpec((M2_HEADS, D_MODEL), lambda i: (0, 0)),
                  pl.BlockSpec((tm, D_MODEL), lambda i: (i, 0))],
        out_specs=pl.BlockSpec((M2_HEADS, tm), lambda i: (0, i)),
        compiler_params=_params("arbitrary"),
        name="dtproj",
    )(w_dt_t, h)


def _lbar(lre, lim, log_step):
    step = jnp.exp(log_step)
    mag = jnp.exp(lre * step)
    return mag * jnp.cos(lim * step), mag * jnp.sin(lim * step)


def _s5prep_body(lre_ref, lim_ref, st_ref, btr_ref, bti_ref, ctr_ref, cti_ref,
                 wb_ref, wct_ref, w2_ref):
    ns = S5_TILE_STATE
    lre, lim = lre_ref[...], lim_ref[...]
    lbr, lbi = _lbar(lre, lim, st_ref[...])
    den = lre * lre + lim * lim
    nr, ni = lbr - 1.0, lbi
    cr = (nr * lre + ni * lim) / den
    ci = (ni * lre - nr * lim) / den
    row = jnp.right_shift(lax.broadcasted_iota(jnp.int32, (LANE, ns), 0), int(math.log2(S5_GROUP)))
    col = jnp.right_shift(lax.broadcasted_iota(jnp.int32, (LANE, ns), 1), int(math.log2(S5_STATE)))
    same = row == col
    btr = jnp.where(same, btr_ref[...], 0.0)
    bti = jnp.where(same, bti_ref[...], 0.0)
    ctr = jnp.where(same, ctr_ref[...], 0.0)
    cti = jnp.where(same, cti_ref[...], 0.0)
    bbr = cr * btr - ci * bti
    bbi = cr * bti + ci * btr
    pows = [(jnp.ones_like(lbr), jnp.zeros_like(lbr))]
    for _ in range(S5_Q):
        pr, pi = pows[-1]
        pows.append((pr * lbr - pi * lbi, pr * lbi + pi * lbr))
    for s in range(S5_Q):
        pr, pi = pows[S5_Q - 1 - s]
        rows = slice(s * LANE, (s + 1) * LANE)
        wb_ref[rows, :ns] = (pr * bbr - pi * bbi).astype(BF16)
        wb_ref[rows, ns:] = (pr * bbi + pi * bbr).astype(BF16)
    for t in range(S5_Q):
        pr, pi = pows[t + 1]
        rows = slice(t * LANE, (t + 1) * LANE)
        wct_ref[rows, :ns] = (ctr * pr - cti * pi).astype(BF16)
        wct_ref[rows, ns:] = (-(ctr * pi + cti * pr)).astype(BF16)
    ccat = jnp.concatenate([ctr, -cti], axis=1).astype(BF16)
    kall = lax.dot_general(wb_ref[...], ccat, (((1,), (1,)), ((), ())),
                           preferred_element_type=F32)
    w2_ref[LANE:, :LANE] = jnp.zeros((LANE, LANE), BF16)
    for tau in range(S5_Q):
        s = S5_Q - 1 - tau
        k_tau = kall[s * LANE:(s + 1) * LANE, :].astype(BF16)
        w2_ref[:LANE, tau * LANE:(tau + 1) * LANE] = k_tau
        if tau + 1 < S5_Q:
            w2_ref[LANE:, (tau + 1) * LANE:(tau + 2) * LANE] = k_tau


def _s5_rows(lam_re, lam_im, log_step):
    row = lambda a: a.astype(F32).reshape(S5_TILES, 1, S5_TILE_STATE)
    return row(lam_re), row(lam_im), row(jnp.broadcast_to(log_step[:, None], lam_re.shape))


def _s5_prep(rows, b_re, b_im, c_re, c_im):
    ns = S5_TILE_STATE

    def tiled(a):
        a = a.astype(F32).reshape(S5_TILES, LANE, S5_STATE)
        return jnp.tile(a, (1, 1, S5_TILE_GROUPS))

    row_spec = pl.BlockSpec((None, 1, ns), lambda j: (j, 0, 0))
    mat_spec = pl.BlockSpec((None, LANE, ns), lambda j: (j, 0, 0))
    return pl.pallas_call(
        _s5prep_body,
        out_shape=(jax.ShapeDtypeStruct((S5_TILES, S5_Q * LANE, 2 * ns), BF16),
                   jax.ShapeDtypeStruct((S5_TILES, S5_Q * LANE, 2 * ns), BF16),
                   jax.ShapeDtypeStruct((S5_TILES, 2 * LANE, S5_Q * LANE), BF16)),
        grid=(S5_TILES,),
        in_specs=[row_spec, row_spec, row_spec, mat_spec, mat_spec, mat_spec, mat_spec],
        out_specs=(pl.BlockSpec((None, S5_Q * LANE, 2 * ns), lambda j: (j, 0, 0)),
                   pl.BlockSpec((None, S5_Q * LANE, 2 * ns), lambda j: (j, 0, 0)),
                   pl.BlockSpec((None, 2 * LANE, S5_Q * LANE), lambda j: (j, 0, 0))),
        compiler_params=_params("arbitrary"),
        name="s5prep",
    )(*rows, tiled(b_re.swapaxes(1, 2)), tiled(b_im.swapaxes(1, 2)), tiled(c_re), tiled(c_im))


S5_SCAN_UNROLL = 8


def _s5_body(u_ref, lre_ref, lim_ref, st_ref, wb_ref, wct_ref, w2_ref, d_ref, o_ref, e_scr, acc, ynat):
    ns = S5_TILE_STATE
    nc = e_scr.shape[0]
    slabs = [u_ref[pl.ds(s, nc, stride=S5_Q), :] for s in range(S5_Q)]
    sb = [x.astype(BF16) for x in slabs]

    e_scr[...] = jnp.dot(jnp.concatenate(sb, axis=1), wb_ref[...], preferred_element_type=F32)
    ar, ai = _lbar(lre_ref[...], lim_ref[...], st_ref[...])
    for _ in range(int(math.log2(S5_Q))):
        ar, ai = ar * ar - ai * ai, 2.0 * ar * ai

    def step(c, carry):
        hr, hi = carry
        er, ei = e_scr[pl.ds(c, 1), :ns], e_scr[pl.ds(c, 1), ns:]
        e_scr[pl.ds(c, 1), :ns] = hr
        e_scr[pl.ds(c, 1), ns:] = hi
        return ar * hr - ai * hi + er, ar * hi + ai * hr + ei

    zero = jnp.zeros((1, ns), F32)
    lax.fori_loop(0, nc, step, (zero, zero), unroll=S5_SCAN_UNROLL)

    acc[...] = lax.dot_general(e_scr[...].astype(BF16), wct_ref[...], (((1,), (1,)), ((), ())),
                               preferred_element_type=F32)
    for s in range(0, S5_Q, 2):
        pair = jnp.concatenate([sb[s], sb[s + 1]], axis=1)
        width = (S5_Q - s) * LANE
        acc[:, s * LANE:] += jnp.dot(pair, w2_ref[:, :width], preferred_element_type=F32)
    d = d_ref[...]
    for t in range(S5_Q):
        y = acc[:, t * LANE:(t + 1) * LANE] + d * slabs[t]
        ynat[pl.ds(t, nc, stride=S5_Q), :] = jax.nn.gelu(y)
    o_ref[...] = ynat[...].astype(BF16)


def _s5(u, rows, wb, wct, w2, d_skip):
    seq = u.shape[0]
    nc, ns = seq // S5_Q, S5_TILE_STATE
    row_spec = pl.BlockSpec((None, 1, ns), lambda j: (j, 0, 0))
    return pl.pallas_call(
        _s5_body,
        out_shape=jax.ShapeDtypeStruct((seq, D_MODEL), BF16),
        grid=(S5_TILES,),
        in_specs=[pl.BlockSpec((seq, LANE), lambda j: (0, j)), row_spec, row_spec, row_spec,
                  pl.BlockSpec((None, S5_Q * LANE, 2 * ns), lambda j: (j, 0, 0)),
                  pl.BlockSpec((None, S5_Q * LANE, 2 * ns), lambda j: (j, 0, 0)),
                  pl.BlockSpec((None, 2 * LANE, S5_Q * LANE), lambda j: (j, 0, 0)),
                  pl.BlockSpec((1, LANE), lambda j: (0, j))],
        out_specs=pl.BlockSpec((seq, LANE), lambda j: (0, j)),
        scratch_shapes=[pltpu.VMEM((nc, 2 * ns), F32), pltpu.VMEM((nc, S5_Q * LANE), F32),
                        pltpu.VMEM((seq, LANE), F32)],
        compiler_params=_params("arbitrary"),
        name="s5",
    )(u, *rows, wb, wct, w2, d_skip.astype(F32).reshape(1, D_MODEL))


def _glu_body(a_ref, wa_ref, wb_ref, ba_ref, bb_ref, z_ref, o_ref, wbf):
    tn = wa_ref.shape[1]

    @pl.when(pl.program_id(1) == 0)
    def _():
        wbf[:, :tn] = wa_ref[...].astype(BF16)
        wbf[:, tn:] = wb_ref[...].astype(BF16)

    g = jnp.dot(a_ref[...], wbf[...], preferred_element_type=F32)
    ga = g[:, :tn] + ba_ref[...]
    gb = g[:, tn:] + bb_ref[...]
    o_ref[...] = (ga * jax.nn.sigmoid(gb) * z_ref[...].astype(F32)).astype(BF16)


def _glu(ys5, w_glu3, b_glu, zs, tm, tn):
    seq = ys5.shape[0]
    nb = D_MODEL // tn
    return pl.pallas_call(
        _glu_body,
        out_shape=jax.ShapeDtypeStruct((seq, D_MODEL), BF16),
        grid=(nb, seq // tm),
        in_specs=[pl.BlockSpec((tm, D_MODEL), lambda j, i: (i, 0)),
                  pl.BlockSpec((None, D_MODEL, tn), lambda j, i: (0, 0, j)),
                  pl.BlockSpec((None, D_MODEL, tn), lambda j, i: (0, 0, nb + j)),
                  pl.BlockSpec((1, tn), lambda j, i: (0, j)),
                  pl.BlockSpec((1, tn), lambda j, i: (0, nb + j)),
                  pl.BlockSpec((tm, tn), lambda j, i: (i, Z_S5 // tn + j))],
        out_specs=pl.BlockSpec((tm, tn), lambda j, i: (i, j)),
        scratch_shapes=[pltpu.VMEM((D_MODEL, 2 * tn), BF16)],
        compiler_params=_params("arbitrary", "arbitrary"),
        name="glu",
    )(ys5, w_glu3, w_glu3, b_glu, b_glu, zs)


def _split3(x):
    hi = x.astype(BF16)
    r1 = x - hi.astype(F32)
    mid = r1.astype(BF16)
    lo = (r1 - mid.astype(F32)).astype(BF16)
    return hi, mid, lo


def _split2(x):
    hi = x.astype(BF16)
    mid = (x - hi.astype(F32)).astype(BF16)
    return jnp.concatenate([hi, mid], axis=1)


def _dtprep_body(dt_ref, dtb_ref, alog_ref, tri_ref, arow_ref, acol_ref, ehm_ref):
    T = M2_CHUNK
    dt = jax.nn.softplus(dt_ref[...] + dtb_ref[...])
    dta = dt * (-jnp.exp(alog_ref[...]))
    hi, mid, lo = _split3(dta)
    tri = tri_ref[...]
    acum = (jnp.dot(hi, tri, preferred_element_type=F32) + jnp.dot(mid, tri, preferred_element_type=F32)
            + jnp.dot(lo, tri, preferred_element_type=F32))
    rem = acum[:, T - 1:T] - acum
    arow_ref[...] = acum - jnp.log(dt)
    dte = dt * jnp.exp(rem)
    eac = jnp.exp(acum)
    pad1 = jnp.zeros((T - M2_HPG, T), F32)
    pad2 = jnp.zeros((T - 2 * M2_HPG, T), F32)
    for g in range(M2_GROUPS):
        hs = slice(g * M2_HPG, (g + 1) * M2_HPG)
        acol_ref[g] = jnp.concatenate([acum[hs], pad1], axis=0).T
        ehm_ref[g] = _split2(jnp.concatenate([dte[hs], eac[hs], pad2], axis=0).T)


def _dt_prep(dt_t, dt_bias, a_log):
    seq = dt_t.shape[1]
    T = M2_CHUNK
    lanes = lambda v: jnp.broadcast_to(v.astype(F32)[:, None], (M2_HEADS, T))
    tri = (jnp.arange(T)[:, None] <= jnp.arange(T)[None, :]).astype(BF16)
    return pl.pallas_call(
        _dtprep_body,
        out_shape=(jax.ShapeDtypeStruct((M2_HEADS, seq), F32),
                   jax.ShapeDtypeStruct((M2_GROUPS, seq, LANE), F32),
                   jax.ShapeDtypeStruct((M2_GROUPS, seq, 2 * LANE), BF16)),
        grid=(seq // T,),
        in_specs=[pl.BlockSpec((M2_HEADS, T), lambda c: (0, c)),
                  pl.BlockSpec((M2_HEADS, T), lambda c: (0, 0)),
                  pl.BlockSpec((M2_HEADS, T), lambda c: (0, 0)),
                  pl.BlockSpec((T, T), lambda c: (0, 0))],
        out_specs=(pl.BlockSpec((M2_HEADS, T), lambda c: (0, c)),
                   pl.BlockSpec((M2_GROUPS, T, LANE), lambda c: (0, c, 0)),
                   pl.BlockSpec((M2_GROUPS, T, 2 * LANE), lambda c: (0, c, 0))),
        compiler_params=_params("arbitrary"),
        name="dtprep",
    )(dt_t, lanes(dt_bias), lanes(a_log), tri)


def _ssd_body(x_ref, b_ref, c_ref, z_ref, arow_ref, acol_ref, ehm_ref, exp_ref, dskip_ref, ng_ref, o_ref, state):
    T = M2_CHUNK

    @pl.when(pl.program_id(1) == 0)
    def _():
        state[...] = jnp.zeros_like(state)

    xb = x_ref[...]
    xs = xb.astype(F32)
    bm_b, cm_b = b_ref[...], c_ref[...]
    arow, acol = arow_ref[...], acol_ref[...]
    ehm = ehm_ref[...]
    dte_e = jnp.dot(ehm, exp_ref[0], preferred_element_type=F32)
    eac_e = jnp.dot(ehm, exp_ref[1], preferred_element_type=F32)

    scores = lax.dot_general(cm_b, bm_b, (((1,), (1,)), ((), ())), preferred_element_type=F32)
    y_off = jnp.dot(cm_b, state[...].astype(BF16), preferred_element_type=F32) * eac_e

    causal = (lax.broadcasted_iota(jnp.int32, (T, T), 0) >= lax.broadcasted_iota(jnp.int32, (T, T), 1))
    left = lax.broadcasted_iota(jnp.int32, (T, LANE), 1) < M2_HEADDIM
    zero = jnp.zeros((T, LANE), BF16)
    ys = []
    for p in range(M2_HPG // 2):
        lhs = []
        for h in (2 * p, 2 * p + 1):
            seg = jnp.broadcast_to(acol[:, h:h + 1], (T, T)) - jnp.broadcast_to(arow[h:h + 1, :], (T, T))
            lhs.append((scores * jnp.exp(jnp.where(causal, seg, -1e30))).astype(BF16))
        xp = xb[:, p * LANE:(p + 1) * LANE]
        rhs = jnp.concatenate([jnp.where(left, xp, zero), jnp.where(left, zero, xp)], axis=0)
        ys.append(jnp.dot(jnp.concatenate(lhs, axis=1), rhs, preferred_element_type=F32))
    y = jnp.concatenate(ys, axis=1) + y_off + dskip_ref[...] * xs

    xw = (xs * dte_e).astype(BF16)
    state[...] = (state[...] * eac_e[T - 1:T, :]
                  + jnp.dot(bm_b.astype(F32).T.astype(BF16), xw, preferred_element_type=F32))

    y = y * z_ref[...].astype(F32)
    o_ref[...] = (_rms(y) * ng_ref[...]).astype(BF16)


def _ssd(xbc, zs, arow, acol, ehm, d_skip, norm_g):
    seq = xbc.shape[0]
    T, gw, n = M2_CHUNK, M2_GW, M2_STATE
    bb = M2_D_INNER // n
    cb = bb + M2_GROUPS
    col = jnp.arange(2 * LANE)[:, None] % LANE
    head = jnp.arange(gw)[None, :] // M2_HEADDIM
    expand = jnp.stack([col == head, col == head + M2_HPG]).astype(BF16)
    return pl.pallas_call(
        _ssd_body,
        out_shape=jax.ShapeDtypeStruct((seq, M2_D_INNER), BF16),
        grid=(M2_GROUPS, seq // T),
        in_specs=[pl.BlockSpec((T, gw), lambda g, c: (c, g)),
                  pl.BlockSpec((T, n), lambda g, c: (c, bb + g)),
                  pl.BlockSpec((T, n), lambda g, c: (c, cb + g)),
                  pl.BlockSpec((T, gw), lambda g, c: (c, Z_M2 // gw + g)),
                  pl.BlockSpec((M2_HPG, T), lambda g, c: (g, c)),
                  pl.BlockSpec((None, T, LANE), lambda g, c: (g, c, 0)),
                  pl.BlockSpec((None, T, 2 * LANE), lambda g, c: (g, c, 0)),
                  pl.BlockSpec((2, 2 * LANE, gw), lambda g, c: (0, 0, 0)),
                  pl.BlockSpec((1, gw), lambda g, c: (0, g)),
                  pl.BlockSpec((1, gw), lambda g, c: (0, g))],
        out_specs=pl.BlockSpec((T, gw), lambda g, c: (c, g)),
        scratch_shapes=[pltpu.VMEM((n, gw), F32)],
        compiler_params=_params("arbitrary", "arbitrary"),
        name="ssd",
    )(xbc, xbc, xbc, zs, arow, acol, ehm, expand,
      jnp.repeat(d_skip.astype(F32), M2_HEADDIM).reshape(1, M2_D_INNER),
      norm_g.astype(F32).reshape(1, M2_D_INNER))


def _merge_body(a1_ref, w1_ref, a2_ref, w2_ref, g1_ref, g2_ref, o_ref):
    m1 = jnp.dot(a1_ref[...], w1_ref[...], preferred_element_type=F32)
    m2 = jnp.dot(a2_ref[...], w2_ref[...], preferred_element_type=F32)
    o_ref[...] = (g1_ref[...].astype(F32) * m1 + g2_ref[...].astype(F32) * m2).astype(BF16)


def _merge(y1, w1, y2, w2, gates, tm, tn):
    seq = y1.shape[0]
    nb = D_MODEL // tn
    return pl.pallas_call(
        _merge_body,
        out_shape=jax.ShapeDtypeStruct((seq, D_MODEL), BF16),
        grid=(seq // tm, nb),
        in_specs=[pl.BlockSpec((tm, D_MODEL), lambda i, j: (i, 0)),
                  pl.BlockSpec((D_MODEL, tn), lambda i, j: (0, j)),
                  pl.BlockSpec((tm, M2_D_INNER), lambda i, j: (i, 0)),
                  pl.BlockSpec((M2_D_INNER, tn), lambda i, j: (0, j)),
                  pl.BlockSpec((tm, tn), lambda i, j: (i, j)),
                  pl.BlockSpec((tm, tn), lambda i, j: (i, nb + j))],
        out_specs=pl.BlockSpec((tm, tn), lambda i, j: (i, j)),
        compiler_params=_params("arbitrary", "arbitrary"),
        name="merge",
    )(y1, w1, y2, w2, gates, gates)


def _out_body(a_ref, w_ref, x_ref, gate_ref, o_ref, wbf):
    @pl.when(pl.program_id(1) == 0)
    def _():
        wbf[...] = w_ref[...].astype(BF16)

    m = jnp.dot(a_ref[...], wbf[...], preferred_element_type=F32)
    o_ref[...] = x_ref[...] + gate_ref[...] * m


def _out_proj(merged, w_out3, x2, mod, tm, tn):
    seq = merged.shape[0]
    nb = D_MODEL // tn
    return pl.pallas_call(
        _out_body,
        out_shape=jax.ShapeDtypeStruct((seq, D_MODEL), F32),
        grid=(nb, seq // tm),
        in_specs=[pl.BlockSpec((tm, D_MODEL), lambda j, i: (i, 0)),
                  pl.BlockSpec((None, D_MODEL, tn), lambda j, i: (0, 0, j)),
                  pl.BlockSpec((tm, tn), lambda j, i: (i, j)),
                  pl.BlockSpec((1, tn), lambda j, i: (0, 2 * nb + j))],
        out_specs=pl.BlockSpec((tm, tn), lambda j, i: (i, j)),
        scratch_shapes=[pltpu.VMEM((D_MODEL, tn), BF16)],
        compiler_params=_params("arbitrary", "arbitrary"),
        name="outproj",
    )(merged, w_out3, x2, mod)


def _fnorm_body(x_ref, g_ref, o_ref):
    o_ref[...] = _rms(x_ref[...]) * g_ref[...]


def _final_norm(x2, g, tm):
    seq = x2.shape[0]
    return pl.pallas_call(
        _fnorm_body,
        out_shape=jax.ShapeDtypeStruct((seq, D_MODEL), F32),
        grid=(seq // tm,),
        in_specs=[pl.BlockSpec((tm, D_MODEL), lambda i: (i, 0)),
                  pl.BlockSpec((1, D_MODEL), lambda i: (0, 0))],
        out_specs=pl.BlockSpec((tm, D_MODEL), lambda i: (i, 0)),
        compiler_params=_params("arbitrary"),
        name="fnorm",
    )(x2, g)


def _tiles(seq):
    return dict(tm=min(1024, seq), tm_merge=min(512, seq), tm_norm=min(256, seq), tn=512, tn_glu=256)


def kernel(x, c, w_ada, b_ada, norm_g, w_in, s5_lambda_re, s5_lambda_im, s5_log_step, s5_b_re, s5_b_im,
           s5_c_re, s5_c_im, s5_d, s5_w_glu, s5_b_glu, m2_conv_w, m2_conv_b, m2_dt_bias, m2_a_log, m2_d,
           m2_norm_g, w_br_s5, w_br_m2, w_out, final_g):
    bsz, seq, _ = x.shape
    assert bsz == 1 and w_ada.shape[0] == 1 and seq % (S5_Q * SUBLANE) == 0 and seq % M2_CHUNK == 0
    tl = _tiles(seq)
    tm, tn = tl["tm"], tl["tn"]
    x2 = x.reshape(seq, D_MODEL)

    w_dt_t = w_in[0][:, WIN_DT:WIN_GATES].T.astype(BF16)
    w1, w2 = w_br_s5[0].astype(BF16), w_br_m2[0].astype(BF16)
    conv = (m2_conv_w[0].astype(F32), m2_conv_b[0].astype(F32).reshape(1, M2_CONV_DIM))

    mod = _ada_mod(c, w_ada[0], b_ada[0])
    h = _norm_mod(x2, norm_g[0].reshape(1, D_MODEL), mod, tl["tm_norm"])
    u = _proj(h, w_in, 0, D_MODEL, F32, tm, tn, "inproj_u")
    zs = _proj(h, w_in, WIN_S5Z, WIN_XBC - WIN_S5Z, BF16, tm, tn, "inproj_z", act=jax.nn.silu)
    xbc = _proj(h, w_in, WIN_XBC, M2_CONV_DIM, BF16, tm, tn, "inproj_x", conv=conv)
    gates = _proj(h, w_in, WIN_GATES, 2 * D_MODEL, BF16, tm, tn, "inproj_g", act=jax.nn.sigmoid, wsub=LANE)
    dt_t = _dt_proj(h, w_dt_t, tm)

    rows = _s5_rows(s5_lambda_re[0], s5_lambda_im[0], s5_log_step[0])
    wb, wct, w2t = _s5_prep(rows, s5_b_re[0], s5_b_im[0], s5_c_re[0], s5_c_im[0])
    ys5 = _s5(u, rows, wb, wct, w2t, s5_d[0])
    y1 = _glu(ys5, s5_w_glu, s5_b_glu[0].astype(F32).reshape(1, -1), zs, tm, tl["tn_glu"])

    arow, acol, ehm = _dt_prep(dt_t, m2_dt_bias[0], m2_a_log[0])
    y2 = _ssd(xbc, zs, arow, acol, ehm, m2_d[0], m2_norm_g[0])

    merged = _merge(y1, w1, y2, w2, gates, tl["tm_merge"], tn)
    xn = _out_proj(merged, w_out, x2, mod, tm, tn)
    out = _final_norm(xn, final_g.astype(F32).reshape(1, D_MODEL), tl["tm_norm"])
    return out.reshape(bsz, seq, D_MODEL)
```

```python
import functools
import math

import jax
import jax.numpy as jnp
from jax import lax
from jax.experimental import pallas as pl
from jax.experimental.pallas import tpu as pltpu

F32 = jnp.float32
BF16 = jnp.bfloat16

LANE = 128
SUBLANE = 8
VMEM_LIMIT_BYTES = 56 * 1024 * 1024

D_MODEL = 4096
EPS = 1e-6

S5_GROUP = 16
S5_STATE = 64
S5_Q = 16
S5_TILE_GROUPS = LANE // S5_GROUP
S5_TILE_STATE = S5_TILE_GROUPS * S5_STATE
S5_TILES = D_MODEL // LANE

M2_D_INNER = 2 * D_MODEL
M2_HEADDIM = 64
M2_HEADS = M2_D_INNER // M2_HEADDIM
M2_GROUPS = 8
M2_HPG = M2_HEADS // M2_GROUPS
M2_STATE = 128
M2_CONV = 4
M2_CHUNK = 128
M2_GW = M2_HPG * M2_HEADDIM
M2_GN = M2_GROUPS * M2_STATE
M2_CONV_DIM = M2_D_INNER + 2 * M2_GN

WIN_S5Z = D_MODEL
WIN_XBC = 2 * D_MODEL + M2_D_INNER
WIN_DT = WIN_XBC + M2_CONV_DIM
WIN_GATES = WIN_DT + M2_HEADS
Z_S5 = 0
Z_M2 = D_MODEL


def _params(*sem):
    return pltpu.CompilerParams(dimension_semantics=sem, vmem_limit_bytes=VMEM_LIMIT_BYTES)


MOD_TN = 512


def _mod_body(cb_ref, w_ref, b_ref, o_ref):
    cb = cb_ref[...]
    for q in range(MOD_TN // LANE):
        sl = slice(q * LANE, (q + 1) * LANE)
        o_ref[:, sl] = jnp.sum(w_ref[:, sl] * cb, axis=0, keepdims=True) + b_ref[:, sl]


def _ada_mod(c, w_ada, b_ada):
    n = w_ada.shape[1]
    cb = jnp.broadcast_to(c.reshape(D_MODEL, 1), (D_MODEL, LANE))
    return pl.pallas_call(
        _mod_body,
        out_shape=jax.ShapeDtypeStruct((1, n), F32),
        grid=(n // MOD_TN,),
        in_specs=[pl.BlockSpec((D_MODEL, LANE), lambda j: (0, 0)),
                  pl.BlockSpec((D_MODEL, MOD_TN), lambda j: (0, j)),
                  pl.BlockSpec((1, MOD_TN), lambda j: (0, j))],
        out_specs=pl.BlockSpec((1, MOD_TN), lambda j: (0, j)),
        compiler_params=_params("arbitrary"),
        name="mod",
    )(cb, w_ada, b_ada.reshape(1, n))


def _rms(x):
    return x * lax.rsqrt(jnp.mean(x * x, axis=-1, keepdims=True) + EPS)


def _norm_body(x_ref, g_ref, shift_ref, scale_ref, o_ref):
    h = _rms(x_ref[...]) * g_ref[...] * (1.0 + scale_ref[...]) + shift_ref[...]
    o_ref[...] = h.astype(BF16)


def _norm_mod(x2, g, mod, tm):
    seq = x2.shape[0]
    return pl.pallas_call(
        _norm_body,
        out_shape=jax.ShapeDtypeStruct((seq, D_MODEL), BF16),
        grid=(seq // tm,),
        in_specs=[pl.BlockSpec((tm, D_MODEL), lambda i: (i, 0)),
                  pl.BlockSpec((1, D_MODEL), lambda i: (0, 0)),
                  pl.BlockSpec((1, D_MODEL), lambda i: (0, 0)),
                  pl.BlockSpec((1, D_MODEL), lambda i: (0, 1))],
        out_specs=pl.BlockSpec((tm, D_MODEL), lambda i: (i, 0)),
        compiler_params=_params("arbitrary"),
        name="norm",
    )(x2, g, mod, mod)


MXU_COLS = 256


def _proj_body(*refs, n_w, act, conv):
    a_ref, w_refs, rest = refs[0], refs[1:1 + n_w], refs[1 + n_w:]
    if conv:
        cw_ref, cb_ref, o_ref, wbf, cbuf = rest
    else:
        o_ref, wbf = rest
    wsub = w_refs[0].shape[1]
    tm, tn = o_ref.shape

    @pl.when(pl.program_id(1) == 0)
    def _():
        for q, w_ref in enumerate(w_refs):
            wbf[:, q * wsub:(q + 1) * wsub] = w_ref[...].astype(BF16)
        if conv:
            cbuf[:SUBLANE, :] = jnp.zeros((SUBLANE, tn), F32)

    for s in range(tn // MXU_COLS):
        cols = slice(s * MXU_COLS, (s + 1) * MXU_COLS)
        r = jnp.dot(a_ref[...], wbf[:, cols], preferred_element_type=F32)
        if conv:
            k = cw_ref.shape[0]
            cbuf[SUBLANE:, cols] = r
            y = cb_ref[:, cols] + cw_ref[k - 1:k, cols] * r
            for d in range(1, k):
                y = y + cw_ref[k - 1 - d:k - d, cols] * cbuf[pl.ds(SUBLANE - d, tm), cols]
            o_ref[:, cols] = jax.nn.silu(y).astype(o_ref.dtype)
            cbuf[:SUBLANE, cols] = r[tm - SUBLANE:]
        else:
            o_ref[:, cols] = (act(r) if act else r).astype(o_ref.dtype)


def _proj(h, w3, col0, n, out_dtype, tm, tn, name, act=None, wsub=None, conv=None):
    seq, k = h.shape
    wsub = wsub or tn
    n_w = tn // wsub
    assert col0 % wsub == 0 and n % tn == 0 and tn % wsub == 0
    w_specs = [pl.BlockSpec((None, k, wsub), lambda j, i, q=q: (0, 0, col0 // wsub + j * n_w + q))
               for q in range(n_w)]
    extra, extra_specs, scratch = [], [], [pltpu.VMEM((k, tn), BF16)]
    if conv:
        extra = list(conv)
        extra_specs = [pl.BlockSpec((conv[0].shape[0], tn), lambda j, i: (0, j)),
                       pl.BlockSpec((1, tn), lambda j, i: (0, j))]
        scratch.append(pltpu.VMEM((tm + SUBLANE, tn), F32))
    return pl.pallas_call(
        functools.partial(_proj_body, n_w=n_w, act=act, conv=bool(conv)),
        out_shape=jax.ShapeDtypeStruct((seq, n), out_dtype),
        grid=(n // tn, seq // tm),
        in_specs=[pl.BlockSpec((tm, k), lambda j, i: (i, 0))] + w_specs + extra_specs,
        out_specs=pl.BlockSpec((tm, tn), lambda j, i: (i, j)),
        scratch_shapes=scratch,
        compiler_params=_params("arbitrary", "arbitrary"),
        name=name,
    )(h, *([w3] * n_w), *extra)


def _lbar(lre, lim, log_step):
    step = jnp.exp(log_step)
    mag = jnp.exp(lre * step)
    return mag * jnp.cos(lim * step), mag * jnp.sin(lim * step)


def _s5prep_body(lre_ref, lim_ref, st_ref, btr_ref, bti_ref, ctr_ref, cti_ref,
                 wb_ref, wct_ref, w2_ref):
    ns = S5_TILE_STATE
    lre, lim = lre_ref[...], lim_ref[...]
    lbr, lbi = _lbar(lre, lim, st_ref[...])
    den = lre * lre + lim * lim
    nr, ni = lbr - 1.0, lbi
    cr = (nr * lre + ni * lim) / den
    ci = (ni * lre - nr * lim) / den
    row = jnp.right_shift(lax.broadcasted_iota(jnp.int32, (LANE, ns), 0), int(math.log2(S5_GROUP)))
    col = jnp.right_shift(lax.broadcasted_iota(jnp.int32, (LANE, ns), 1), int(math.log2(S5_STATE)))
    same = row == col
    btr = jnp.where(same, btr_ref[...], 0.0)
    bti = jnp.where(same, bti_ref[...], 0.0)
    ctr = jnp.where(same, ctr_ref[...], 0.0)
    cti = jnp.where(same, cti_ref[...], 0.0)
    bbr = cr * btr - ci * bti
    bbi = cr * bti + ci * btr
    pows = [(jnp.ones_like(lbr), jnp.zeros_like(lbr))]
    for _ in range(S5_Q):
        pr, pi = pows[-1]
        pows.append((pr * lbr - pi * lbi, pr * lbi + pi * lbr))
    for s in range(S5_Q):
        pr, pi = pows[S5_Q - 1 - s]
        rows = slice(s * LANE, (s + 1) * LANE)
        wb_ref[rows, :ns] = (pr * bbr - pi * bbi).astype(BF16)
        wb_ref[rows, ns:] = (pr * bbi + pi * bbr).astype(BF16)
    for t in range(S5_Q):
        pr, pi = pows[t + 1]
        rows = slice(t * LANE, (t + 1) * LANE)
        wct_ref[rows, :ns] = (ctr * pr - cti * pi).astype(BF16)
        wct_ref[rows, ns:] = (-(ctr * pi + cti * pr)).astype(BF16)
    ccat = jnp.concatenate([ctr, -cti], axis=1).astype(BF16)
    kall = lax.dot_general(wb_ref[...], ccat, (((1,), (1,)), ((), ())),
                           preferred_element_type=F32)
    w2_ref[LANE:, :LANE] = jnp.zeros((LANE, LANE), BF16)
    for tau in range(S5_Q):
        s = S5_Q - 1 - tau
        k_tau = kall[s * LANE:(s + 1) * LANE, :].astype(BF16)
        w2_ref[:LANE, tau * LANE:(tau + 1) * LANE] = k_tau
        if tau + 1 < S5_Q:
            w2_ref[LANE:, (tau + 1) * LANE:(tau + 2) * LANE] = k_tau


def _s5_rows(lam_re, lam_im, log_step):
    row = lambda a: a.astype(F32).reshape(S5_TILES, 1, S5_TILE_STATE)
    return row(lam_re), row(lam_im), row(jnp.broadcast_to(log_step[:, None], lam_re.shape))


def _s5_prep(rows, b_re, b_im, c_re, c_im):
    ns = S5_TILE_STATE

    def tiled(a):
        a = a.astype(F32).reshape(S5_TILES, LANE, S5_STATE)
        return jnp.tile(a, (1, 1, S5_TILE_GROUPS))

    row_spec = pl.BlockSpec((None, 1, ns), lambda j: (j, 0, 0))
    mat_spec = pl.BlockSpec((None, LANE, ns), lambda j: (j, 0, 0))
    return pl.pallas_call(
        _s5prep_body,
        out_shape=(jax.ShapeDtypeStruct((S5_TILES, S5_Q * LANE, 2 * ns), BF16),
                   jax.ShapeDtypeStruct((S5_TILES, S5_Q * LANE, 2 * ns), BF16),
                   jax.ShapeDtypeStruct((S5_TILES, 2 * LANE, S5_Q * LANE), BF16)),
        grid=(S5_TILES,),
        in_specs=[row_spec, row_spec, row_spec, mat_spec, mat_spec, mat_spec, mat_spec],
        out_specs=(pl.BlockSpec((None, S5_Q * LANE, 2 * ns), lambda j: (j, 0, 0)),
                   pl.BlockSpec((None, S5_Q * LANE, 2 * ns), lambda j: (j, 0, 0)),
                   pl.BlockSpec((None, 2 * LANE, S5_Q * LANE), lambda j: (j, 0, 0))),
        compiler_params=_params("arbitrary"),
        name="s5prep",
    )(*rows, tiled(b_re.swapaxes(1, 2)), tiled(b_im.swapaxes(1, 2)), tiled(c_re), tiled(c_im))


S5_SCAN_UNROLL = 8


def _s5_body(u_ref, lre_ref, lim_ref, st_ref, wb_ref, wct_ref, w2_ref, d_ref, o_ref, e_scr, acc, ynat):
    ns = S5_TILE_STATE
    nc = e_scr.shape[0]
    slabs = [u_ref[pl.ds(s, nc, stride=S5_Q), :] for s in range(S5_Q)]
    sb = [x.astype(BF16) for x in slabs]

    e_scr[...] = jnp.dot(jnp.concatenate(sb, axis=1), wb_ref[...], preferred_element_type=F32)
    ar, ai = _lbar(lre_ref[...], lim_ref[...], st_ref[...])
    for _ in range(int(math.log2(S5_Q))):
        ar, ai = ar * ar - ai * ai, 2.0 * ar * ai

    def step(c, carry):
        hr, hi = carry
        er, ei = e_scr[pl.ds(c, 1), :ns], e_scr[pl.ds(c, 1), ns:]
        e_scr[pl.ds(c, 1), :ns] = hr
        e_scr[pl.ds(c, 1), ns:] = hi
        return ar * hr - ai * hi + er, ar * hi + ai * hr + ei

    zero = jnp.zeros((1, ns), F32)
    lax.fori_loop(0, nc, step, (zero, zero), unroll=S5_SCAN_UNROLL)

    acc[...] = lax.dot_general(e_scr[...].astype(BF16), wct_ref[...], (((1,), (1,)), ((), ())),
                               preferred_element_type=F32)
    for s in range(0, S5_Q, 2):
        pair = jnp.concatenate([sb[s], sb[s + 1]], axis=1)
        width = (S5_Q - s) * LANE
        acc[:, s * LANE:] += jnp.dot(pair, w2_ref[:, :width], preferred_element_type=F32)
    d = d_ref[...]
    for t in range(S5_Q):
        y = acc[:, t * LANE:(t + 1) * LANE] + d * slabs[t]
        ynat[pl.ds(t, nc, stride=S5_Q), :] = jax.nn.gelu(y)
    o_ref[...] = ynat[...].astype(BF16)


def _s5(u, rows, wb, wct, w2, d_skip):
    seq = u.shape[0]
    nc, ns = seq // S5_Q, S5_TILE_STATE
    row_spec = pl.BlockSpec((None, 1, ns), lambda j: (j, 0, 0))
    return pl.pallas_call(
        _s5_body,
        out_shape=jax.ShapeDtypeStruct((seq, D_MODEL), BF16),
        grid=(S5_TILES,),
        in_specs=[pl.BlockSpec((seq, LANE), lambda j: (0, j)), row_spec, row_spec, row_spec,
                  pl.BlockSpec((None, S5_Q * LANE, 2 * ns), lambda j: (j, 0, 0)),
                  pl.BlockSpec((None, S5_Q * LANE, 2 * ns), lambda j: (j, 0, 0)),
                  pl.BlockSpec((None, 2 * LANE, S5_Q * LANE), lambda j: (j, 0, 0)),
                  pl.BlockSpec((1, LANE), lambda j: (0, j))],
        out_specs=pl.BlockSpec((seq, LANE), lambda j: (0, j)),
        scratch_shapes=[pltpu.VMEM((nc, 2 * ns), F32), pltpu.VMEM((nc, S5_Q * LANE), F32),
                        pltpu.VMEM((seq, LANE), F32)],
        compiler_params=_params("arbitrary"),
        name="s5",
    )(u, *rows, wb, wct, w2, d_skip.astype(F32).reshape(1, D_MODEL))


def _glu_body(a_ref, wa_ref, wb_ref, ba_ref, bb_ref, z_ref, o_ref, wbf):
    tn = wa_ref.shape[1]

    @pl.when(pl.program_id(1) == 0)
    def _():
        wbf[:, :tn] = wa_ref[...].astype(BF16)
        wbf[:, tn:] = wb_ref[...].astype(BF16)

    g = jnp.dot(a_ref[...], wbf[...], preferred_element_type=F32)
    ga = g[:, :tn] + ba_ref[...]
    gb = g[:, tn:] + bb_ref[...]
    o_ref[...] = (ga * jax.nn.sigmoid(gb) * z_ref[...].astype(F32)).astype(BF16)


def _glu(ys5, w_glu3, b_glu, zs, tm, tn):
    seq = ys5.shape[0]
    nb = D_MODEL // tn
    return pl.pallas_call(
        _glu_body,
        out_shape=jax.ShapeDtypeStruct((seq, D_MODEL), BF16),
        grid=(nb, seq // tm),
        in_specs=[pl.BlockSpec((tm, D_MODEL), lambda j, i: (i, 0)),
                  pl.BlockSpec((None, D_MODEL, tn), lambda j, i: (0, 0, j)),
                  pl.BlockSpec((None, D_MODEL, tn), lambda j, i: (0, 0, nb + j)),
                  pl.BlockSpec((1, tn), lambda j, i: (0, j)),
                  pl.BlockSpec((1, tn), lambda j, i: (0, nb + j)),
                  pl.BlockSpec((tm, tn), lambda j, i: (i, Z_S5 // tn + j))],
        out_specs=pl.BlockSpec((tm, tn), lambda j, i: (i, j)),
        scratch_shapes=[pltpu.VMEM((D_MODEL, 2 * tn), BF16)],
        compiler_params=_params("arbitrary", "arbitrary"),
        name="glu",
    )(ys5, w_glu3, w_glu3, b_glu, b_glu, zs)


def _split3(x):
    hi = x.astype(BF16)
    r1 = x - hi.astype(F32)
    mid = r1.astype(BF16)
    lo = (r1 - mid.astype(F32)).astype(BF16)
    return hi, mid, lo


def _split2(x):
    hi = x.astype(BF16)
    mid = (x - hi.astype(F32)).astype(BF16)
    return jnp.concatenate([hi, mid], axis=1)


def _dtprep_body(h_ref, w_ref, dtb_ref, alog_ref, tri_ref, arow_ref, acol_ref, ehm_ref, wbf):
    T = M2_CHUNK

    @pl.when(pl.program_id(0) == 0)
    def _():
        wbf[...] = w_ref[...].astype(BF16)

    raw = jnp.dot(h_ref[...].astype(F32), wbf[...].astype(F32), preferred_element_type=F32)
    dt = jax.nn.softplus(raw.T + dtb_ref[...])
    dta = dt * (-jnp.exp(alog_ref[...]))
    hi, mid, lo = _split3(dta)
    tri = tri_ref[...]
    acum = (jnp.dot(hi, tri, preferred_element_type=F32) + jnp.dot(mid, tri, preferred_element_type=F32)
            + jnp.dot(lo, tri, preferred_element_type=F32))
    rem = acum[:, T - 1:T] - acum
    arow_ref[...] = acum - jnp.log(dt)
    dte = dt * jnp.exp(rem)
    eac = jnp.exp(acum)
    pad1 = jnp.zeros((T - M2_HPG, T), F32)
    pad2 = jnp.zeros((T - 2 * M2_HPG, T), F32)
    for g in range(M2_GROUPS):
        hs = slice(g * M2_HPG, (g + 1) * M2_HPG)
        acol_ref[g] = jnp.concatenate([acum[hs], pad1], axis=0).T
        ehm_ref[g] = _split2(jnp.concatenate([dte[hs], eac[hs], pad2], axis=0).T)


def _dt_prep(h, w_in3, dt_bias, a_log):
    seq = h.shape[0]
    T = M2_CHUNK
    lanes = lambda v: jnp.broadcast_to(v.astype(F32)[:, None], (M2_HEADS, T))
    tri = (jnp.arange(T)[:, None] <= jnp.arange(T)[None, :]).astype(BF16)
    return pl.pallas_call(
        _dtprep_body,
        out_shape=(jax.ShapeDtypeStruct((M2_HEADS, seq), F32),
                   jax.ShapeDtypeStruct((M2_GROUPS, seq, LANE), F32),
                   jax.ShapeDtypeStruct((M2_GROUPS, seq, 2 * LANE), BF16)),
        grid=(seq // T,),
        in_specs=[pl.BlockSpec((T, D_MODEL), lambda c: (c, 0)),
                  pl.BlockSpec((None, D_MODEL, M2_HEADS), lambda c: (0, 0, WIN_DT // M2_HEADS)),
                  pl.BlockSpec((M2_HEADS, T), lambda c: (0, 0)),
                  pl.BlockSpec((M2_HEADS, T), lambda c: (0, 0)),
                  pl.BlockSpec((T, T), lambda c: (0, 0))],
        out_specs=(pl.BlockSpec((M2_HEADS, T), lambda c: (0, c)),
                   pl.BlockSpec((M2_GROUPS, T, LANE), lambda c: (0, c, 0)),
                   pl.BlockSpec((M2_GROUPS, T, 2 * LANE), lambda c: (0, c, 0))),
        scratch_shapes=[pltpu.VMEM((D_MODEL, M2_HEADS), BF16)],
        compiler_params=_params("arbitrary"),
        name="dtprep",
    )(h, w_in3, lanes(dt_bias), lanes(a_log), tri)


def _ssd_body(x_ref, b_ref, c_ref, z_ref, arow_ref, acol_ref, ehm_ref, exp_ref, dskip_ref, ng_ref, o_ref, state):
    T = M2_CHUNK

    @pl.when(pl.program_id(1) == 0)
    def _():
        state[...] = jnp.zeros_like(state)

    xb = x_ref[...]
    xs = xb.astype(F32)
    bm_b, cm_b = b_ref[...], c_ref[...]
    arow, acol = arow_ref[...], acol_ref[...]
    ehm = ehm_ref[...]
    dte_e = jnp.dot(ehm, exp_ref[0], preferred_element_type=F32)
    eac_e = jnp.dot(ehm, exp_ref[1], preferred_element_type=F32)

    scores = lax.dot_general(cm_b, bm_b, (((1,), (1,)), ((), ())), preferred_element_type=F32)
    y_off = jnp.dot(cm_b, state[...].astype(BF16), preferred_element_type=F32) * eac_e

    causal = (lax.broadcasted_iota(jnp.int32, (T, T), 0) >= lax.broadcasted_iota(jnp.int32, (T, T), 1))
    left = lax.broadcasted_iota(jnp.int32, (T, LANE), 1) < M2_HEADDIM
    zero = jnp.zeros((T, LANE), BF16)
    ys = []
    for p in range(M2_HPG // 2):
        lhs = []
        for h in (2 * p, 2 * p + 1):
            seg = jnp.broadcast_to(acol[:, h:h + 1], (T, T)) - jnp.broadcast_to(arow[h:h + 1, :], (T, T))
            lhs.append((scores * jnp.exp(jnp.where(causal, seg, -1e30))).astype(BF16))
        xp = xb[:, p * LANE:(p + 1) * LANE]
        rhs = jnp.concatenate([jnp.where(left, xp, zero), jnp.where(left, zero, xp)], axis=0)
        ys.append(jnp.dot(jnp.concatenate(lhs, axis=1), rhs, preferred_element_type=F32))
    y = jnp.concatenate(ys, axis=1) + y_off + dskip_ref[...] * xs

    xw = (xs * dte_e).astype(BF16)
    state[...] = (state[...] * eac_e[T - 1:T, :]
                  + jnp.dot(bm_b.astype(F32).T.astype(BF16), xw, preferred_element_type=F32))

    y = y * z_ref[...].astype(F32)
    o_ref[...] = (_rms(y) * ng_ref[...]).astype(BF16)


def _ssd(xbc, zs, arow, acol, ehm, d_skip, norm_g):
    seq = xbc.shape[0]
    T, gw, n = M2_CHUNK, M2_GW, M2_STATE
    bb = M2_D_INNER // n
    cb = bb + M2_GROUPS
    col = jnp.arange(2 * LANE)[:, None] % LANE
    head = jnp.arange(gw)[None, :] // M2_HEADDIM
    expand = jnp.stack([col == head, col == head + M2_HPG]).astype(BF16)
    return pl.pallas_call(
        _ssd_body,
        out_shape=jax.ShapeDtypeStruct((seq, M2_D_INNER), BF16),
        grid=(M2_GROUPS, seq // T),
        in_specs=[pl.BlockSpec((T, gw), lambda g, c: (c, g)),
                  pl.BlockSpec((T, n), lambda g, c: (c, bb + g)),
                  pl.BlockSpec((T, n), lambda g, c: (c, cb + g)),
                  pl.BlockSpec((T, gw), lambda g, c: (c, Z_M2 // gw + g)),
                  pl.BlockSpec((M2_HPG, T), lambda g, c: (g, c)),
                  pl.BlockSpec((None, T, LANE), lambda g, c: (g, c, 0)),
                  pl.BlockSpec((None, T, 2 * LANE), lambda g, c: (g, c, 0)),
                  pl.BlockSpec((2, 2 * LANE, gw), lambda g, c: (0, 0, 0)),
                  pl.BlockSpec((1, gw), lambda g, c: (0, g)),
                  pl.BlockSpec((1, gw), lambda g, c: (0, g))],
        out_specs=pl.BlockSpec((T, gw), lambda g, c: (c, g)),
        scratch_shapes=[pltpu.VMEM((n, gw), F32)],
        compiler_params=_params("arbitrary", "arbitrary"),
        name="ssd",
    )(xbc, xbc, xbc, zs, arow, acol, ehm, expand,
      jnp.repeat(d_skip.astype(F32), M2_HEADDIM).reshape(1, M2_D_INNER),
      norm_g.astype(F32).reshape(1, M2_D_INNER))


def _merge_body(a1_ref, w1_ref, a2_ref, w2_ref, g1_ref, g2_ref, o_ref):
    m1 = jnp.dot(a1_ref[...], w1_ref[...], preferred_element_type=F32)
    m2 = jnp.dot(a2_ref[...], w2_ref[...], preferred_element_type=F32)
    o_ref[...] = (g1_ref[...].astype(F32) * m1 + g2_ref[...].astype(F32) * m2).astype(BF16)


def _merge(y1, w1, y2, w2, gates, tm, tn):
    seq = y1.shape[0]
    nb = D_MODEL // tn
    return pl.pallas_call(
        _merge_body,
        out_shape=jax.ShapeDtypeStruct((seq, D_MODEL), BF16),
        grid=(seq // tm, nb),
        in_specs=[pl.BlockSpec((tm, D_MODEL), lambda i, j: (i, 0)),
                  pl.BlockSpec((D_MODEL, tn), lambda i, j: (0, j)),
                  pl.BlockSpec((tm, M2_D_INNER), lambda i, j: (i, 0)),
                  pl.BlockSpec((M2_D_INNER, tn), lambda i, j: (0, j)),
                  pl.BlockSpec((tm, tn), lambda i, j: (i, j)),
                  pl.BlockSpec((tm, tn), lambda i, j: (i, nb + j))],
        out_specs=pl.BlockSpec((tm, tn), lambda i, j: (i, j)),
        compiler_params=_params("arbitrary", "arbitrary"),
        name="merge",
    )(y1, w1, y2, w2, gates, gates)


def _out_body(a_ref, w_ref, x_ref, gate_ref, o_ref, wbf):
    @pl.when(pl.program_id(1) == 0)
    def _():
        wbf[...] = w_ref[...].astype(BF16)

    m = jnp.dot(a_ref[...], wbf[...], preferred_element_type=F32)
    o_ref[...] = x_ref[...] + gate_ref[...] * m


def _out_proj(merged, w_out3, x2, mod, tm, tn):
    seq = merged.shape[0]
    nb = D_MODEL // tn
    return pl.pallas_call(
        _out_body,
        out_shape=jax.ShapeDtypeStruct((seq, D_MODEL), F32),
        grid=(nb, seq // tm),
        in_specs=[pl.BlockSpec((tm, D_MODEL), lambda j, i: (i, 0)),
                  pl.BlockSpec((None, D_MODEL, tn), lambda j, i: (0, 0, j)),
                  pl.BlockSpec((tm, tn), lambda j, i: (i, j)),
                  pl.BlockSpec((1, tn), lambda j, i: (0, 2 * nb + j))],
        out_specs=pl.BlockSpec((tm, tn), lambda j, i: (i, j)),
        scratch_shapes=[pltpu.VMEM((D_MODEL, tn), BF16)],
        compiler_params=_params("arbitrary", "arbitrary"),
        name="outproj",
    )(merged, w_out3, x2, mod)


def _fnorm_body(x_ref, g_ref, o_ref):
    o_ref[...] = _rms(x_ref[...]) * g_ref[...]


def _final_norm(x2, g, tm):
    seq = x2.shape[0]
    return pl.pallas_call(
        _fnorm_body,
        out_shape=jax.ShapeDtypeStruct((seq, D_MODEL), F32),
        grid=(seq // tm,),
        in_specs=[pl.BlockSpec((tm, D_MODEL), lambda i: (i, 0)),
                  pl.BlockSpec((1, D_MODEL), lambda i: (0, 0))],
        out_specs=pl.BlockSpec((tm, D_MODEL), lambda i: (i, 0)),
        compiler_params=_params("arbitrary"),
        name="fnorm",
    )(x2, g)


def _tiles(seq):
    return dict(tm=min(1024, seq), tm_merge=min(512, seq), tm_norm=min(256, seq), tn=512, tn_glu=256)


def kernel(x, c, w_ada, b_ada, norm_g, w_in, s5_lambda_re, s5_lambda_im, s5_log_step, s5_b_re, s5_b_im,
           s5_c_re, s5_c_im, s5_d, s5_w_glu, s5_b_glu, m2_conv_w, m2_conv_b, m2_dt_bias, m2_a_log, m2_d,
           m2_norm_g, w_br_s5, w_br_m2, w_out, final_g):
    bsz, seq, _ = x.shape
    assert bsz == 1 and w_ada.shape[0] == 1 and seq % (S5_Q * SUBLANE) == 0 and seq % M2_CHUNK == 0
    tl = _tiles(seq)
    tm, tn = tl["tm"], tl["tn"]
    x2 = x.reshape(seq, D_MODEL)

    w1, w2 = w_br_s5[0].astype(BF16), w_br_m2[0].astype(BF16)
    conv = (m2_conv_w[0].astype(F32), m2_conv_b[0].astype(F32).reshape(1, M2_CONV_DIM))

    mod = _ada_mod(c, w_ada[0], b_ada[0])
    h = _norm_mod(x2, norm_g[0].reshape(1, D_MODEL), mod, tl["tm_norm"])
    u = _proj(h, w_in, 0, D_MODEL, F32, tm, tn, "inproj_u")
    zs = _proj(h, w_in, WIN_S5Z, WIN_XBC - WIN_S5Z, BF16, tm, tn, "inproj_z", act=jax.nn.silu)
    xbc = _proj(h, w_in, WIN_XBC, M2_CONV_DIM, BF16, tm, tn, "inproj_x", conv=conv)
    gates = _proj(h, w_in, WIN_GATES, 2 * D_MODEL, BF16, tm, tn, "inproj_g", act=jax.nn.sigmoid, wsub=LANE)

    rows = _s5_rows(s5_lambda_re[0], s5_lambda_im[0], s5_log_step[0])
    wb, wct, w2t = _s5_prep(rows, s5_b_re[0], s5_b_im[0], s5_c_re[0], s5_c_im[0])
    ys5 = _s5(u, rows, wb, wct, w2t, s5_d[0])
    y1 = _glu(ys5, s5_w_glu, s5_b_glu[0].astype(F32).reshape(1, -1), zs, tm, tl["tn_glu"])

    arow, acol, ehm = _dt_prep(h, w_in, m2_dt_bias[0], m2_a_log[0])
    y2 = _ssd(xbc, zs, arow, acol, ehm, m2_d[0], m2_norm_g[0])

    merged = _merge(y1, w1, y2, w2, gates, tl["tm_merge"], tn)
    xn = _out_proj(merged, w_out, x2, mod, tm, tn)
    out = _final_norm(xn, final_g.astype(F32).reshape(1, D_MODEL), tl["tm_norm"])
    return out.reshape(bsz, seq, D_MODEL)
```

```python
import functools
import math

import jax
import jax.numpy as jnp
from jax import lax
from jax.experimental import pallas as pl
from jax.experimental.pallas import tpu as pltpu

F32 = jnp.float32
BF16 = jnp.bfloat16

LANE = 128
SUBLANE = 8
VMEM_LIMIT_BYTES = 60 * 1024 * 1024

D_MODEL = 4096
EPS = 1e-6

S5_GROUP = 16
S5_STATE = 64
S5_Q = 16
S5_TILE_GROUPS = LANE // S5_GROUP
S5_TILE_STATE = S5_TILE_GROUPS * S5_STATE
S5_TILES = D_MODEL // LANE

M2_D_INNER = 2 * D_MODEL
M2_HEADDIM = 64
M2_HEADS = M2_D_INNER // M2_HEADDIM
M2_GROUPS = 8
M2_HPG = M2_HEADS // M2_GROUPS
M2_STATE = 128
M2_CONV = 4
M2_CHUNK = 128
M2_GW = M2_HPG * M2_HEADDIM
M2_GN = M2_GROUPS * M2_STATE
M2_CONV_DIM = M2_D_INNER + 2 * M2_GN

WIN_S5Z = D_MODEL
WIN_XBC = 2 * D_MODEL + M2_D_INNER
WIN_DT = WIN_XBC + M2_CONV_DIM
WIN_GATES = WIN_DT + M2_HEADS
Z_S5 = 0
Z_M2 = D_MODEL


def _params(*sem):
    return pltpu.CompilerParams(dimension_semantics=sem, vmem_limit_bytes=VMEM_LIMIT_BYTES)


MOD_TN = 512


def _mod_body(cb_ref, w_ref, b_ref, o_ref):
    cb = cb_ref[...]
    for q in range(MOD_TN // LANE):
        sl = slice(q * LANE, (q + 1) * LANE)
        o_ref[:, sl] = jnp.sum(w_ref[:, sl] * cb, axis=0, keepdims=True) + b_ref[:, sl]


def _ada_mod(c, w_ada, b_ada):
    n = w_ada.shape[1]
    cb = jnp.broadcast_to(c.reshape(D_MODEL, 1), (D_MODEL, LANE))
    return pl.pallas_call(
        _mod_body,
        out_shape=jax.ShapeDtypeStruct((1, n), F32),
        grid=(n // MOD_TN,),
        in_specs=[pl.BlockSpec((D_MODEL, LANE), lambda j: (0, 0)),
                  pl.BlockSpec((D_MODEL, MOD_TN), lambda j: (0, j)),
                  pl.BlockSpec((1, MOD_TN), lambda j: (0, j))],
        out_specs=pl.BlockSpec((1, MOD_TN), lambda j: (0, j)),
        compiler_params=_params("arbitrary"),
        name="mod",
    )(cb, w_ada, b_ada.reshape(1, n))


def _rms(x):
    return x * lax.rsqrt(jnp.mean(x * x, axis=-1, keepdims=True) + EPS)


def _norm_body(x_ref, g_ref, shift_ref, scale_ref, o_ref):
    h = _rms(x_ref[...]) * g_ref[...] * (1.0 + scale_ref[...]) + shift_ref[...]
    o_ref[...] = h.astype(BF16)


def _norm_mod(x2, g, mod, tm):
    seq = x2.shape[0]
    return pl.pallas_call(
        _norm_body,
        out_shape=jax.ShapeDtypeStruct((seq, D_MODEL), BF16),
        grid=(seq // tm,),
        in_specs=[pl.BlockSpec((tm, D_MODEL), lambda i: (i, 0)),
                  pl.BlockSpec((1, D_MODEL), lambda i: (0, 0)),
                  pl.BlockSpec((1, D_MODEL), lambda i: (0, 0)),
                  pl.BlockSpec((1, D_MODEL), lambda i: (0, 1))],
        out_specs=pl.BlockSpec((tm, D_MODEL), lambda i: (i, 0)),
        compiler_params=_params("arbitrary"),
        name="norm",
    )(x2, g, mod, mod)


def _fetch_cast_weights(w_hbm, col_starts, wf32, wbf, sems, j, n_j):
    tn = wf32.shape[1] // len(col_starts)

    def copies(jj):
        return [pltpu.make_async_copy(w_hbm.at[0, :, pl.ds(pl.multiple_of(c0 + jj * tn, LANE), tn)],
                                      wf32.at[:, pl.ds(q * tn, tn)], sems.at[q])
                for q, c0 in enumerate(col_starts)]

    @pl.when(j == 0)
    def _():
        for cp in copies(0):
            cp.start()

    for cp in copies(j):
        cp.wait()
    wbf[...] = wf32[...].astype(BF16)

    @pl.when(j + 1 < n_j)
    def _():
        for cp in copies(j + 1):
            cp.start()


def _proj_body(*refs, col0, n_j, act, conv):
    if conv:
        a_ref, w_hbm, cw_ref, cb_ref, o_ref, wf32, wbf, sems, cbuf = refs
    else:
        a_ref, w_hbm, o_ref, wf32, wbf, sems = refs
    tm, tn = o_ref.shape

    @pl.when(pl.program_id(1) == 0)
    def _():
        _fetch_cast_weights(w_hbm, (col0,), wf32, wbf, sems, pl.program_id(0), n_j)
        if conv:
            cbuf[:SUBLANE, :] = jnp.zeros((SUBLANE, tn), F32)

    r = jnp.dot(a_ref[...], wbf[...], preferred_element_type=F32)
    if conv:
        k = cw_ref.shape[0]
        cbuf[SUBLANE:, :] = r
        y = cb_ref[...] + cw_ref[k - 1:k, :] * r
        for d in range(1, k):
            y = y + cw_ref[k - 1 - d:k - d, :] * cbuf[pl.ds(SUBLANE - d, tm), :]
        o_ref[...] = jax.nn.silu(y).astype(o_ref.dtype)
        cbuf[:SUBLANE, :] = r[tm - SUBLANE:]
    else:
        o_ref[...] = (act(r) if act else r).astype(o_ref.dtype)


def _proj(h, w3, col0, n, out_dtype, tm, tn, name, act=None, conv=None):
    seq, k = h.shape
    assert col0 % LANE == 0 and n % tn == 0
    extra, extra_specs = [], []
    scratch = [pltpu.VMEM((k, tn), F32), pltpu.VMEM((k, tn), BF16), pltpu.SemaphoreType.DMA((1,))]
    if conv:
        extra = list(conv)
        extra_specs = [pl.BlockSpec((conv[0].shape[0], tn), lambda j, i: (0, j)),
                       pl.BlockSpec((1, tn), lambda j, i: (0, j))]
        scratch.append(pltpu.VMEM((tm + SUBLANE, tn), F32))
    return pl.pallas_call(
        functools.partial(_proj_body, col0=col0, n_j=n // tn, act=act, conv=bool(conv)),
        out_shape=jax.ShapeDtypeStruct((seq, n), out_dtype),
        grid=(n // tn, seq // tm),
        in_specs=[pl.BlockSpec((tm, k), lambda j, i: (i, 0)), pl.BlockSpec(memory_space=pl.ANY)] + extra_specs,
        out_specs=pl.BlockSpec((tm, tn), lambda j, i: (i, j)),
        scratch_shapes=scratch,
        compiler_params=_params("arbitrary", "arbitrary"),
        name=name,
    )(h, w3, *extra)


def _lbar(lre, lim, log_step):
    step = jnp.exp(log_step)
    mag = jnp.exp(lre * step)
    return mag * jnp.cos(lim * step), mag * jnp.sin(lim * step)


def _s5prep_body(lre_ref, lim_ref, st_ref, btr_ref, bti_ref, ctr_ref, cti_ref,
                 wb_ref, wct_ref, w2_ref):
    ns = S5_TILE_STATE
    lre, lim = lre_ref[...], lim_ref[...]
    lbr, lbi = _lbar(lre, lim, st_ref[...])
    den = lre * lre + lim * lim
    nr, ni = lbr - 1.0, lbi
    cr = (nr * lre + ni * lim) / den
    ci = (ni * lre - nr * lim) / den
    row = jnp.right_shift(lax.broadcasted_iota(jnp.int32, (LANE, ns), 0), int(math.log2(S5_GROUP)))
    col = jnp.right_shift(lax.broadcasted_iota(jnp.int32, (LANE, ns), 1), int(math.log2(S5_STATE)))
    same = row == col
    btr = jnp.where(same, btr_ref[...], 0.0)
    bti = jnp.where(same, bti_ref[...], 0.0)
    ctr = jnp.where(same, ctr_ref[...], 0.0)
    cti = jnp.where(same, cti_ref[...], 0.0)
    bbr = cr * btr - ci * bti
    bbi = cr * bti + ci * btr
    pows = [(jnp.ones_like(lbr), jnp.zeros_like(lbr))]
    for _ in range(S5_Q):
        pr, pi = pows[-1]
        pows.append((pr * lbr - pi * lbi, pr * lbi + pi * lbr))
    for s in range(S5_Q):
        pr, pi = pows[S5_Q - 1 - s]
        rows = slice(s * LANE, (s + 1) * LANE)
        wb_ref[rows, :ns] = (pr * bbr - pi * bbi).astype(BF16)
        wb_ref[rows, ns:] = (pr * bbi + pi * bbr).astype(BF16)
    for t in range(S5_Q):
        pr, pi = pows[t + 1]
        rows = slice(t * LANE, (t + 1) * LANE)
        wct_ref[rows, :ns] = (ctr * pr - cti * pi).astype(BF16)
        wct_ref[rows, ns:] = (-(ctr * pi + cti * pr)).astype(BF16)
    ccat = jnp.concatenate([ctr, -cti], axis=1).astype(BF16)
    kall = lax.dot_general(wb_ref[...], ccat, (((1,), (1,)), ((), ())),
                           preferred_element_type=F32)
    w2_ref[LANE:, :LANE] = jnp.zeros((LANE, LANE), BF16)
    for tau in range(S5_Q):
        s = S5_Q - 1 - tau
        k_tau = kall[s * LANE:(s + 1) * LANE, :].astype(BF16)
        w2_ref[:LANE, tau * LANE:(tau + 1) * LANE] = k_tau
        if tau + 1 < S5_Q:
            w2_ref[LANE:, (tau + 1) * LANE:(tau + 2) * LANE] = k_tau


def _s5_rows(lam_re, lam_im, log_step):
    row = lambda a: a.astype(F32).reshape(S5_TILES, 1, S5_TILE_STATE)
    return row(lam_re), row(lam_im), row(jnp.broadcast_to(log_step[:, None], lam_re.shape))


def _s5_prep(rows, b_re, b_im, c_re, c_im):
    ns = S5_TILE_STATE

    def tiled(a):
        a = a.astype(F32).reshape(S5_TILES, LANE, S5_STATE)
        return jnp.tile(a, (1, 1, S5_TILE_GROUPS))

    row_spec = pl.BlockSpec((None, 1, ns), lambda j: (j, 0, 0))
    mat_spec = pl.BlockSpec((None, LANE, ns), lambda j: (j, 0, 0))
    return pl.pallas_call(
        _s5prep_body,
        out_shape=(jax.ShapeDtypeStruct((S5_TILES, S5_Q * LANE, 2 * ns), BF16),
                   jax.ShapeDtypeStruct((S5_TILES, S5_Q * LANE, 2 * ns), BF16),
                   jax.ShapeDtypeStruct((S5_TILES, 2 * LANE, S5_Q * LANE), BF16)),
        grid=(S5_TILES,),
        in_specs=[row_spec, row_spec, row_spec, mat_spec, mat_spec, mat_spec, mat_spec],
        out_specs=(pl.BlockSpec((None, S5_Q * LANE, 2 * ns), lambda j: (j, 0, 0)),
                   pl.BlockSpec((None, S5_Q * LANE, 2 * ns), lambda j: (j, 0, 0)),
                   pl.BlockSpec((None, 2 * LANE, S5_Q * LANE), lambda j: (j, 0, 0))),
        compiler_params=_params("arbitrary"),
        name="s5prep",
    )(*rows, tiled(b_re.swapaxes(1, 2)), tiled(b_im.swapaxes(1, 2)), tiled(c_re), tiled(c_im))


S5_SCAN_UNROLL = 8


def _s5_body(u_ref, lre_ref, lim_ref, st_ref, wb_ref, wct_ref, w2_ref, d_ref, o_ref, e_scr, acc, ynat):
    ns = S5_TILE_STATE
    nc = e_scr.shape[0]
    slabs = [u_ref[pl.ds(s, nc, stride=S5_Q), :] for s in range(S5_Q)]
    sb = [x.astype(BF16) for x in slabs]

    e_scr[...] = jnp.dot(jnp.concatenate(sb, axis=1), wb_ref[...], preferred_element_type=F32)
    ar, ai = _lbar(lre_ref[...], lim_ref[...], st_ref[...])
    for _ in range(int(math.log2(S5_Q))):
        ar, ai = ar * ar - ai * ai, 2.0 * ar * ai

    def step(c, carry):
        hr, hi = carry
        er, ei = e_scr[pl.ds(c, 1), :ns], e_scr[pl.ds(c, 1), ns:]
        e_scr[pl.ds(c, 1), :ns] = hr
        e_scr[pl.ds(c, 1), ns:] = hi
        return ar * hr - ai * hi + er, ar * hi + ai * hr + ei

    zero = jnp.zeros((1, ns), F32)
    lax.fori_loop(0, nc, step, (zero, zero), unroll=S5_SCAN_UNROLL)

    acc[...] = lax.dot_general(e_scr[...].astype(BF16), wct_ref[...], (((1,), (1,)), ((), ())),
                               preferred_element_type=F32)
    for s in range(0, S5_Q, 2):
        pair = jnp.concatenate([sb[s], sb[s + 1]], axis=1)
        width = (S5_Q - s) * LANE
        acc[:, s * LANE:] += jnp.dot(pair, w2_ref[:, :width], preferred_element_type=F32)
    d = d_ref[...]
    for t in range(S5_Q):
        y = acc[:, t * LANE:(t + 1) * LANE] + d * slabs[t]
        ynat[pl.ds(t, nc, stride=S5_Q), :] = jax.nn.gelu(y)
    o_ref[...] = ynat[...].astype(BF16)


def _s5(u, rows, wb, wct, w2, d_skip):
    seq = u.shape[0]
    nc, ns = seq // S5_Q, S5_TILE_STATE
    row_spec = pl.BlockSpec((None, 1, ns), lambda j: (j, 0, 0))
    return pl.pallas_call(
        _s5_body,
        out_shape=jax.ShapeDtypeStruct((seq, D_MODEL), BF16),
        grid=(S5_TILES,),
        in_specs=[pl.BlockSpec((seq, LANE), lambda j: (0, j)), row_spec, row_spec, row_spec,
                  pl.BlockSpec((None, S5_Q * LANE, 2 * ns), lambda j: (j, 0, 0)),
                  pl.BlockSpec((None, S5_Q * LANE, 2 * ns), lambda j: (j, 0, 0)),
                  pl.BlockSpec((None, 2 * LANE, S5_Q * LANE), lambda j: (j, 0, 0)),
                  pl.BlockSpec((1, LANE), lambda j: (0, j))],
        out_specs=pl.BlockSpec((seq, LANE), lambda j: (0, j)),
        scratch_shapes=[pltpu.VMEM((nc, 2 * ns), F32), pltpu.VMEM((nc, S5_Q * LANE), F32),
                        pltpu.VMEM((seq, LANE), F32)],
        compiler_params=_params("arbitrary"),
        name="s5",
    )(u, *rows, wb, wct, w2, d_skip.astype(F32).reshape(1, D_MODEL))


def _glu_body(a_ref, w_hbm, ba_ref, bb_ref, z_ref, o_ref, wf32, wbf, sems, *, n_j):
    tn = o_ref.shape[1]

    @pl.when(pl.program_id(1) == 0)
    def _():
        _fetch_cast_weights(w_hbm, (0, n_j * tn), wf32, wbf, sems, pl.program_id(0), n_j)

    g = jnp.dot(a_ref[...], wbf[...], preferred_element_type=F32)
    ga = g[:, :tn] + ba_ref[...]
    gb = g[:, tn:] + bb_ref[...]
    o_ref[...] = (ga * jax.nn.sigmoid(gb) * z_ref[...].astype(F32)).astype(BF16)


def _glu(ys5, w_glu3, b_glu, zs, tm, tn):
    seq = ys5.shape[0]
    nb = D_MODEL // tn
    return pl.pallas_call(
        functools.partial(_glu_body, n_j=nb),
        out_shape=jax.ShapeDtypeStruct((seq, D_MODEL), BF16),
        grid=(nb, seq // tm),
        in_specs=[pl.BlockSpec((tm, D_MODEL), lambda j, i: (i, 0)),
                  pl.BlockSpec(memory_space=pl.ANY),
                  pl.BlockSpec((1, tn), lambda j, i: (0, j)),
                  pl.BlockSpec((1, tn), lambda j, i: (0, nb + j)),
                  pl.BlockSpec((tm, tn), lambda j, i: (i, Z_S5 // tn + j))],
        out_specs=pl.BlockSpec((tm, tn), lambda j, i: (i, j)),
        scratch_shapes=[pltpu.VMEM((D_MODEL, 2 * tn), F32), pltpu.VMEM((D_MODEL, 2 * tn), BF16),
                        pltpu.SemaphoreType.DMA((2,))],
        compiler_params=_params("arbitrary", "arbitrary"),
        name="glu",
    )(ys5, w_glu3, b_glu, b_glu, zs)


def _split3(x):
    hi = x.astype(BF16)
    r1 = x - hi.astype(F32)
    mid = r1.astype(BF16)
    lo = (r1 - mid.astype(F32)).astype(BF16)
    return hi, mid, lo


def _split2(x):
    hi = x.astype(BF16)
    mid = (x - hi.astype(F32)).astype(BF16)
    return jnp.concatenate([hi, mid], axis=1)


def _dtprep_body(h_ref, w_ref, dtb_ref, alog_ref, tri_ref, arow_ref, acol_ref, ehm_ref, wbf):
    T = M2_CHUNK

    @pl.when(pl.program_id(0) == 0)
    def _():
        wbf[...] = w_ref[...].astype(BF16)

    raw = jnp.dot(h_ref[...].astype(F32), wbf[...].astype(F32), preferred_element_type=F32)
    dt = jax.nn.softplus(raw.T + dtb_ref[...])
    dta = dt * (-jnp.exp(alog_ref[...]))
    hi, mid, lo = _split3(dta)
    tri = tri_ref[...]
    acum = (jnp.dot(hi, tri, preferred_element_type=F32) + jnp.dot(mid, tri, preferred_element_type=F32)
            + jnp.dot(lo, tri, preferred_element_type=F32))
    rem = acum[:, T - 1:T] - acum
    arow_ref[...] = acum - jnp.log(dt)
    dte = dt * jnp.exp(rem)
    eac = jnp.exp(acum)
    pad1 = jnp.zeros((T - M2_HPG, T), F32)
    pad2 = jnp.zeros((T - 2 * M2_HPG, T), F32)
    for g in range(M2_GROUPS):
        hs = slice(g * M2_HPG, (g + 1) * M2_HPG)
        acol_ref[g] = jnp.concatenate([acum[hs], pad1], axis=0).T
        ehm_ref[g] = _split2(jnp.concatenate([dte[hs], eac[hs], pad2], axis=0).T)


def _dt_prep(h, w_in3, dt_bias, a_log):
    seq = h.shape[0]
    T = M2_CHUNK
    lanes = lambda v: jnp.broadcast_to(v.astype(F32)[:, None], (M2_HEADS, T))
    tri = (jnp.arange(T)[:, None] <= jnp.arange(T)[None, :]).astype(BF16)
    return pl.pallas_call(
        _dtprep_body,
        out_shape=(jax.ShapeDtypeStruct((M2_HEADS, seq), F32),
                   jax.ShapeDtypeStruct((M2_GROUPS, seq, LANE), F32),
                   jax.ShapeDtypeStruct((M2_GROUPS, seq, 2 * LANE), BF16)),
        grid=(seq // T,),
        in_specs=[pl.BlockSpec((T, D_MODEL), lambda c: (c, 0)),
                  pl.BlockSpec((None, D_MODEL, M2_HEADS), lambda c: (0, 0, WIN_DT // M2_HEADS)),
                  pl.BlockSpec((M2_HEADS, T), lambda c: (0, 0)),
                  pl.BlockSpec((M2_HEADS, T), lambda c: (0, 0)),
                  pl.BlockSpec((T, T), lambda c: (0, 0))],
        out_specs=(pl.BlockSpec((M2_HEADS, T), lambda c: (0, c)),
                   pl.BlockSpec((M2_GROUPS, T, LANE), lambda c: (0, c, 0)),
                   pl.BlockSpec((M2_GROUPS, T, 2 * LANE), lambda c: (0, c, 0))),
        scratch_shapes=[pltpu.VMEM((D_MODEL, M2_HEADS), BF16)],
        compiler_params=_params("arbitrary"),
        name="dtprep",
    )(h, w_in3, lanes(dt_bias), lanes(a_log), tri)


def _ssd_body(x_ref, b_ref, c_ref, z_ref, arow_ref, acol_ref, ehm_ref, exp_ref, dskip_ref, ng_ref, o_ref, state):
    T = M2_CHUNK

    @pl.when(pl.program_id(1) == 0)
    def _():
        state[...] = jnp.zeros_like(state)

    xb = x_ref[...]
    xs = xb.astype(F32)
    bm_b, cm_b = b_ref[...], c_ref[...]
    arow, acol = arow_ref[...], acol_ref[...]
    ehm = ehm_ref[...]
    dte_e = jnp.dot(ehm, exp_ref[0], preferred_element_type=F32)
    eac_e = jnp.dot(ehm, exp_ref[1], preferred_element_type=F32)

    scores = lax.dot_general(cm_b, bm_b, (((1,), (1,)), ((), ())), preferred_element_type=F32)
    y_off = jnp.dot(cm_b, state[...].astype(BF16), preferred_element_type=F32) * eac_e

    causal = (lax.broadcasted_iota(jnp.int32, (T, T), 0) >= lax.broadcasted_iota(jnp.int32, (T, T), 1))
    left = lax.broadcasted_iota(jnp.int32, (T, LANE), 1) < M2_HEADDIM
    zero = jnp.zeros((T, LANE), BF16)
    ys = []
    for p in range(M2_HPG // 2):
        lhs = []
        for h in (2 * p, 2 * p + 1):
            seg = jnp.broadcast_to(acol[:, h:h + 1], (T, T)) - jnp.broadcast_to(arow[h:h + 1, :], (T, T))
            lhs.append((scores * jnp.exp(jnp.where(causal, seg, -1e30))).astype(BF16))
        xp = xb[:, p * LANE:(p + 1) * LANE]
        rhs = jnp.concatenate([jnp.where(left, xp, zero), jnp.where(left, zero, xp)], axis=0)
        ys.append(jnp.dot(jnp.concatenate(lhs, axis=1), rhs, preferred_element_type=F32))
    y = jnp.concatenate(ys, axis=1) + y_off + dskip_ref[...] * xs

    xw = (xs * dte_e).astype(BF16)
    state[...] = (state[...] * eac_e[T - 1:T, :]
                  + jnp.dot(bm_b.astype(F32).T.astype(BF16), xw, preferred_element_type=F32))

    y = y * z_ref[...].astype(F32)
    o_ref[...] = (_rms(y) * ng_ref[...]).astype(BF16)


def _ssd(xbc, zs, arow, acol, ehm, d_skip, norm_g):
    seq = xbc.shape[0]
    T, gw, n = M2_CHUNK, M2_GW, M2_STATE
    bb = M2_D_INNER // n
    cb = bb + M2_GROUPS
    col = jnp.arange(2 * LANE)[:, None] % LANE
    head = jnp.arange(gw)[None, :] // M2_HEADDIM
    expand = jnp.stack([col == head, col == head + M2_HPG]).astype(BF16)
    return pl.pallas_call(
        _ssd_body,
        out_shape=jax.ShapeDtypeStruct((seq, M2_D_INNER), BF16),
        grid=(M2_GROUPS, seq // T),
        in_specs=[pl.BlockSpec((T, gw), lambda g, c: (c, g)),
                  pl.BlockSpec((T, n), lambda g, c: (c, bb + g)),
                  pl.BlockSpec((T, n), lambda g, c: (c, cb + g)),
                  pl.BlockSpec((T, gw), lambda g, c: (c, Z_M2 // gw + g)),
                  pl.BlockSpec((M2_HPG, T), lambda g, c: (g, c)),
                  pl.BlockSpec((None, T, LANE), lambda g, c: (g, c, 0)),
                  pl.BlockSpec((None, T, 2 * LANE), lambda g, c: (g, c, 0)),
                  pl.BlockSpec((2, 2 * LANE, gw), lambda g, c: (0, 0, 0)),
                  pl.BlockSpec((1, gw), lambda g, c: (0, g)),
                  pl.BlockSpec((1, gw), lambda g, c: (0, g))],
        out_specs=pl.BlockSpec((T, gw), lambda g, c: (c, g)),
        scratch_shapes=[pltpu.VMEM((n, gw), F32)],
        compiler_params=_params("arbitrary", "arbitrary"),
        name="ssd",
    )(xbc, xbc, xbc, zs, arow, acol, ehm, expand,
      jnp.repeat(d_skip.astype(F32), M2_HEADDIM).reshape(1, M2_D_INNER),
      norm_g.astype(F32).reshape(1, M2_D_INNER))


def _merge_body(a1_ref, w1_ref, a2_ref, w2_ref, g1_ref, g2_ref, o_ref):
    m1 = jnp.dot(a1_ref[...], w1_ref[...], preferred_element_type=F32)
    m2 = jnp.dot(a2_ref[...], w2_ref[...], preferred_element_type=F32)
    o_ref[...] = (g1_ref[...].astype(F32) * m1 + g2_ref[...].astype(F32) * m2).astype(BF16)


def _merge(y1, w1, y2, w2, gates, tm, tn):
    seq = y1.shape[0]
    nb = D_MODEL // tn
    return pl.pallas_call(
        _merge_body,
        out_shape=jax.ShapeDtypeStruct((seq, D_MODEL), BF16),
        grid=(seq // tm, nb),
        in_specs=[pl.BlockSpec((tm, D_MODEL), lambda i, j: (i, 0)),
                  pl.BlockSpec((D_MODEL, tn), lambda i, j: (0, j)),
                  pl.BlockSpec((tm, M2_D_INNER), lambda i, j: (i, 0)),
                  pl.BlockSpec((M2_D_INNER, tn), lambda i, j: (0, j)),
                  pl.BlockSpec((tm, tn), lambda i, j: (i, j)),
                  pl.BlockSpec((tm, tn), lambda i, j: (i, nb + j))],
        out_specs=pl.BlockSpec((tm, tn), lambda i, j: (i, j)),
        compiler_params=_params("arbitrary", "arbitrary"),
        name="merge",
    )(y1, w1, y2, w2, gates, gates)


def _out_body(a_ref, w_hbm, x_ref, gate_ref, o_ref, wf32, wbf, sems, *, n_j):
    @pl.when(pl.program_id(1) == 0)
    def _():
        _fetch_cast_weights(w_hbm, (0,), wf32, wbf, sems, pl.program_id(0), n_j)

    m = jnp.dot(a_ref[...], wbf[...], preferred_element_type=F32)
    o_ref[...] = x_ref[...] + gate_ref[...] * m


def _out_proj(merged, w_out3, x2, mod, tm, tn):
    seq = merged.shape[0]
    nb = D_MODEL // tn
    return pl.pallas_call(
        functools.partial(_out_body, n_j=nb),
        out_shape=jax.ShapeDtypeStruct((seq, D_MODEL), F32),
        grid=(nb, seq // tm),
        in_specs=[pl.BlockSpec((tm, D_MODEL), lambda j, i: (i, 0)),
                  pl.BlockSpec(memory_space=pl.ANY),
                  pl.BlockSpec((tm, tn), lambda j, i: (i, j)),
                  pl.BlockSpec((1, tn), lambda j, i: (0, 2 * nb + j))],
        out_specs=pl.BlockSpec((tm, tn), lambda j, i: (i, j)),
        scratch_shapes=[pltpu.VMEM((D_MODEL, tn), F32), pltpu.VMEM((D_MODEL, tn), BF16),
                        pltpu.SemaphoreType.DMA((1,))],
        compiler_params=_params("arbitrary", "arbitrary"),
        name="outproj",
    )(merged, w_out3, x2, mod)


def _fnorm_body(x_ref, g_ref, o_ref):
    o_ref[...] = _rms(x_ref[...]) * g_ref[...]


def _final_norm(x2, g, tm):
    seq = x2.shape[0]
    return pl.pallas_call(
        _fnorm_body,
        out_shape=jax.ShapeDtypeStruct((seq, D_MODEL), F32),
        grid=(seq // tm,),
        in_specs=[pl.BlockSpec((tm, D_MODEL), lambda i: (i, 0)),
                  pl.BlockSpec((1, D_MODEL), lambda i: (0, 0))],
        out_specs=pl.BlockSpec((tm, D_MODEL), lambda i: (i, 0)),
        compiler_params=_params("arbitrary"),
        name="fnorm",
    )(x2, g)


def _tiles(seq):
    return dict(tm=min(1024, seq), tm_merge=min(512, seq), tm_norm=min(256, seq), tn=512, tn_in=1024, tn_glu=256)


def kernel(x, c, w_ada, b_ada, norm_g, w_in, s5_lambda_re, s5_lambda_im, s5_log_step, s5_b_re, s5_b_im,
           s5_c_re, s5_c_im, s5_d, s5_w_glu, s5_b_glu, m2_conv_w, m2_conv_b, m2_dt_bias, m2_a_log, m2_d,
           m2_norm_g, w_br_s5, w_br_m2, w_out, final_g):
    bsz, seq, _ = x.shape
    assert bsz == 1 and w_ada.shape[0] == 1 and seq % (S5_Q * SUBLANE) == 0 and seq % M2_CHUNK == 0
    tl = _tiles(seq)
    tm, tn = tl["tm"], tl["tn"]
    x2 = x.reshape(seq, D_MODEL)

    w1, w2 = w_br_s5[0].astype(BF16), w_br_m2[0].astype(BF16)
    conv = (m2_conv_w[0].astype(F32), m2_conv_b[0].astype(F32).reshape(1, M2_CONV_DIM))

    mod = _ada_mod(c, w_ada[0], b_ada[0])
    h = _norm_mod(x2, norm_g[0].reshape(1, D_MODEL), mod, tl["tm_norm"])
    tn_in = tl["tn_in"]
    u = _proj(h, w_in, 0, D_MODEL, F32, tm, tn_in, "inproj_u")
    zs = _proj(h, w_in, WIN_S5Z, WIN_XBC - WIN_S5Z, BF16, tm, tn_in, "inproj_z", act=jax.nn.silu)
    xbc = _proj(h, w_in, WIN_XBC, M2_CONV_DIM, BF16, tm, tn_in, "inproj_x", conv=conv)
    gates = _proj(h, w_in, WIN_GATES, 2 * D_MODEL, BF16, tm, tn_in, "inproj_g", act=jax.nn.sigmoid)

    rows = _s5_rows(s5_lambda_re[0], s5_lambda_im[0], s5_log_step[0])
    wb, wct, w2t = _s5_prep(rows, s5_b_re[0], s5_b_im[0], s5_c_re[0], s5_c_im[0])
    ys5 = _s5(u, rows, wb, wct, w2t, s5_d[0])
    y1 = _glu(ys5, s5_w_glu, s5_b_glu[0].astype(F32).reshape(1, -1), zs, tm, tl["tn_glu"])

    arow, acol, ehm = _dt_prep(h, w_in, m2_dt_bias[0], m2_a_log[0])
    y2 = _ssd(xbc, zs, arow, acol, ehm, m2_d[0], m2_norm_g[0])

    merged = _merge(y1, w1, y2, w2, gates, tl["tm_merge"], tn)
    xn = _out_proj(merged, w_out, x2, mod, tm, tn)
    out = _final_norm(xn, final_g.astype(F32).reshape(1, D_MODEL), tl["tm_norm"])
    return out.reshape(bsz, seq, D_MODEL)
```

```python
import functools
import math

import jax
import jax.numpy as jnp
from jax import lax
from jax.experimental import pallas as pl
from jax.experimental.pallas import tpu as pltpu

F32 = jnp.float32
BF16 = jnp.bfloat16

LANE = 128
SUBLANE = 8
VMEM_LIMIT_BYTES = 60 * 1024 * 1024

D_MODEL = 4096
EPS = 1e-6

S5_GROUP = 16
S5_STATE = 64
S5_Q = 16
S5_TILE_GROUPS = LANE // S5_GROUP
S5_TILE_STATE = S5_TILE_GROUPS * S5_STATE
S5_TILES = D_MODEL // LANE

M2_D_INNER = 2 * D_MODEL
M2_HEADDIM = 64
M2_HEADS = M2_D_INNER // M2_HEADDIM
M2_GROUPS = 8
M2_HPG = M2_HEADS // M2_GROUPS
M2_STATE = 128
M2_CONV = 4
M2_CHUNK = 128
M2_CHUNKS_PER_STEP = 2
M2_GW = M2_HPG * M2_HEADDIM
M2_GN = M2_GROUPS * M2_STATE
M2_CONV_DIM = M2_D_INNER + 2 * M2_GN

WIN_S5Z = D_MODEL
WIN_XBC = 2 * D_MODEL + M2_D_INNER
WIN_DT = WIN_XBC + M2_CONV_DIM
WIN_GATES = WIN_DT + M2_HEADS
Z_S5 = 0
Z_M2 = D_MODEL


def _params(*sem):
    return pltpu.CompilerParams(dimension_semantics=sem, vmem_limit_bytes=VMEM_LIMIT_BYTES)


MOD_TN = 512


def _mod_body(cb_ref, w_ref, b_ref, o_ref):
    cb = cb_ref[...]
    for q in range(MOD_TN // LANE):
        sl = slice(q * LANE, (q + 1) * LANE)
        o_ref[:, sl] = jnp.sum(w_ref[:, sl] * cb, axis=0, keepdims=True) + b_ref[:, sl]


def _ada_mod(c, w_ada, b_ada):
    n = w_ada.shape[1]
    cb = jnp.broadcast_to(c.reshape(D_MODEL, 1), (D_MODEL, LANE))
    return pl.pallas_call(
        _mod_body,
        out_shape=jax.ShapeDtypeStruct((1, n), F32),
        grid=(n // MOD_TN,),
        in_specs=[pl.BlockSpec((D_MODEL, LANE), lambda j: (0, 0)),
                  pl.BlockSpec((D_MODEL, MOD_TN), lambda j: (0, j)),
                  pl.BlockSpec((1, MOD_TN), lambda j: (0, j))],
        out_specs=pl.BlockSpec((1, MOD_TN), lambda j: (0, j)),
        compiler_params=_params("arbitrary"),
        name="mod",
    )(cb, w_ada, b_ada.reshape(1, n))


def _rms(x):
    return x * lax.rsqrt(jnp.mean(x * x, axis=-1, keepdims=True) + EPS)


def _norm_body(x_ref, g_ref, shift_ref, scale_ref, o_ref):
    h = _rms(x_ref[...]) * g_ref[...] * (1.0 + scale_ref[...]) + shift_ref[...]
    o_ref[...] = h.astype(BF16)


def _norm_mod(x2, g, mod, tm):
    seq = x2.shape[0]
    return pl.pallas_call(
        _norm_body,
        out_shape=jax.ShapeDtypeStruct((seq, D_MODEL), BF16),
        grid=(seq // tm,),
        in_specs=[pl.BlockSpec((tm, D_MODEL), lambda i: (i, 0)),
                  pl.BlockSpec((1, D_MODEL), lambda i: (0, 0)),
                  pl.BlockSpec((1, D_MODEL), lambda i: (0, 0)),
                  pl.BlockSpec((1, D_MODEL), lambda i: (0, 1))],
        out_specs=pl.BlockSpec((tm, D_MODEL), lambda i: (i, 0)),
        compiler_params=_params("arbitrary"),
        name="norm",
    )(x2, g, mod, mod)


CAST_ROWS = 512


def _fetch_cast_weights(w_hbm, col_starts, wf32, wbf, sems, j, n_j):
    tn = wf32.shape[1] // len(col_starts)

    def copies(jj):
        return [pltpu.make_async_copy(w_hbm.at[0, :, pl.ds(pl.multiple_of(c0 + jj * tn, LANE), tn)],
                                      wf32.at[:, pl.ds(q * tn, tn)], sems.at[q])
                for q, c0 in enumerate(col_starts)]

    @pl.when(j == 0)
    def _():
        for cp in copies(0):
            cp.start()

    for cp in copies(j):
        cp.wait()
    for r0 in range(0, wf32.shape[0], CAST_ROWS):
        wbf[r0:r0 + CAST_ROWS, :] = wf32[r0:r0 + CAST_ROWS, :].astype(BF16)

    @pl.when(j + 1 < n_j)
    def _():
        for cp in copies(j + 1):
            cp.start()


def _proj_body(*refs, col0, n_j, act, conv):
    if conv:
        a_ref, w_hbm, cw_ref, cb_ref, o_ref, wf32, wbf, sems, cbuf = refs
    else:
        a_ref, w_hbm, o_ref, wf32, wbf, sems = refs
    tm, tn = o_ref.shape

    @pl.when(pl.program_id(1) == 0)
    def _():
        _fetch_cast_weights(w_hbm, (col0,), wf32, wbf, sems, pl.program_id(0), n_j)
        if conv:
            cbuf[:SUBLANE, :] = jnp.zeros((SUBLANE, tn), F32)

    r = jnp.dot(a_ref[...], wbf[...], preferred_element_type=F32)
    if conv:
        k = cw_ref.shape[0]
        cbuf[SUBLANE:, :] = r
        y = cb_ref[...] + cw_ref[k - 1:k, :] * r
        for d in range(1, k):
            y = y + cw_ref[k - 1 - d:k - d, :] * cbuf[pl.ds(SUBLANE - d, tm), :]
        o_ref[...] = jax.nn.silu(y).astype(o_ref.dtype)
        cbuf[:SUBLANE, :] = r[tm - SUBLANE:]
    else:
        o_ref[...] = (act(r) if act else r).astype(o_ref.dtype)


def _proj(h, w3, col0, n, out_dtype, tm, tn, name, act=None, conv=None):
    seq, k = h.shape
    assert col0 % LANE == 0 and n % tn == 0
    extra, extra_specs = [], []
    scratch = [pltpu.VMEM((k, tn), F32), pltpu.VMEM((k, tn), BF16), pltpu.SemaphoreType.DMA((1,))]
    if conv:
        extra = list(conv)
        extra_specs = [pl.BlockSpec((conv[0].shape[0], tn), lambda j, i: (0, j)),
                       pl.BlockSpec((1, tn), lambda j, i: (0, j))]
        scratch.append(pltpu.VMEM((tm + SUBLANE, tn), F32))
    return pl.pallas_call(
        functools.partial(_proj_body, col0=col0, n_j=n // tn, act=act, conv=bool(conv)),
        out_shape=jax.ShapeDtypeStruct((seq, n), out_dtype),
        grid=(n // tn, seq // tm),
        in_specs=[pl.BlockSpec((tm, k), lambda j, i: (i, 0)), pl.BlockSpec(memory_space=pl.ANY)] + extra_specs,
        out_specs=pl.BlockSpec((tm, tn), lambda j, i: (i, j)),
        scratch_shapes=scratch,
        compiler_params=_params("arbitrary", "arbitrary"),
        name=name,
    )(h, w3, *extra)


def _lbar(lre, lim, log_step):
    step = jnp.exp(log_step)
    mag = jnp.exp(lre * step)
    return mag * jnp.cos(lim * step), mag * jnp.sin(lim * step)


def _s5prep_body(lre_ref, lim_ref, st_ref, btr_ref, bti_ref, ctr_ref, cti_ref,
                 wb_ref, wct_ref, w2_ref):
    ns = S5_TILE_STATE
    lre, lim = lre_ref[...], lim_ref[...]
    lbr, lbi = _lbar(lre, lim, st_ref[...])
    den = lre * lre + lim * lim
    nr, ni = lbr - 1.0, lbi
    cr = (nr * lre + ni * lim) / den
    ci = (ni * lre - nr * lim) / den
    row = jnp.right_shift(lax.broadcasted_iota(jnp.int32, (LANE, ns), 0), int(math.log2(S5_GROUP)))
    col = jnp.right_shift(lax.broadcasted_iota(jnp.int32, (LANE, ns), 1), int(math.log2(S5_STATE)))
    same = row == col
    btr = jnp.where(same, btr_ref[...], 0.0)
    bti = jnp.where(same, bti_ref[...], 0.0)
    ctr = jnp.where(same, ctr_ref[...], 0.0)
    cti = jnp.where(same, cti_ref[...], 0.0)
    bbr = cr * btr - ci * bti
    bbi = cr * bti + ci * btr
    pows = [(jnp.ones_like(lbr), jnp.zeros_like(lbr))]
    for _ in range(S5_Q):
        pr, pi = pows[-1]
        pows.append((pr * lbr - pi * lbi, pr * lbi + pi * lbr))
    for s in range(S5_Q):
        pr, pi = pows[S5_Q - 1 - s]
        rows = slice(s * LANE, (s + 1) * LANE)
        wb_ref[rows, :ns] = (pr * bbr - pi * bbi).astype(BF16)
        wb_ref[rows, ns:] = (pr * bbi + pi * bbr).astype(BF16)
    for t in range(S5_Q):
        pr, pi = pows[t + 1]
        rows = slice(t * LANE, (t + 1) * LANE)
        wct_ref[rows, :ns] = (ctr * pr - cti * pi).astype(BF16)
        wct_ref[rows, ns:] = (-(ctr * pi + cti * pr)).astype(BF16)
    ccat = jnp.concatenate([ctr, -cti], axis=1).astype(BF16)
    kall = lax.dot_general(wb_ref[...], ccat, (((1,), (1,)), ((), ())),
                           preferred_element_type=F32)
    w2_ref[LANE:, :LANE] = jnp.zeros((LANE, LANE), BF16)
    for tau in range(S5_Q):
        s = S5_Q - 1 - tau
        k_tau = kall[s * LANE:(s + 1) * LANE, :].astype(BF16)
        w2_ref[:LANE, tau * LANE:(tau + 1) * LANE] = k_tau
        if tau + 1 < S5_Q:
            w2_ref[LANE:, (tau + 1) * LANE:(tau + 2) * LANE] = k_tau


def _s5_rows(lam_re, lam_im, log_step):
    row = lambda a: a.astype(F32).reshape(S5_TILES, 1, S5_TILE_STATE)
    return row(lam_re), row(lam_im), row(jnp.broadcast_to(log_step[:, None], lam_re.shape))


def _s5_prep(rows, b_re, b_im, c_re, c_im):
    ns = S5_TILE_STATE

    def tiled(a):
        a = a.astype(F32).reshape(S5_TILES, LANE, S5_STATE)
        return jnp.tile(a, (1, 1, S5_TILE_GROUPS))

    row_spec = pl.BlockSpec((None, 1, ns), lambda j: (j, 0, 0))
    mat_spec = pl.BlockSpec((None, LANE, ns), lambda j: (j, 0, 0))
    return pl.pallas_call(
        _s5prep_body,
        out_shape=(jax.ShapeDtypeStruct((S5_TILES, S5_Q * LANE, 2 * ns), BF16),
                   jax.ShapeDtypeStruct((S5_TILES, S5_Q * LANE, 2 * ns), BF16),
                   jax.ShapeDtypeStruct((S5_TILES, 2 * LANE, S5_Q * LANE), BF16)),
        grid=(S5_TILES,),
        in_specs=[row_spec, row_spec, row_spec, mat_spec, mat_spec, mat_spec, mat_spec],
        out_specs=(pl.BlockSpec((None, S5_Q * LANE, 2 * ns), lambda j: (j, 0, 0)),
                   pl.BlockSpec((None, S5_Q * LANE, 2 * ns), lambda j: (j, 0, 0)),
                   pl.BlockSpec((None, 2 * LANE, S5_Q * LANE), lambda j: (j, 0, 0))),
        compiler_params=_params("arbitrary"),
        name="s5prep",
    )(*rows, tiled(b_re.swapaxes(1, 2)), tiled(b_im.swapaxes(1, 2)), tiled(c_re), tiled(c_im))


S5_SCAN_UNROLL = 8


def _s5_body(u_ref, lre_ref, lim_ref, st_ref, wb_ref, wct_ref, w2_ref, d_ref, o_ref, e_scr, acc, ynat):
    ns = S5_TILE_STATE
    nc = e_scr.shape[0]
    slabs = [u_ref[pl.ds(s, nc, stride=S5_Q), :] for s in range(S5_Q)]
    sb = [x.astype(BF16) for x in slabs]

    e_scr[...] = jnp.dot(jnp.concatenate(sb, axis=1), wb_ref[...], preferred_element_type=F32)
    ar, ai = _lbar(lre_ref[...], lim_ref[...], st_ref[...])
    for _ in range(int(math.log2(S5_Q))):
        ar, ai = ar * ar - ai * ai, 2.0 * ar * ai

    def step(c, carry):
        hr, hi = carry
        er, ei = e_scr[pl.ds(c, 1), :ns], e_scr[pl.ds(c, 1), ns:]
        e_scr[pl.ds(c, 1), :ns] = hr
        e_scr[pl.ds(c, 1), ns:] = hi
        return ar * hr - ai * hi + er, ar * hi + ai * hr + ei

    zero = jnp.zeros((1, ns), F32)
    lax.fori_loop(0, nc, step, (zero, zero), unroll=S5_SCAN_UNROLL)

    acc[...] = lax.dot_general(e_scr[...].astype(BF16), wct_ref[...], (((1,), (1,)), ((), ())),
                               preferred_element_type=F32)
    for s in range(0, S5_Q, 2):
        pair = jnp.concatenate([sb[s], sb[s + 1]], axis=1)
        width = (S5_Q - s) * LANE
        acc[:, s * LANE:] += jnp.dot(pair, w2_ref[:, :width], preferred_element_type=F32)
    d = d_ref[...]
    for t in range(S5_Q):
        y = acc[:, t * LANE:(t + 1) * LANE] + d * slabs[t]
        ynat[pl.ds(t, nc, stride=S5_Q), :] = jax.nn.gelu(y)
    o_ref[...] = ynat[...].astype(BF16)


def _s5(u, rows, wb, wct, w2, d_skip):
    seq = u.shape[0]
    nc, ns = seq // S5_Q, S5_TILE_STATE
    row_spec = pl.BlockSpec((None, 1, ns), lambda j: (j, 0, 0))
    return pl.pallas_call(
        _s5_body,
        out_shape=jax.ShapeDtypeStruct((seq, D_MODEL), BF16),
        grid=(S5_TILES,),
        in_specs=[pl.BlockSpec((seq, LANE), lambda j: (0, j)), row_spec, row_spec, row_spec,
                  pl.BlockSpec((None, S5_Q * LANE, 2 * ns), lambda j: (j, 0, 0)),
                  pl.BlockSpec((None, S5_Q * LANE, 2 * ns), lambda j: (j, 0, 0)),
                  pl.BlockSpec((None, 2 * LANE, S5_Q * LANE), lambda j: (j, 0, 0)),
                  pl.BlockSpec((1, LANE), lambda j: (0, j))],
        out_specs=pl.BlockSpec((seq, LANE), lambda j: (0, j)),
        scratch_shapes=[pltpu.VMEM((nc, 2 * ns), F32), pltpu.VMEM((nc, S5_Q * LANE), F32),
                        pltpu.VMEM((seq, LANE), F32)],
        compiler_params=_params("arbitrary"),
        name="s5",
    )(u, *rows, wb, wct, w2, d_skip.astype(F32).reshape(1, D_MODEL))


def _glu_body(a_ref, w_hbm, ba_ref, bb_ref, z_ref, o_ref, wf32, wbf, sems, *, n_j):
    tn = o_ref.shape[1]

    @pl.when(pl.program_id(1) == 0)
    def _():
        _fetch_cast_weights(w_hbm, (0, n_j * tn), wf32, wbf, sems, pl.program_id(0), n_j)

    g = jnp.dot(a_ref[...], wbf[...], preferred_element_type=F32)
    ga = g[:, :tn] + ba_ref[...]
    gb = g[:, tn:] + bb_ref[...]
    o_ref[...] = (ga * jax.nn.sigmoid(gb) * z_ref[...].astype(F32)).astype(BF16)


def _glu(ys5, w_glu3, b_glu, zs, tm, tn):
    seq = ys5.shape[0]
    nb = D_MODEL // tn
    return pl.pallas_call(
        functools.partial(_glu_body, n_j=nb),
        out_shape=jax.ShapeDtypeStruct((seq, D_MODEL), BF16),
        grid=(nb, seq // tm),
        in_specs=[pl.BlockSpec((tm, D_MODEL), lambda j, i: (i, 0)),
                  pl.BlockSpec(memory_space=pl.ANY),
                  pl.BlockSpec((1, tn), lambda j, i: (0, j)),
                  pl.BlockSpec((1, tn), lambda j, i: (0, nb + j)),
                  pl.BlockSpec((tm, tn), lambda j, i: (i, Z_S5 // tn + j))],
        out_specs=pl.BlockSpec((tm, tn), lambda j, i: (i, j)),
        scratch_shapes=[pltpu.VMEM((D_MODEL, 2 * tn), F32), pltpu.VMEM((D_MODEL, 2 * tn), BF16),
                        pltpu.SemaphoreType.DMA((2,))],
        compiler_params=_params("arbitrary", "arbitrary"),
        name="glu",
    )(ys5, w_glu3, b_glu, b_glu, zs)


def _split3(x):
    hi = x.astype(BF16)
    r1 = x - hi.astype(F32)
    mid = r1.astype(BF16)
    lo = (r1 - mid.astype(F32)).astype(BF16)
    return hi, mid, lo


def _split2(x):
    hi = x.astype(BF16)
    mid = (x - hi.astype(F32)).astype(BF16)
    return jnp.concatenate([hi, mid], axis=1)


def _dtprep_body(h_ref, w_ref, dtb_ref, alog_ref, tri_ref, arow_ref, acol_ref, ehm_ref, wbf):
    T = M2_CHUNK

    @pl.when(pl.program_id(0) == 0)
    def _():
        wbf[...] = w_ref[...].astype(BF16)

    raw = jnp.dot(h_ref[...].astype(F32), wbf[...].astype(F32), preferred_element_type=F32)
    dt = jax.nn.softplus(raw.T + dtb_ref[...])
    dta = dt * (-jnp.exp(alog_ref[...]))
    hi, mid, lo = _split3(dta)
    tri = tri_ref[...]
    acum = (jnp.dot(hi, tri, preferred_element_type=F32) + jnp.dot(mid, tri, preferred_element_type=F32)
            + jnp.dot(lo, tri, preferred_element_type=F32))
    rem = acum[:, T - 1:T] - acum
    arow_ref[...] = acum - jnp.log(dt)
    dte = dt * jnp.exp(rem)
    eac = jnp.exp(acum)
    pad1 = jnp.zeros((T - M2_HPG, T), F32)
    pad2 = jnp.zeros((T - 2 * M2_HPG, T), F32)
    for g in range(M2_GROUPS):
        hs = slice(g * M2_HPG, (g + 1) * M2_HPG)
        acol_ref[g] = jnp.concatenate([acum[hs], pad1], axis=0).T
        ehm_ref[g] = _split2(jnp.concatenate([dte[hs], eac[hs], pad2], axis=0).T)


def _dt_prep(h, w_in3, dt_bias, a_log):
    seq = h.shape[0]
    T = M2_CHUNK
    lanes = lambda v: jnp.broadcast_to(v.astype(F32)[:, None], (M2_HEADS, T))
    tri = (jnp.arange(T)[:, None] <= jnp.arange(T)[None, :]).astype(BF16)
    return pl.pallas_call(
        _dtprep_body,
        out_shape=(jax.ShapeDtypeStruct((M2_HEADS, seq), F32),
                   jax.ShapeDtypeStruct((M2_GROUPS, seq, LANE), F32),
                   jax.ShapeDtypeStruct((M2_GROUPS, seq, 2 * LANE), BF16)),
        grid=(seq // T,),
        in_specs=[pl.BlockSpec((T, D_MODEL), lambda c: (c, 0)),
                  pl.BlockSpec((None, D_MODEL, M2_HEADS), lambda c: (0, 0, WIN_DT // M2_HEADS)),
                  pl.BlockSpec((M2_HEADS, T), lambda c: (0, 0)),
                  pl.BlockSpec((M2_HEADS, T), lambda c: (0, 0)),
                  pl.BlockSpec((T, T), lambda c: (0, 0))],
        out_specs=(pl.BlockSpec((M2_HEADS, T), lambda c: (0, c)),
                   pl.BlockSpec((M2_GROUPS, T, LANE), lambda c: (0, c, 0)),
                   pl.BlockSpec((M2_GROUPS, T, 2 * LANE), lambda c: (0, c, 0))),
        scratch_shapes=[pltpu.VMEM((D_MODEL, M2_HEADS), BF16)],
        compiler_params=_params("arbitrary"),
        name="dtprep",
    )(h, w_in3, lanes(dt_bias), lanes(a_log), tri)


def _ssd_body(x_ref, b_ref, c_ref, z_ref, arow_ref, acol_ref, ehm_ref, exp_ref, dskip_ref, ng_ref, o_ref, state):
    @pl.when(pl.program_id(1) == 0)
    def _():
        state[...] = jnp.zeros_like(state)

    for ci in range(M2_CHUNKS_PER_STEP):
        rows = slice(ci * M2_CHUNK, (ci + 1) * M2_CHUNK)
        o_ref[rows, :] = _ssd_chunk(x_ref[rows, :], b_ref[rows, :], c_ref[rows, :], z_ref[rows, :],
                                    arow_ref[:, rows], acol_ref[rows, :], ehm_ref[rows, :],
                                    exp_ref, dskip_ref, ng_ref, state)


def _ssd_chunk(xb, bm_b, cm_b, zb, arow, acol, ehm, exp_ref, dskip_ref, ng_ref, state):
    T = M2_CHUNK
    xs = xb.astype(F32)
    dte_e = jnp.dot(ehm, exp_ref[0], preferred_element_type=F32)
    eac_e = jnp.dot(ehm, exp_ref[1], preferred_element_type=F32)

    scores = lax.dot_general(cm_b, bm_b, (((1,), (1,)), ((), ())), preferred_element_type=F32)
    y_off = jnp.dot(cm_b, state[...].astype(BF16), preferred_element_type=F32) * eac_e

    causal = (lax.broadcasted_iota(jnp.int32, (T, T), 0) >= lax.broadcasted_iota(jnp.int32, (T, T), 1))
    left = lax.broadcasted_iota(jnp.int32, (T, LANE), 1) < M2_HEADDIM
    zero = jnp.zeros((T, LANE), BF16)
    ys = []
    for p in range(M2_HPG // 2):
        lhs = []
        for h in (2 * p, 2 * p + 1):
            seg = jnp.broadcast_to(acol[:, h:h + 1], (T, T)) - jnp.broadcast_to(arow[h:h + 1, :], (T, T))
            lhs.append((scores * jnp.exp(jnp.where(causal, seg, -1e30))).astype(BF16))
        xp = xb[:, p * LANE:(p + 1) * LANE]
        rhs = jnp.concatenate([jnp.where(left, xp, zero), jnp.where(left, zero, xp)], axis=0)
        ys.append(jnp.dot(jnp.concatenate(lhs, axis=1), rhs, preferred_element_type=F32))
    y = jnp.concatenate(ys, axis=1) + y_off + dskip_ref[...] * xs

    xw = (xs * dte_e).astype(BF16)
    state[...] = (state[...] * eac_e[T - 1:T, :]
                  + jnp.dot(bm_b.astype(F32).T.astype(BF16), xw, preferred_element_type=F32))

    y = y * zb.astype(F32)
    return (_rms(y) * ng_ref[...]).astype(BF16)


def _ssd(xbc, zs, arow, acol, ehm, d_skip, norm_g):
    seq = xbc.shape[0]
    T, gw, n = M2_CHUNK * M2_CHUNKS_PER_STEP, M2_GW, M2_STATE
    bb = M2_D_INNER // n
    cb = bb + M2_GROUPS
    col = jnp.arange(2 * LANE)[:, None] % LANE
    head = jnp.arange(gw)[None, :] // M2_HEADDIM
    expand = jnp.stack([col == head, col == head + M2_HPG]).astype(BF16)
    return pl.pallas_call(
        _ssd_body,
        out_shape=jax.ShapeDtypeStruct((seq, M2_D_INNER), BF16),
        grid=(M2_GROUPS, seq // T),
        in_specs=[pl.BlockSpec((T, gw), lambda g, c: (c, g)),
                  pl.BlockSpec((T, n), lambda g, c: (c, bb + g)),
                  pl.BlockSpec((T, n), lambda g, c: (c, cb + g)),
                  pl.BlockSpec((T, gw), lambda g, c: (c, Z_M2 // gw + g)),
                  pl.BlockSpec((M2_HPG, T), lambda g, c: (g, c)),
                  pl.BlockSpec((None, T, LANE), lambda g, c: (g, c, 0)),
                  pl.BlockSpec((None, T, 2 * LANE), lambda g, c: (g, c, 0)),
                  pl.BlockSpec((2, 2 * LANE, gw), lambda g, c: (0, 0, 0)),
                  pl.BlockSpec((1, gw), lambda g, c: (0, g)),
                  pl.BlockSpec((1, gw), lambda g, c: (0, g))],
        out_specs=pl.BlockSpec((T, gw), lambda g, c: (c, g)),
        scratch_shapes=[pltpu.VMEM((n, gw), F32)],
        compiler_params=_params("arbitrary", "arbitrary"),
        name="ssd",
    )(xbc, xbc, xbc, zs, arow, acol, ehm, expand,
      jnp.repeat(d_skip.astype(F32), M2_HEADDIM).reshape(1, M2_D_INNER),
      norm_g.astype(F32).reshape(1, M2_D_INNER))


def _merge_body(a1_ref, w1_ref, a2_ref, w2_ref, g1_ref, g2_ref, o_ref):
    m1 = jnp.dot(a1_ref[...], w1_ref[...], preferred_element_type=F32)
    m2 = jnp.dot(a2_ref[...], w2_ref[...], preferred_element_type=F32)
    o_ref[...] = (g1_ref[...].astype(F32) * m1 + g2_ref[...].astype(F32) * m2).astype(BF16)


def _merge(y1, w1, y2, w2, gates, tm, tn):
    seq = y1.shape[0]
    nb = D_MODEL // tn
    return pl.pallas_call(
        _merge_body,
        out_shape=jax.ShapeDtypeStruct((seq, D_MODEL), BF16),
        grid=(seq // tm, nb),
        in_specs=[pl.BlockSpec((tm, D_MODEL), lambda i, j: (i, 0)),
                  pl.BlockSpec((D_MODEL, tn), lambda i, j: (0, j)),
                  pl.BlockSpec((tm, M2_D_INNER), lambda i, j: (i, 0)),
                  pl.BlockSpec((M2_D_INNER, tn), lambda i, j: (0, j)),
                  pl.BlockSpec((tm, tn), lambda i, j: (i, j)),
                  pl.BlockSpec((tm, tn), lambda i, j: (i, nb + j))],
        out_specs=pl.BlockSpec((tm, tn), lambda i, j: (i, j)),
        compiler_params=_params("arbitrary", "arbitrary"),
        name="merge",
    )(y1, w1, y2, w2, gates, gates)


def _out_body(a_ref, w_hbm, x_ref, gate_ref, o_ref, wf32, wbf, sems, *, n_j):
    @pl.when(pl.program_id(1) == 0)
    def _():
        _fetch_cast_weights(w_hbm, (0,), wf32, wbf, sems, pl.program_id(0), n_j)

    m = jnp.dot(a_ref[...], wbf[...], preferred_element_type=F32)
    o_ref[...] = x_ref[...] + gate_ref[...] * m


def _out_proj(merged, w_out3, x2, mod, tm, tn):
    seq = merged.shape[0]
    nb = D_MODEL // tn
    return pl.pallas_call(
        functools.partial(_out_body, n_j=nb),
        out_shape=jax.ShapeDtypeStruct((seq, D_MODEL), F32),
        grid=(nb, seq // tm),
        in_specs=[pl.BlockSpec((tm, D_MODEL), lambda j, i: (i, 0)),
                  pl.BlockSpec(memory_space=pl.ANY),
                  pl.BlockSpec((tm, tn), lambda j, i: (i, j)),
                  pl.BlockSpec((1, tn), lambda j, i: (0, 2 * nb + j))],
        out_specs=pl.BlockSpec((tm, tn), lambda j, i: (i, j)),
        scratch_shapes=[pltpu.VMEM((D_MODEL, tn), F32), pltpu.VMEM((D_MODEL, tn), BF16),
                        pltpu.SemaphoreType.DMA((1,))],
        compiler_params=_params("arbitrary", "arbitrary"),
        name="outproj",
    )(merged, w_out3, x2, mod)


def _fnorm_body(x_ref, g_ref, o_ref):
    o_ref[...] = _rms(x_ref[...]) * g_ref[...]


def _final_norm(x2, g, tm):
    seq = x2.shape[0]
    return pl.pallas_call(
        _fnorm_body,
        out_shape=jax.ShapeDtypeStruct((seq, D_MODEL), F32),
        grid=(seq // tm,),
        in_specs=[pl.BlockSpec((tm, D_MODEL), lambda i: (i, 0)),
                  pl.BlockSpec((1, D_MODEL), lambda i: (0, 0))],
        out_specs=pl.BlockSpec((tm, D_MODEL), lambda i: (i, 0)),
        compiler_params=_params("arbitrary"),
        name="fnorm",
    )(x2, g)


def _tiles(seq):
    return dict(tm=min(1024, seq), tm_merge=min(512, seq), tm_norm=min(256, seq), tn=512, tn_in=1024, tn_glu=256)


def kernel(x, c, w_ada, b_ada, norm_g, w_in, s5_lambda_re, s5_lambda_im, s5_log_step, s5_b_re, s5_b_im,
           s5_c_re, s5_c_im, s5_d, s5_w_glu, s5_b_glu, m2_conv_w, m2_conv_b, m2_dt_bias, m2_a_log, m2_d,
           m2_norm_g, w_br_s5, w_br_m2, w_out, final_g):
    bsz, seq, _ = x.shape
    assert bsz == 1 and w_ada.shape[0] == 1 and seq % (S5_Q * SUBLANE) == 0 and seq % (M2_CHUNK * M2_CHUNKS_PER_STEP) == 0
    tl = _tiles(seq)
    tm, tn = tl["tm"], tl["tn"]
    x2 = x.reshape(seq, D_MODEL)

    w1, w2 = w_br_s5[0].astype(BF16), w_br_m2[0].astype(BF16)
    conv = (m2_conv_w[0].astype(F32), m2_conv_b[0].astype(F32).reshape(1, M2_CONV_DIM))

    mod = _ada_mod(c, w_ada[0], b_ada[0])
    h = _norm_mod(x2, norm_g[0].reshape(1, D_MODEL), mod, tl["tm_norm"])
    tn_in = tl["tn_in"]
    u = _proj(h, w_in, 0, D_MODEL, F32, tm, tn_in, "inproj_u")
    zs = _proj(h, w_in, WIN_S5Z, WIN_XBC - WIN_S5Z, BF16, tm, tn_in, "inproj_z", act=jax.nn.silu)
    xbc = _proj(h, w_in, WIN_XBC, M2_CONV_DIM, BF16, tm, tn_in, "inproj_x", conv=conv)
    gates = _proj(h, w_in, WIN_GATES, 2 * D_MODEL, BF16, tm, tn_in, "inproj_g", act=jax.nn.sigmoid)

    rows = _s5_rows(s5_lambda_re[0], s5_lambda_im[0], s5_log_step[0])
    wb, wct, w2t = _s5_prep(rows, s5_b_re[0], s5_b_im[0], s5_c_re[0], s5_c_im[0])
    ys5 = _s5(u, rows, wb, wct, w2t, s5_d[0])
    y1 = _glu(ys5, s5_w_glu, s5_b_glu[0].astype(F32).reshape(1, -1), zs, tm, tl["tn_glu"])

    arow, acol, ehm = _dt_prep(h, w_in, m2_dt_bias[0], m2_a_log[0])
    y2 = _ssd(xbc, zs, arow, acol, ehm, m2_d[0], m2_norm_g[0])

    merged = _merge(y1, w1, y2, w2, gates, tl["tm_merge"], tn)
    xn = _out_proj(merged, w_out, x2, mod, tm, tn)
    out = _final_norm(xn, final_g.astype(F32).reshape(1, D_MODEL), tl["tm_norm"])
    return out.reshape(bsz, seq, D_MODEL)
```

```python
import functools
import math

import jax
import jax.numpy as jnp
from jax import lax
from jax.experimental import pallas as pl
from jax.experimental.pallas import tpu as pltpu

F32 = jnp.float32
BF16 = jnp.bfloat16

LANE = 128
SUBLANE = 8
VMEM_LIMIT_BYTES = 60 * 1024 * 1024

D_MODEL = 4096
EPS = 1e-6

S5_GROUP = 16
S5_STATE = 64
S5_Q = 16
S5_TILE_GROUPS = LANE // S5_GROUP
S5_TILE_STATE = S5_TILE_GROUPS * S5_STATE
S5_TILES = D_MODEL // LANE

M2_D_INNER = 2 * D_MODEL
M2_HEADDIM = 64
M2_HEADS = M2_D_INNER // M2_HEADDIM
M2_GROUPS = 8
M2_HPG = M2_HEADS // M2_GROUPS
M2_STATE = 128
M2_CONV = 4
M2_CHUNK = 128
M2_CHUNKS_PER_STEP = 4
M2_GW = M2_HPG * M2_HEADDIM
M2_GN = M2_GROUPS * M2_STATE
M2_CONV_DIM = M2_D_INNER + 2 * M2_GN

WIN_S5Z = D_MODEL
WIN_XBC = 2 * D_MODEL + M2_D_INNER
WIN_DT = WIN_XBC + M2_CONV_DIM
WIN_GATES = WIN_DT + M2_HEADS
Z_S5 = 0
Z_M2 = D_MODEL


def _params(*sem):
    return pltpu.CompilerParams(dimension_semantics=sem, vmem_limit_bytes=VMEM_LIMIT_BYTES)


MOD_TN = 512


def _mod_body(cb_ref, w_ref, b_ref, o_ref):
    cb = cb_ref[...]
    for q in range(MOD_TN // LANE):
        sl = slice(q * LANE, (q + 1) * LANE)
        o_ref[:, sl] = jnp.sum(w_ref[:, sl] * cb, axis=0, keepdims=True) + b_ref[:, sl]


def _ada_mod(c, w_ada, b_ada):
    n = w_ada.shape[1]
    cb = jnp.broadcast_to(c.reshape(D_MODEL, 1), (D_MODEL, LANE))
    return pl.pallas_call(
        _mod_body,
        out_shape=jax.ShapeDtypeStruct((1, n), F32),
        grid=(n // MOD_TN,),
        in_specs=[pl.BlockSpec((D_MODEL, LANE), lambda j: (0, 0)),
                  pl.BlockSpec((D_MODEL, MOD_TN), lambda j: (0, j)),
                  pl.BlockSpec((1, MOD_TN), lambda j: (0, j))],
        out_specs=pl.BlockSpec((1, MOD_TN), lambda j: (0, j)),
        compiler_params=_params("arbitrary"),
        name="mod",
    )(cb, w_ada, b_ada.reshape(1, n))


def _rms(x):
    return x * lax.rsqrt(jnp.mean(x * x, axis=-1, keepdims=True) + EPS)


def _norm_body(x_ref, g_ref, shift_ref, scale_ref, o_ref):
    h = _rms(x_ref[...]) * g_ref[...] * (1.0 + scale_ref[...]) + shift_ref[...]
    o_ref[...] = h.astype(BF16)


def _norm_mod(x2, g, mod, tm):
    seq = x2.shape[0]
    return pl.pallas_call(
        _norm_body,
        out_shape=jax.ShapeDtypeStruct((seq, D_MODEL), BF16),
        grid=(seq // tm,),
        in_specs=[pl.BlockSpec((tm, D_MODEL), lambda i: (i, 0)),
                  pl.BlockSpec((1, D_MODEL), lambda i: (0, 0)),
                  pl.BlockSpec((1, D_MODEL), lambda i: (0, 0)),
                  pl.BlockSpec((1, D_MODEL), lambda i: (0, 1))],
        out_specs=pl.BlockSpec((tm, D_MODEL), lambda i: (i, 0)),
        compiler_params=_params("arbitrary"),
        name="norm",
    )(x2, g, mod, mod)


CAST_ROWS = 512


def _fetch_cast_weights(w_hbm, col_starts, wf32, wbf, sems, j, n_j):
    tn = wf32.shape[1] // len(col_starts)

    def copies(jj):
        return [pltpu.make_async_copy(w_hbm.at[0, :, pl.ds(pl.multiple_of(c0 + jj * tn, LANE), tn)],
                                      wf32.at[:, pl.ds(q * tn, tn)], sems.at[q])
                for q, c0 in enumerate(col_starts)]

    @pl.when(j == 0)
    def _():
        for cp in copies(0):
            cp.start()

    for cp in copies(j):
        cp.wait()
    for r0 in range(0, wf32.shape[0], CAST_ROWS):
        wbf[r0:r0 + CAST_ROWS, :] = wf32[r0:r0 + CAST_ROWS, :].astype(BF16)

    @pl.when(j + 1 < n_j)
    def _():
        for cp in copies(j + 1):
            cp.start()


def _proj_body(*refs, col0, n_j, act, conv):
    if conv:
        a_ref, w_hbm, cw_ref, cb_ref, o_ref, wf32, wbf, sems, cbuf = refs
    else:
        a_ref, w_hbm, o_ref, wf32, wbf, sems = refs
    tm, tn = o_ref.shape

    @pl.when(pl.program_id(1) == 0)
    def _():
        _fetch_cast_weights(w_hbm, (col0,), wf32, wbf, sems, pl.program_id(0), n_j)
        if conv:
            cbuf[:SUBLANE, :] = jnp.zeros((SUBLANE, tn), F32)

    r = jnp.dot(a_ref[...], wbf[...], preferred_element_type=F32)
    if conv:
        k = cw_ref.shape[0]
        cbuf[SUBLANE:, :] = r
        y = cb_ref[...] + cw_ref[k - 1:k, :] * r
        for d in range(1, k):
            y = y + cw_ref[k - 1 - d:k - d, :] * cbuf[pl.ds(SUBLANE - d, tm), :]
        o_ref[...] = jax.nn.silu(y).astype(o_ref.dtype)
        cbuf[:SUBLANE, :] = r[tm - SUBLANE:]
    else:
        o_ref[...] = (act(r) if act else r).astype(o_ref.dtype)


def _proj(h, w3, col0, n, out_dtype, tm, tn, name, act=None, conv=None):
    seq, k = h.shape
    assert col0 % LANE == 0 and n % tn == 0
    extra, extra_specs = [], []
    scratch = [pltpu.VMEM((k, tn), F32), pltpu.VMEM((k, tn), BF16), pltpu.SemaphoreType.DMA((1,))]
    if conv:
        extra = list(conv)
        extra_specs = [pl.BlockSpec((conv[0].shape[0], tn), lambda j, i: (0, j)),
                       pl.BlockSpec((1, tn), lambda j, i: (0, j))]
        scratch.append(pltpu.VMEM((tm + SUBLANE, tn), F32))
    return pl.pallas_call(
        functools.partial(_proj_body, col0=col0, n_j=n // tn, act=act, conv=bool(conv)),
        out_shape=jax.ShapeDtypeStruct((seq, n), out_dtype),
        grid=(n // tn, seq // tm),
        in_specs=[pl.BlockSpec((tm, k), lambda j, i: (i, 0)), pl.BlockSpec(memory_space=pl.ANY)] + extra_specs,
        out_specs=pl.BlockSpec((tm, tn), lambda j, i: (i, j)),
        scratch_shapes=scratch,
        compiler_params=_params("arbitrary", "arbitrary"),
        name=name,
    )(h, w3, *extra)


def _lbar(lre, lim, log_step):
    step = jnp.exp(log_step)
    mag = jnp.exp(lre * step)
    return mag * jnp.cos(lim * step), mag * jnp.sin(lim * step)


def _s5prep_body(lre_ref, lim_ref, st_ref, btr_ref, bti_ref, ctr_ref, cti_ref,
                 wb_ref, wct_ref, w2_ref):
    ns = S5_TILE_STATE
    lre, lim = lre_ref[...], lim_ref[...]
    lbr, lbi = _lbar(lre, lim, st_ref[...])
    den = lre * lre + lim * lim
    nr, ni = lbr - 1.0, lbi
    cr = (nr * lre + ni * lim) / den
    ci = (ni * lre - nr * lim) / den
    row = jnp.right_shift(lax.broadcasted_iota(jnp.int32, (LANE, ns), 0), int(math.log2(S5_GROUP)))
    col = jnp.right_shift(lax.broadcasted_iota(jnp.int32, (LANE, ns), 1), int(math.log2(S5_STATE)))
    same = row == col
    btr = jnp.where(same, btr_ref[...], 0.0)
    bti = jnp.where(same, bti_ref[...], 0.0)
    ctr = jnp.where(same, ctr_ref[...], 0.0)
    cti = jnp.where(same, cti_ref[...], 0.0)
    bbr = cr * btr - ci * bti
    bbi = cr * bti + ci * btr
    pows = [(jnp.ones_like(lbr), jnp.zeros_like(lbr))]
    for _ in range(S5_Q):
        pr, pi = pows[-1]
        pows.append((pr * lbr - pi * lbi, pr * lbi + pi * lbr))
    for s in range(S5_Q):
        pr, pi = pows[S5_Q - 1 - s]
        rows = slice(s * LANE, (s + 1) * LANE)
        wb_ref[rows, :ns] = (pr * bbr - pi * bbi).astype(BF16)
        wb_ref[rows, ns:] = (pr * bbi + pi * bbr).astype(BF16)
    for t in range(S5_Q):
        pr, pi = pows[t + 1]
        rows = slice(t * LANE, (t + 1) * LANE)
        wct_ref[rows, :ns] = (ctr * pr - cti * pi).astype(BF16)
        wct_ref[rows, ns:] = (-(ctr * pi + cti * pr)).astype(BF16)
    ccat = jnp.concatenate([ctr, -cti], axis=1).astype(BF16)
    kall = lax.dot_general(wb_ref[...], ccat, (((1,), (1,)), ((), ())),
                           preferred_element_type=F32)
    w2_ref[LANE:, :LANE] = jnp.zeros((LANE, LANE), BF16)
    for tau in range(S5_Q):
        s = S5_Q - 1 - tau
        k_tau = kall[s * LANE:(s + 1) * LANE, :].astype(BF16)
        w2_ref[:LANE, tau * LANE:(tau + 1) * LANE] = k_tau
        if tau + 1 < S5_Q:
            w2_ref[LANE:, (tau + 1) * LANE:(tau + 2) * LANE] = k_tau


def _s5_rows(lam_re, lam_im, log_step):
    row = lambda a: a.astype(F32).reshape(S5_TILES, 1, S5_TILE_STATE)
    return row(lam_re), row(lam_im), row(jnp.broadcast_to(log_step[:, None], lam_re.shape))


def _s5_prep(rows, b_re, b_im, c_re, c_im):
    ns = S5_TILE_STATE

    def tiled(a):
        a = a.astype(F32).reshape(S5_TILES, LANE, S5_STATE)
        return jnp.tile(a, (1, 1, S5_TILE_GROUPS))

    row_spec = pl.BlockSpec((None, 1, ns), lambda j: (j, 0, 0))
    mat_spec = pl.BlockSpec((None, LANE, ns), lambda j: (j, 0, 0))
    return pl.pallas_call(
        _s5prep_body,
        out_shape=(jax.ShapeDtypeStruct((S5_TILES, S5_Q * LANE, 2 * ns), BF16),
                   jax.ShapeDtypeStruct((S5_TILES, S5_Q * LANE, 2 * ns), BF16),
                   jax.ShapeDtypeStruct((S5_TILES, 2 * LANE, S5_Q * LANE), BF16)),
        grid=(S5_TILES,),
        in_specs=[row_spec, row_spec, row_spec, mat_spec, mat_spec, mat_spec, mat_spec],
        out_specs=(pl.BlockSpec((None, S5_Q * LANE, 2 * ns), lambda j: (j, 0, 0)),
                   pl.BlockSpec((None, S5_Q * LANE, 2 * ns), lambda j: (j, 0, 0)),
                   pl.BlockSpec((None, 2 * LANE, S5_Q * LANE), lambda j: (j, 0, 0))),
        compiler_params=_params("arbitrary"),
        name="s5prep",
    )(*rows, tiled(b_re.swapaxes(1, 2)), tiled(b_im.swapaxes(1, 2)), tiled(c_re), tiled(c_im))


def _s5_body(u_ref, lre_ref, lim_ref, st_ref, wb_ref, wct_ref, w2_ref, d_ref, o_ref, e_scr, acc, ynat):
    ns = S5_TILE_STATE
    nc = e_scr.shape[0]
    slabs = [u_ref[pl.ds(s, nc, stride=S5_Q), :] for s in range(S5_Q)]
    sb = [x.astype(BF16) for x in slabs]

    e_scr[...] = jnp.dot(jnp.concatenate(sb, axis=1), wb_ref[...], preferred_element_type=F32)
    ar, ai = _lbar(lre_ref[...], lim_ref[...], st_ref[...])
    for _ in range(int(math.log2(S5_Q))):
        ar, ai = ar * ar - ai * ai, 2.0 * ar * ai

    n_pairs = S5_Q // 2
    seg = nc // n_pairs
    hr = hi = jnp.zeros((1, ns), F32)
    for p in range(n_pairs):
        for c in range(p * seg, (p + 1) * seg):
            er, ei = e_scr[c:c + 1, :ns], e_scr[c:c + 1, ns:]
            e_scr[c:c + 1, :ns] = hr
            e_scr[c:c + 1, ns:] = hi
            hr, hi = ar * hr - ai * hi + er, ar * hi + ai * hr + ei
        s = 2 * p
        pair = jnp.concatenate([sb[s], sb[s + 1]], axis=1)
        width = (S5_Q - s) * LANE
        toep = jnp.dot(pair, w2_ref[:, :width], preferred_element_type=F32)
        if p == 0:
            acc[...] = toep
        else:
            acc[:, s * LANE:] += toep

    acc[...] += lax.dot_general(e_scr[...].astype(BF16), wct_ref[...], (((1,), (1,)), ((), ())),
                                preferred_element_type=F32)
    d = d_ref[...]
    for t in range(S5_Q):
        y = acc[:, t * LANE:(t + 1) * LANE] + d * slabs[t]
        ynat[pl.ds(t, nc, stride=S5_Q), :] = jax.nn.gelu(y)
    o_ref[...] = ynat[...].astype(BF16)


def _s5(u, rows, wb, wct, w2, d_skip):
    seq = u.shape[0]
    nc, ns = seq // S5_Q, S5_TILE_STATE
    row_spec = pl.BlockSpec((None, 1, ns), lambda j: (j, 0, 0))
    return pl.pallas_call(
        _s5_body,
        out_shape=jax.ShapeDtypeStruct((seq, D_MODEL), BF16),
        grid=(S5_TILES,),
        in_specs=[pl.BlockSpec((seq, LANE), lambda j: (0, j)), row_spec, row_spec, row_spec,
                  pl.BlockSpec((None, S5_Q * LANE, 2 * ns), lambda j: (j, 0, 0)),
                  pl.BlockSpec((None, S5_Q * LANE, 2 * ns), lambda j: (j, 0, 0)),
                  pl.BlockSpec((None, 2 * LANE, S5_Q * LANE), lambda j: (j, 0, 0)),
                  pl.BlockSpec((1, LANE), lambda j: (0, j))],
        out_specs=pl.BlockSpec((seq, LANE), lambda j: (0, j)),
        scratch_shapes=[pltpu.VMEM((nc, 2 * ns), F32), pltpu.VMEM((nc, S5_Q * LANE), F32),
                        pltpu.VMEM((seq, LANE), F32)],
        compiler_params=_params("arbitrary"),
        name="s5",
    )(u, *rows, wb, wct, w2, d_skip.astype(F32).reshape(1, D_MODEL))


def _glu_body(a_ref, w_hbm, ba_ref, bb_ref, z_ref, o_ref, wf32, wbf, sems, *, n_j):
    tn = o_ref.shape[1]

    @pl.when(pl.program_id(1) == 0)
    def _():
        _fetch_cast_weights(w_hbm, (0, n_j * tn), wf32, wbf, sems, pl.program_id(0), n_j)

    g = jnp.dot(a_ref[...], wbf[...], preferred_element_type=F32)
    ga = g[:, :tn] + ba_ref[...]
    gb = g[:, tn:] + bb_ref[...]
    o_ref[...] = (ga * jax.nn.sigmoid(gb) * z_ref[...].astype(F32)).astype(BF16)


def _glu(ys5, w_glu3, b_glu, zs, tm, tn):
    seq = ys5.shape[0]
    nb = D_MODEL // tn
    return pl.pallas_call(
        functools.partial(_glu_body, n_j=nb),
        out_shape=jax.ShapeDtypeStruct((seq, D_MODEL), BF16),
        grid=(nb, seq // tm),
        in_specs=[pl.BlockSpec((tm, D_MODEL), lambda j, i: (i, 0)),
                  pl.BlockSpec(memory_space=pl.ANY),
                  pl.BlockSpec((1, tn), lambda j, i: (0, j)),
                  pl.BlockSpec((1, tn), lambda j, i: (0, nb + j)),
                  pl.BlockSpec((tm, tn), lambda j, i: (i, Z_S5 // tn + j))],
        out_specs=pl.BlockSpec((tm, tn), lambda j, i: (i, j)),
        scratch_shapes=[pltpu.VMEM((D_MODEL, 2 * tn), F32), pltpu.VMEM((D_MODEL, 2 * tn), BF16),
                        pltpu.SemaphoreType.DMA((2,))],
        compiler_params=_params("arbitrary", "arbitrary"),
        name="glu",
    )(ys5, w_glu3, b_glu, b_glu, zs)


def _split3(x):
    hi = x.astype(BF16)
    r1 = x - hi.astype(F32)
    mid = r1.astype(BF16)
    lo = (r1 - mid.astype(F32)).astype(BF16)
    return hi, mid, lo


def _split2(x):
    hi = x.astype(BF16)
    mid = (x - hi.astype(F32)).astype(BF16)
    return jnp.concatenate([hi, mid], axis=1)


def _dtprep_body(h_ref, w_ref, dtb_ref, alog_ref, tri_ref, arow_ref, acol_ref, ehm_ref, wbf):
    T = M2_CHUNK

    @pl.when(pl.program_id(0) == 0)
    def _():
        wbf[...] = w_ref[...].astype(BF16)

    raw = jnp.dot(h_ref[...].astype(F32), wbf[...].astype(F32), preferred_element_type=F32)
    tri = tri_ref[...]
    neg_a = -jnp.exp(alog_ref[...])
    pad1 = jnp.zeros((T - M2_HPG, T), F32)
    pad2 = jnp.zeros((T - 2 * M2_HPG, T), F32)
    for ci in range(M2_CHUNKS_PER_STEP):
        rows = slice(ci * T, (ci + 1) * T)
        dt = jax.nn.softplus(raw[rows, :].T + dtb_ref[...])
        hi, mid, lo = _split3(dt * neg_a)
        acum = (jnp.dot(hi, tri, preferred_element_type=F32) + jnp.dot(mid, tri, preferred_element_type=F32)
                + jnp.dot(lo, tri, preferred_element_type=F32))
        rem = acum[:, T - 1:T] - acum
        arow_ref[:, rows] = acum - jnp.log(dt)
        dte = dt * jnp.exp(rem)
        eac = jnp.exp(acum)
        for g in range(M2_GROUPS):
            hs = slice(g * M2_HPG, (g + 1) * M2_HPG)
            acol_ref[g, rows, :] = jnp.concatenate([acum[hs], pad1], axis=0).T
            ehm_ref[g, rows, :] = _split2(jnp.concatenate([dte[hs], eac[hs], pad2], axis=0).T)


def _dt_prep(h, w_in3, dt_bias, a_log):
    seq = h.shape[0]
    T, ts = M2_CHUNK, M2_CHUNK * M2_CHUNKS_PER_STEP
    lanes = lambda v: jnp.broadcast_to(v.astype(F32)[:, None], (M2_HEADS, T))
    tri = (jnp.arange(T)[:, None] <= jnp.arange(T)[None, :]).astype(BF16)
    return pl.pallas_call(
        _dtprep_body,
        out_shape=(jax.ShapeDtypeStruct((M2_HEADS, seq), F32),
                   jax.ShapeDtypeStruct((M2_GROUPS, seq, LANE), F32),
                   jax.ShapeDtypeStruct((M2_GROUPS, seq, 2 * LANE), BF16)),
        grid=(seq // ts,),
        in_specs=[pl.BlockSpec((ts, D_MODEL), lambda c: (c, 0)),
                  pl.BlockSpec((None, D_MODEL, M2_HEADS), lambda c: (0, 0, WIN_DT // M2_HEADS)),
                  pl.BlockSpec((M2_HEADS, T), lambda c: (0, 0)),
                  pl.BlockSpec((M2_HEADS, T), lambda c: (0, 0)),
                  pl.BlockSpec((T, T), lambda c: (0, 0))],
        out_specs=(pl.BlockSpec((M2_HEADS, ts), lambda c: (0, c)),
                   pl.BlockSpec((M2_GROUPS, ts, LANE), lambda c: (0, c, 0)),
                   pl.BlockSpec((M2_GROUPS, ts, 2 * LANE), lambda c: (0, c, 0))),
        scratch_shapes=[pltpu.VMEM((D_MODEL, M2_HEADS), BF16)],
        compiler_params=_params("arbitrary"),
        name="dtprep",
    )(h, w_in3, lanes(dt_bias), lanes(a_log), tri)


def _ssd_body(x_ref, b_ref, c_ref, z_ref, arow_ref, acol_ref, ehm_ref, exp_ref, dskip_ref, ng_ref, o_ref, state):
    @pl.when(pl.program_id(1) == 0)
    def _():
        state[...] = jnp.zeros_like(state)

    for ci in range(M2_CHUNKS_PER_STEP):
        rows = slice(ci * M2_CHUNK, (ci + 1) * M2_CHUNK)
        o_ref[rows, :] = _ssd_chunk(x_ref[rows, :], b_ref[rows, :], c_ref[rows, :], z_ref[rows, :],
                                    arow_ref[:, rows], acol_ref[rows, :], ehm_ref[rows, :],
                                    exp_ref, dskip_ref, ng_ref, state)


def _ssd_chunk(xb, bm_b, cm_b, zb, arow, acol, ehm, exp_ref, dskip_ref, ng_ref, state):
    T = M2_CHUNK
    xs = xb.astype(F32)
    dte_e = jnp.dot(ehm, exp_ref[0], preferred_element_type=F32)
    eac_e = jnp.dot(ehm, exp_ref[1], preferred_element_type=F32)

    scores = lax.dot_general(cm_b, bm_b, (((1,), (1,)), ((), ())), preferred_element_type=F32)
    y_off = jnp.dot(cm_b, state[...].astype(BF16), preferred_element_type=F32) * eac_e

    causal = (lax.broadcasted_iota(jnp.int32, (T, T), 0) >= lax.broadcasted_iota(jnp.int32, (T, T), 1))
    left = lax.broadcasted_iota(jnp.int32, (T, LANE), 1) < M2_HEADDIM
    zero = jnp.zeros((T, LANE), BF16)
    ys = []
    for p in range(M2_HPG // 2):
        lhs = []
        for h in (2 * p, 2 * p + 1):
            seg = jnp.broadcast_to(acol[:, h:h + 1], (T, T)) - jnp.broadcast_to(arow[h:h + 1, :], (T, T))
            lhs.append((scores * jnp.exp(jnp.where(causal, seg, -1e30))).astype(BF16))
        xp = xb[:, p * LANE:(p + 1) * LANE]
        rhs = jnp.concatenate([jnp.where(left, xp, zero), jnp.where(left, zero, xp)], axis=0)
        ys.append(jnp.dot(jnp.concatenate(lhs, axis=1), rhs, preferred_element_type=F32))
    y = jnp.concatenate(ys, axis=1) + y_off + dskip_ref[...] * xs

    xw = (xs * dte_e).astype(BF16)
    state[...] = (state[...] * eac_e[T - 1:T, :]
                  + jnp.dot(bm_b.astype(F32).T.astype(BF16), xw, preferred_element_type=F32))

    y = y * zb.astype(F32)
    return (_rms(y) * ng_ref[...]).astype(BF16)


def _ssd(xbc, zs, arow, acol, ehm, d_skip, norm_g):
    seq = xbc.shape[0]
    T, gw, n = M2_CHUNK * M2_CHUNKS_PER_STEP, M2_GW, M2_STATE
    bb = M2_D_INNER // n
    cb = bb + M2_GROUPS
    col = jnp.arange(2 * LANE)[:, None] % LANE
    head = jnp.arange(gw)[None, :] // M2_HEADDIM
    expand = jnp.stack([col == head, col == head + M2_HPG]).astype(BF16)
    return pl.pallas_call(
        _ssd_body,
        out_shape=jax.ShapeDtypeStruct((seq, M2_D_INNER), BF16),
        grid=(M2_GROUPS, seq // T),
        in_specs=[pl.BlockSpec((T, gw), lambda g, c: (c, g)),
                  pl.BlockSpec((T, n), lambda g, c: (c, bb + g)),
                  pl.BlockSpec((T, n), lambda g, c: (c, cb + g)),
                  pl.BlockSpec((T, gw), lambda g, c: (c, Z_M2 // gw + g)),
                  pl.BlockSpec((M2_HPG, T), lambda g, c: (g, c)),
                  pl.BlockSpec((None, T, LANE), lambda g, c: (g, c, 0)),
                  pl.BlockSpec((None, T, 2 * LANE), lambda g, c: (g, c, 0)),
                  pl.BlockSpec((2, 2 * LANE, gw), lambda g, c: (0, 0, 0)),
                  pl.BlockSpec((1, gw), lambda g, c: (0, g)),
                  pl.BlockSpec((1, gw), lambda g, c: (0, g))],
        out_specs=pl.BlockSpec((T, gw), lambda g, c: (c, g)),
        scratch_shapes=[pltpu.VMEM((n, gw), F32)],
        compiler_params=_params("arbitrary", "arbitrary"),
        name="ssd",
    )(xbc, xbc, xbc, zs, arow, acol, ehm, expand,
      jnp.repeat(d_skip.astype(F32), M2_HEADDIM).reshape(1, M2_D_INNER),
      norm_g.astype(F32).reshape(1, M2_D_INNER))


def _merge_body(a1_ref, w1_ref, a2_ref, w2_ref, g1_ref, g2_ref, o_ref):
    m1 = jnp.dot(a1_ref[...], w1_ref[...], preferred_element_type=F32)
    m2 = jnp.dot(a2_ref[...], w2_ref[...], preferred_element_type=F32)
    o_ref[...] = (g1_ref[...].astype(F32) * m1 + g2_ref[...].astype(F32) * m2).astype(BF16)


def _merge(y1, w1, y2, w2, gates, tm, tn):
    seq = y1.shape[0]
    nb = D_MODEL // tn
    return pl.pallas_call(
        _merge_body,
        out_shape=jax.ShapeDtypeStruct((seq, D_MODEL), BF16),
        grid=(seq // tm, nb),
        in_specs=[pl.BlockSpec((tm, D_MODEL), lambda i, j: (i, 0)),
                  pl.BlockSpec((D_MODEL, tn), lambda i, j: (0, j)),
                  pl.BlockSpec((tm, M2_D_INNER), lambda i, j: (i, 0)),
                  pl.BlockSpec((M2_D_INNER, tn), lambda i, j: (0, j)),
                  pl.BlockSpec((tm, tn), lambda i, j: (i, j)),
                  pl.BlockSpec((tm, tn), lambda i, j: (i, nb + j))],
        out_specs=pl.BlockSpec((tm, tn), lambda i, j: (i, j)),
        compiler_params=_params("arbitrary", "arbitrary"),
        name="merge",
    )(y1, w1, y2, w2, gates, gates)


def _out_body(a_ref, w_hbm, x_ref, gate_ref, o_ref, wf32, wbf, sems, *, n_j):
    @pl.when(pl.program_id(1) == 0)
    def _():
        _fetch_cast_weights(w_hbm, (0,), wf32, wbf, sems, pl.program_id(0), n_j)

    m = jnp.dot(a_ref[...], wbf[...], preferred_element_type=F32)
    o_ref[...] = x_ref[...] + gate_ref[...] * m


def _out_proj(merged, w_out3, x2, mod, tm, tn):
    seq = merged.shape[0]
    nb = D_MODEL // tn
    return pl.pallas_call(
        functools.partial(_out_body, n_j=nb),
        out_shape=jax.ShapeDtypeStruct((seq, D_MODEL), F32),
        grid=(nb, seq // tm),
        in_specs=[pl.BlockSpec((tm, D_MODEL), lambda j, i: (i, 0)),
                  pl.BlockSpec(memory_space=pl.ANY),
                  pl.BlockSpec((tm, tn), lambda j, i: (i, j)),
                  pl.BlockSpec((1, tn), lambda j, i: (0, 2 * nb + j))],
        out_specs=pl.BlockSpec((tm, tn), lambda j, i: (i, j)),
        scratch_shapes=[pltpu.VMEM((D_MODEL, tn), F32), pltpu.VMEM((D_MODEL, tn), BF16),
                        pltpu.SemaphoreType.DMA((1,))],
        compiler_params=_params("arbitrary", "arbitrary"),
        name="outproj",
    )(merged, w_out3, x2, mod)


def _fnorm_body(x_ref, g_ref, o_ref):
    o_ref[...] = _rms(x_ref[...]) * g_ref[...]


def _final_norm(x2, g, tm):
    seq = x2.shape[0]
    return pl.pallas_call(
        _fnorm_body,
        out_shape=jax.ShapeDtypeStruct((seq, D_MODEL), F32),
        grid=(seq // tm,),
        in_specs=[pl.BlockSpec((tm, D_MODEL), lambda i: (i, 0)),
                  pl.BlockSpec((1, D_MODEL), lambda i: (0, 0))],
        out_specs=pl.BlockSpec((tm, D_MODEL), lambda i: (i, 0)),
        compiler_params=_params("arbitrary"),
        name="fnorm",
    )(x2, g)


def _tiles(seq):
    return dict(tm=min(1024, seq), tm_merge=min(512, seq), tm_norm=min(256, seq), tn=512, tn_in=1024, tn_glu=256)


def kernel(x, c, w_ada, b_ada, norm_g, w_in, s5_lambda_re, s5_lambda_im, s5_log_step, s5_b_re, s5_b_im,
           s5_c_re, s5_c_im, s5_d, s5_w_glu, s5_b_glu, m2_conv_w, m2_conv_b, m2_dt_bias, m2_a_log, m2_d,
           m2_norm_g, w_br_s5, w_br_m2, w_out, final_g):
    bsz, seq, _ = x.shape
    assert bsz == 1 and w_ada.shape[0] == 1 and seq % (S5_Q * SUBLANE) == 0 and seq % (M2_CHUNK * M2_CHUNKS_PER_STEP) == 0
    tl = _tiles(seq)
    tm, tn = tl["tm"], tl["tn"]
    x2 = x.reshape(seq, D_MODEL)

    w1, w2 = w_br_s5[0].astype(BF16), w_br_m2[0].astype(BF16)
    conv = (m2_conv_w[0].astype(F32), m2_conv_b[0].astype(F32).reshape(1, M2_CONV_DIM))

    mod = _ada_mod(c, w_ada[0], b_ada[0])
    h = _norm_mod(x2, norm_g[0].reshape(1, D_MODEL), mod, tl["tm_norm"])
    tn_in = tl["tn_in"]
    u = _proj(h, w_in, 0, D_MODEL, F32, tm, tn_in, "inproj_u")
    zs = _proj(h, w_in, WIN_S5Z, WIN_XBC - WIN_S5Z, BF16, tm, tn_in, "inproj_z", act=jax.nn.silu)
    xbc = _proj(h, w_in, WIN_XBC, M2_CONV_DIM, BF16, tm, tn_in, "inproj_x", conv=conv)
    gates = _proj(h, w_in, WIN_GATES, 2 * D_MODEL, BF16, tm, tn_in, "inproj_g", act=jax.nn.sigmoid)

    rows = _s5_rows(s5_lambda_re[0], s5_lambda_im[0], s5_log_step[0])
    wb, wct, w2t = _s5_prep(rows, s5_b_re[0], s5_b_im[0], s5_c_re[0], s5_c_im[0])
    ys5 = _s5(u, rows, wb, wct, w2t, s5_d[0])
    y1 = _glu(ys5, s5_w_glu, s5_b_glu[0].astype(F32).reshape(1, -1), zs, tm, tl["tn_glu"])

    arow, acol, ehm = _dt_prep(h, w_in, m2_dt_bias[0], m2_a_log[0])
    y2 = _ssd(xbc, zs, arow, acol, ehm, m2_d[0], m2_norm_g[0])

    merged = _merge(y1, w1, y2, w2, gates, tl["tm_merge"], tn)
    xn = _out_proj(merged, w_out, x2, mod, tm, tn)
    out = _final_norm(xn, final_g.astype(F32).reshape(1, D_MODEL), tl["tm_norm"])
    return out.reshape(bsz, seq, D_MODEL)
```

```python
import functools
import math

import jax
import jax.numpy as jnp
from jax import lax
from jax.experimental import pallas as pl
from jax.experimental.pallas import tpu as pltpu

F32 = jnp.float32
BF16 = jnp.bfloat16

LANE = 128
SUBLANE = 8
VMEM_LIMIT_BYTES = 60 * 1024 * 1024

D_MODEL = 4096
EPS = 1e-6

S5_GROUP = 16
S5_STATE = 64
S5_Q = 8
S5_TILE_GROUPS = LANE // S5_GROUP
S5_TILE_STATE = S5_TILE_GROUPS * S5_STATE
S5_TILES = D_MODEL // LANE

M2_D_INNER = 2 * D_MODEL
M2_HEADDIM = 64
M2_HEADS = M2_D_INNER // M2_HEADDIM
M2_GROUPS = 8
M2_HPG = M2_HEADS // M2_GROUPS
M2_STATE = 128
M2_CONV = 4
M2_CHUNK = 128
M2_CHUNKS_PER_STEP = 4
M2_GW = M2_HPG * M2_HEADDIM
M2_GN = M2_GROUPS * M2_STATE
M2_CONV_DIM = M2_D_INNER + 2 * M2_GN

WIN_S5Z = D_MODEL
WIN_XBC = 2 * D_MODEL + M2_D_INNER
WIN_DT = WIN_XBC + M2_CONV_DIM
WIN_GATES = WIN_DT + M2_HEADS
Z_S5 = 0
Z_M2 = D_MODEL


def _params(*sem):
    return pltpu.CompilerParams(dimension_semantics=sem, vmem_limit_bytes=VMEM_LIMIT_BYTES)


MOD_TN = 512


def _mod_body(cb_ref, w_ref, b_ref, o_ref):
    cb = cb_ref[...]
    for q in range(MOD_TN // LANE):
        sl = slice(q * LANE, (q + 1) * LANE)
        o_ref[:, sl] = jnp.sum(w_ref[:, sl] * cb, axis=0, keepdims=True) + b_ref[:, sl]


def _ada_mod(c, w_ada, b_ada):
    n = w_ada.shape[1]
    cb = jnp.broadcast_to(c.reshape(D_MODEL, 1), (D_MODEL, LANE))
    return pl.pallas_call(
        _mod_body,
        out_shape=jax.ShapeDtypeStruct((1, n), F32),
        grid=(n // MOD_TN,),
        in_specs=[pl.BlockSpec((D_MODEL, LANE), lambda j: (0, 0)),
                  pl.BlockSpec((D_MODEL, MOD_TN), lambda j: (0, j)),
                  pl.BlockSpec((1, MOD_TN), lambda j: (0, j))],
        out_specs=pl.BlockSpec((1, MOD_TN), lambda j: (0, j)),
        compiler_params=_params("arbitrary"),
        name="mod",
    )(cb, w_ada, b_ada.reshape(1, n))


def _rms(x):
    return x * lax.rsqrt(jnp.mean(x * x, axis=-1, keepdims=True) + EPS)


def _norm_body(x_ref, g_ref, shift_ref, scale_ref, o_ref):
    h = _rms(x_ref[...]) * g_ref[...] * (1.0 + scale_ref[...]) + shift_ref[...]
    o_ref[...] = h.astype(BF16)


def _norm_mod(x2, g, mod, tm):
    seq = x2.shape[0]
    return pl.pallas_call(
        _norm_body,
        out_shape=jax.ShapeDtypeStruct((seq, D_MODEL), BF16),
        grid=(seq // tm,),
        in_specs=[pl.BlockSpec((tm, D_MODEL), lambda i: (i, 0)),
                  pl.BlockSpec((1, D_MODEL), lambda i: (0, 0)),
                  pl.BlockSpec((1, D_MODEL), lambda i: (0, 0)),
                  pl.BlockSpec((1, D_MODEL), lambda i: (0, 1))],
        out_specs=pl.BlockSpec((tm, D_MODEL), lambda i: (i, 0)),
        compiler_params=_params("arbitrary"),
        name="norm",
    )(x2, g, mod, mod)


CAST_ROWS = 512


def _fetch_cast_weights(w_hbm, col_starts, wf32, wbf, sems, j, n_j):
    tn = wf32.shape[1] // len(col_starts)

    def copies(jj):
        return [pltpu.make_async_copy(w_hbm.at[0, :, pl.ds(pl.multiple_of(c0 + jj * tn, LANE), tn)],
                                      wf32.at[:, pl.ds(q * tn, tn)], sems.at[q])
                for q, c0 in enumerate(col_starts)]

    @pl.when(j == 0)
    def _():
        for cp in copies(0):
            cp.start()

    for cp in copies(j):
        cp.wait()
    for r0 in range(0, wf32.shape[0], CAST_ROWS):
        wbf[r0:r0 + CAST_ROWS, :] = wf32[r0:r0 + CAST_ROWS, :].astype(BF16)

    @pl.when(j + 1 < n_j)
    def _():
        for cp in copies(j + 1):
            cp.start()


def _proj_body(*refs, col0, n_j, act, conv):
    if conv:
        a_ref, w_hbm, cw_ref, cb_ref, o_ref, wf32, wbf, sems, cbuf = refs
    else:
        a_ref, w_hbm, o_ref, wf32, wbf, sems = refs
    tm, tn = o_ref.shape

    @pl.when(pl.program_id(1) == 0)
    def _():
        _fetch_cast_weights(w_hbm, (col0,), wf32, wbf, sems, pl.program_id(0), n_j)
        if conv:
            cbuf[:SUBLANE, :] = jnp.zeros((SUBLANE, tn), F32)

    r = jnp.dot(a_ref[...], wbf[...], preferred_element_type=F32)
    if conv:
        k = cw_ref.shape[0]
        cbuf[SUBLANE:, :] = r
        y = cb_ref[...] + cw_ref[k - 1:k, :] * r
        for d in range(1, k):
            y = y + cw_ref[k - 1 - d:k - d, :] * cbuf[pl.ds(SUBLANE - d, tm), :]
        o_ref[...] = jax.nn.silu(y).astype(o_ref.dtype)
        cbuf[:SUBLANE, :] = r[tm - SUBLANE:]
    else:
        o_ref[...] = (act(r) if act else r).astype(o_ref.dtype)


def _proj(h, w3, col0, n, out_dtype, tm, tn, name, act=None, conv=None):
    seq, k = h.shape
    assert col0 % LANE == 0 and n % tn == 0
    extra, extra_specs = [], []
    scratch = [pltpu.VMEM((k, tn), F32), pltpu.VMEM((k, tn), BF16), pltpu.SemaphoreType.DMA((1,))]
    if conv:
        extra = list(conv)
        extra_specs = [pl.BlockSpec((conv[0].shape[0], tn), lambda j, i: (0, j)),
                       pl.BlockSpec((1, tn), lambda j, i: (0, j))]
        scratch.append(pltpu.VMEM((tm + SUBLANE, tn), F32))
    return pl.pallas_call(
        functools.partial(_proj_body, col0=col0, n_j=n // tn, act=act, conv=bool(conv)),
        out_shape=jax.ShapeDtypeStruct((seq, n), out_dtype),
        grid=(n // tn, seq // tm),
        in_specs=[pl.BlockSpec((tm, k), lambda j, i: (i, 0)), pl.BlockSpec(memory_space=pl.ANY)] + extra_specs,
        out_specs=pl.BlockSpec((tm, tn), lambda j, i: (i, j)),
        scratch_shapes=scratch,
        compiler_params=_params("arbitrary", "arbitrary"),
        name=name,
    )(h, w3, *extra)


def _lbar(lre, lim, log_step):
    step = jnp.exp(log_step)
    mag = jnp.exp(lre * step)
    return mag * jnp.cos(lim * step), mag * jnp.sin(lim * step)


def _s5prep_body(lre_ref, lim_ref, st_ref, btr_ref, bti_ref, ctr_ref, cti_ref,
                 wb_ref, wct_ref, w2_ref):
    ns = S5_TILE_STATE
    lre, lim = lre_ref[...], lim_ref[...]
    lbr, lbi = _lbar(lre, lim, st_ref[...])
    den = lre * lre + lim * lim
    nr, ni = lbr - 1.0, lbi
    cr = (nr * lre + ni * lim) / den
    ci = (ni * lre - nr * lim) / den
    row = jnp.right_shift(lax.broadcasted_iota(jnp.int32, (LANE, ns), 0), int(math.log2(S5_GROUP)))
    col = jnp.right_shift(lax.broadcasted_iota(jnp.int32, (LANE, ns), 1), int(math.log2(S5_STATE)))
    same = row == col
    btr = jnp.where(same, btr_ref[...], 0.0)
    bti = jnp.where(same, bti_ref[...], 0.0)
    ctr = jnp.where(same, ctr_ref[...], 0.0)
    cti = jnp.where(same, cti_ref[...], 0.0)
    bbr = cr * btr - ci * bti
    bbi = cr * bti + ci * btr
    pows = [(jnp.ones_like(lbr), jnp.zeros_like(lbr))]
    for _ in range(S5_Q):
        pr, pi = pows[-1]
        pows.append((pr * lbr - pi * lbi, pr * lbi + pi * lbr))
    for s in range(S5_Q):
        pr, pi = pows[S5_Q - 1 - s]
        rows = slice(s * LANE, (s + 1) * LANE)
        wb_ref[rows, :ns] = (pr * bbr - pi * bbi).astype(BF16)
        wb_ref[rows, ns:] = (pr * bbi + pi * bbr).astype(BF16)
    for t in range(S5_Q):
        pr, pi = pows[t + 1]
        rows = slice(t * LANE, (t + 1) * LANE)
        wct_ref[rows, :ns] = (ctr * pr - cti * pi).astype(BF16)
        wct_ref[rows, ns:] = (-(ctr * pi + cti * pr)).astype(BF16)
    ccat = jnp.concatenate([ctr, -cti], axis=1).astype(BF16)
    kall = lax.dot_general(wb_ref[...], ccat, (((1,), (1,)), ((), ())),
                           preferred_element_type=F32)
    w2_ref[LANE:, :LANE] = jnp.zeros((LANE, LANE), BF16)
    for tau in range(S5_Q):
        s = S5_Q - 1 - tau
        k_tau = kall[s * LANE:(s + 1) * LANE, :].astype(BF16)
        w2_ref[:LANE, tau * LANE:(tau + 1) * LANE] = k_tau
        if tau + 1 < S5_Q:
            w2_ref[LANE:, (tau + 1) * LANE:(tau + 2) * LANE] = k_tau


def _s5_rows(lam_re, lam_im, log_step):
    row = lambda a: a.astype(F32).reshape(S5_TILES, 1, S5_TILE_STATE)
    return row(lam_re), row(lam_im), row(jnp.broadcast_to(log_step[:, None], lam_re.shape))


def _s5_prep(rows, b_re, b_im, c_re, c_im):
    ns = S5_TILE_STATE

    def tiled(a):
        a = a.astype(F32).reshape(S5_TILES, LANE, S5_STATE)
        return jnp.tile(a, (1, 1, S5_TILE_GROUPS))

    row_spec = pl.BlockSpec((None, 1, ns), lambda j: (j, 0, 0))
    mat_spec = pl.BlockSpec((None, LANE, ns), lambda j: (j, 0, 0))
    return pl.pallas_call(
        _s5prep_body,
        out_shape=(jax.ShapeDtypeStruct((S5_TILES, S5_Q * LANE, 2 * ns), BF16),
                   jax.ShapeDtypeStruct((S5_TILES, S5_Q * LANE, 2 * ns), BF16),
                   jax.ShapeDtypeStruct((S5_TILES, 2 * LANE, S5_Q * LANE), BF16)),
        grid=(S5_TILES,),
        in_specs=[row_spec, row_spec, row_spec, mat_spec, mat_spec, mat_spec, mat_spec],
        out_specs=(pl.BlockSpec((None, S5_Q * LANE, 2 * ns), lambda j: (j, 0, 0)),
                   pl.BlockSpec((None, S5_Q * LANE, 2 * ns), lambda j: (j, 0, 0)),
                   pl.BlockSpec((None, 2 * LANE, S5_Q * LANE), lambda j: (j, 0, 0))),
        compiler_params=_params("arbitrary"),
        name="s5prep",
    )(*rows, tiled(b_re.swapaxes(1, 2)), tiled(b_im.swapaxes(1, 2)), tiled(c_re), tiled(c_im))


def _s5_body(u_ref, lre_ref, lim_ref, st_ref, wb_ref, wct_ref, w2_ref, d_ref, o_ref, e_scr, f_scr, acc, ynat):
    ns = S5_TILE_STATE
    nc = e_scr.shape[0]
    half = nc // 2
    even_rows = lambda s: pl.ds(s, half, stride=2 * S5_Q)
    odd_rows = lambda s: pl.ds(S5_Q + s, half, stride=2 * S5_Q)
    slabs = [jnp.concatenate([u_ref[even_rows(s), :], u_ref[odd_rows(s), :]], axis=0)
             for s in range(S5_Q)]
    sb = [x.astype(BF16) for x in slabs]

    e_scr[...] = jnp.dot(jnp.concatenate(sb, axis=1), wb_ref[...], preferred_element_type=F32)
    ar, ai = _lbar(lre_ref[...], lim_ref[...], st_ref[...])
    for _ in range(int(math.log2(S5_Q))):
        ar, ai = ar * ar - ai * ai, 2.0 * ar * ai

    ee_r, ee_i = e_scr[:half, :ns], e_scr[:half, ns:]
    f_scr[:, :ns] = ar * ee_r - ai * ee_i + e_scr[half:, :ns]
    f_scr[:, ns:] = ar * ee_i + ai * ee_r + e_scr[half:, ns:]
    a2r, a2i = ar * ar - ai * ai, 2.0 * ar * ai

    n_pairs = S5_Q // 2
    seg = half // n_pairs
    hr = hi = jnp.zeros((1, ns), F32)
    for p in range(n_pairs):
        for c in range(p * seg, (p + 1) * seg):
            er, ei = f_scr[c:c + 1, :ns], f_scr[c:c + 1, ns:]
            f_scr[c:c + 1, :ns] = hr
            f_scr[c:c + 1, ns:] = hi
            hr, hi = a2r * hr - a2i * hi + er, a2r * hi + a2i * hr + ei
        s = 2 * p
        pair = jnp.concatenate([sb[s], sb[s + 1]], axis=1)
        width = (S5_Q - s) * LANE
        toep = jnp.dot(pair, w2_ref[:, :width], preferred_element_type=F32)
        if p == 0:
            acc[...] = toep
        else:
            acc[:, s * LANE:] += toep

    g_r, g_i = f_scr[:, :ns], f_scr[:, ns:]
    ee_r, ee_i = e_scr[:half, :ns], e_scr[:half, ns:]
    e_scr[half:, :ns] = ar * g_r - ai * g_i + ee_r
    e_scr[half:, ns:] = ar * g_i + ai * g_r + ee_i
    e_scr[:half, :ns] = g_r
    e_scr[:half, ns:] = g_i
    acc[...] += lax.dot_general(e_scr[...].astype(BF16), wct_ref[...], (((1,), (1,)), ((), ())),
                                preferred_element_type=F32)
    d = d_ref[...]
    for t in range(S5_Q):
        y = jax.nn.gelu(acc[:, t * LANE:(t + 1) * LANE] + d * slabs[t])
        ynat[even_rows(t), :] = y[:half]
        ynat[odd_rows(t), :] = y[half:]
    o_ref[...] = ynat[...].astype(BF16)


def _s5(u, rows, wb, wct, w2, d_skip):
    seq = u.shape[0]
    nc, ns = seq // S5_Q, S5_TILE_STATE
    row_spec = pl.BlockSpec((None, 1, ns), lambda j: (j, 0, 0))
    return pl.pallas_call(
        _s5_body,
        out_shape=jax.ShapeDtypeStruct((seq, D_MODEL), BF16),
        grid=(S5_TILES,),
        in_specs=[pl.BlockSpec((seq, LANE), lambda j: (0, j)), row_spec, row_spec, row_spec,
                  pl.BlockSpec((None, S5_Q * LANE, 2 * ns), lambda j: (j, 0, 0)),
                  pl.BlockSpec((None, S5_Q * LANE, 2 * ns), lambda j: (j, 0, 0)),
                  pl.BlockSpec((None, 2 * LANE, S5_Q * LANE), lambda j: (j, 0, 0)),
                  pl.BlockSpec((1, LANE), lambda j: (0, j))],
        out_specs=pl.BlockSpec((seq, LANE), lambda j: (0, j)),
        scratch_shapes=[pltpu.VMEM((nc, 2 * ns), F32), pltpu.VMEM((nc // 2, 2 * ns), F32),
                        pltpu.VMEM((nc, S5_Q * LANE), F32), pltpu.VMEM((seq, LANE), F32)],
        compiler_params=_params("arbitrary"),
        name="s5",
    )(u, *rows, wb, wct, w2, d_skip.astype(F32).reshape(1, D_MODEL))


def _glu_body(a_ref, w_hbm, ba_ref, bb_ref, z_ref, o_ref, wf32, wbf, sems, *, n_j):
    tn = o_ref.shape[1]

    @pl.when(pl.program_id(1) == 0)
    def _():
        _fetch_cast_weights(w_hbm, (0, n_j * tn), wf32, wbf, sems, pl.program_id(0), n_j)

    g = jnp.dot(a_ref[...], wbf[...], preferred_element_type=F32)
    ga = g[:, :tn] + ba_ref[...]
    gb = g[:, tn:] + bb_ref[...]
    o_ref[...] = (ga * jax.nn.sigmoid(gb) * z_ref[...].astype(F32)).astype(BF16)


def _glu(ys5, w_glu3, b_glu, zs, tm, tn):
    seq = ys5.shape[0]
    nb = D_MODEL // tn
    return pl.pallas_call(
        functools.partial(_glu_body, n_j=nb),
        out_shape=jax.ShapeDtypeStruct((seq, D_MODEL), BF16),
        grid=(nb, seq // tm),
        in_specs=[pl.BlockSpec((tm, D_MODEL), lambda j, i: (i, 0)),
                  pl.BlockSpec(memory_space=pl.ANY),
                  pl.BlockSpec((1, tn), lambda j, i: (0, j)),
                  pl.BlockSpec((1, tn), lambda j, i: (0, nb + j)),
                  pl.BlockSpec((tm, tn), lambda j, i: (i, Z_S5 // tn + j))],
        out_specs=pl.BlockSpec((tm, tn), lambda j, i: (i, j)),
        scratch_shapes=[pltpu.VMEM((D_MODEL, 2 * tn), F32), pltpu.VMEM((D_MODEL, 2 * tn), BF16),
                        pltpu.SemaphoreType.DMA((2,))],
        compiler_params=_params("arbitrary", "arbitrary"),
        name="glu",
    )(ys5, w_glu3, b_glu, b_glu, zs)


def _split3(x):
    hi = x.astype(BF16)
    r1 = x - hi.astype(F32)
    mid = r1.astype(BF16)
    lo = (r1 - mid.astype(F32)).astype(BF16)
    return hi, mid, lo


def _split2(x):
    hi = x.astype(BF16)
    mid = (x - hi.astype(F32)).astype(BF16)
    return jnp.concatenate([hi, mid], axis=1)


def _dtprep_body(h_ref, w_ref, dtb_ref, alog_ref, tri_ref, arow_ref, acol_ref, ehm_ref, wbf):
    T = M2_CHUNK

    @pl.when(pl.program_id(0) == 0)
    def _():
        wbf[...] = w_ref[...].astype(BF16)

    raw = jnp.dot(h_ref[...].astype(F32), wbf[...].astype(F32), preferred_element_type=F32)
    tri = tri_ref[...]
    neg_a = -jnp.exp(alog_ref[...])
    pad1 = jnp.zeros((T - M2_HPG, T), F32)
    pad2 = jnp.zeros((T - 2 * M2_HPG, T), F32)
    for ci in range(M2_CHUNKS_PER_STEP):
        rows = slice(ci * T, (ci + 1) * T)
        dt = jax.nn.softplus(raw[rows, :].T + dtb_ref[...])
        hi, mid, lo = _split3(dt * neg_a)
        acum = (jnp.dot(hi, tri, preferred_element_type=F32) + jnp.dot(mid, tri, preferred_element_type=F32)
                + jnp.dot(lo, tri, preferred_element_type=F32))
        rem = acum[:, T - 1:T] - acum
        arow_ref[:, rows] = acum - jnp.log(dt)
        dte = dt * jnp.exp(rem)
        eac = jnp.exp(acum)
        for g in range(M2_GROUPS):
            hs = slice(g * M2_HPG, (g + 1) * M2_HPG)
            acol_ref[g, rows, :] = jnp.concatenate([acum[hs], pad1], axis=0).T
            ehm_ref[g, rows, :] = _split2(jnp.concatenate([dte[hs], eac[hs], pad2], axis=0).T)


def _dt_prep(h, w_in3, dt_bias, a_log):
    seq = h.shape[0]
    T, ts = M2_CHUNK, M2_CHUNK * M2_CHUNKS_PER_STEP
    lanes = lambda v: jnp.broadcast_to(v.astype(F32)[:, None], (M2_HEADS, T))
    tri = (jnp.arange(T)[:, None] <= jnp.arange(T)[None, :]).astype(BF16)
    return pl.pallas_call(
        _dtprep_body,
        out_shape=(jax.ShapeDtypeStruct((M2_HEADS, seq), F32),
                   jax.ShapeDtypeStruct((M2_GROUPS, seq, LANE), F32),
                   jax.ShapeDtypeStruct((M2_GROUPS, seq, 2 * LANE), BF16)),
        grid=(seq // ts,),
        in_specs=[pl.BlockSpec((ts, D_MODEL), lambda c: (c, 0)),
                  pl.BlockSpec((None, D_MODEL, M2_HEADS), lambda c: (0, 0, WIN_DT // M2_HEADS)),
                  pl.BlockSpec((M2_HEADS, T), lambda c: (0, 0)),
                  pl.BlockSpec((M2_HEADS, T), lambda c: (0, 0)),
                  pl.BlockSpec((T, T), lambda c: (0, 0))],
        out_specs=(pl.BlockSpec((M2_HEADS, ts), lambda c: (0, c)),
                   pl.BlockSpec((M2_GROUPS, ts, LANE), lambda c: (0, c, 0)),
                   pl.BlockSpec((M2_GROUPS, ts, 2 * LANE), lambda c: (0, c, 0))),
        scratch_shapes=[pltpu.VMEM((D_MODEL, M2_HEADS), BF16)],
        compiler_params=_params("arbitrary"),
        name="dtprep",
    )(h, w_in3, lanes(dt_bias), lanes(a_log), tri)


def _ssd_body(x_ref, b_ref, c_ref, z_ref, arow_ref, acol_ref, ehm_ref, exp_ref, dskip_ref, ng_ref, o_ref, state):
    @pl.when(pl.program_id(1) == 0)
    def _():
        state[...] = jnp.zeros_like(state)

    for ci in range(M2_CHUNKS_PER_STEP):
        rows = slice(ci * M2_CHUNK, (ci + 1) * M2_CHUNK)
        o_ref[rows, :] = _ssd_chunk(x_ref[rows, :], b_ref[rows, :], c_ref[rows, :], z_ref[rows, :],
                                    arow_ref[:, rows], acol_ref[rows, :], ehm_ref[rows, :],
                                    exp_ref, dskip_ref, ng_ref, state)


def _ssd_chunk(xb, bm_b, cm_b, zb, arow, acol, ehm, exp_ref, dskip_ref, ng_ref, state):
    T = M2_CHUNK
    xs = xb.astype(F32)
    dte_e = jnp.dot(ehm, exp_ref[0], preferred_element_type=F32)
    eac_e = jnp.dot(ehm, exp_ref[1], preferred_element_type=F32)

    scores = lax.dot_general(cm_b, bm_b, (((1,), (1,)), ((), ())), preferred_element_type=F32)
    y_off = jnp.dot(cm_b, state[...].astype(BF16), preferred_element_type=F32) * eac_e

    causal = (lax.broadcasted_iota(jnp.int32, (T, T), 0) >= lax.broadcasted_iota(jnp.int32, (T, T), 1))
    left = lax.broadcasted_iota(jnp.int32, (T, LANE), 1) < M2_HEADDIM
    zero = jnp.zeros((T, LANE), BF16)
    ys = []
    for p in range(M2_HPG // 2):
        lhs = []
        for h in (2 * p, 2 * p + 1):
            seg = jnp.broadcast_to(acol[:, h:h + 1], (T, T)) - jnp.broadcast_to(arow[h:h + 1, :], (T, T))
            lhs.append((scores * jnp.exp(jnp.where(causal, seg, -1e30))).astype(BF16))
        xp = xb[:, p * LANE:(p + 1) * LANE]
        rhs = jnp.concatenate([jnp.where(left, xp, zero), jnp.where(left, zero, xp)], axis=0)
        ys.append(jnp.dot(jnp.concatenate(lhs, axis=1), rhs, preferred_element_type=F32))
    y = jnp.concatenate(ys, axis=1) + y_off + dskip_ref[...] * xs

    xw = (xs * dte_e).astype(BF16)
    state[...] = (state[...] * eac_e[T - 1:T, :]
                  + jnp.dot(bm_b.astype(F32).T.astype(BF16), xw, preferred_element_type=F32))

    y = y * zb.astype(F32)
    return (_rms(y) * ng_ref[...]).astype(BF16)


def _ssd(xbc, zs, arow, acol, ehm, d_skip, norm_g):
    seq = xbc.shape[0]
    T, gw, n = M2_CHUNK * M2_CHUNKS_PER_STEP, M2_GW, M2_STATE
    bb = M2_D_INNER // n
    cb = bb + M2_GROUPS
    col = jnp.arange(2 * LANE)[:, None] % LANE
    head = jnp.arange(gw)[None, :] // M2_HEADDIM
    expand = jnp.stack([col == head, col == head + M2_HPG]).astype(BF16)
    return pl.pallas_call(
        _ssd_body,
        out_shape=jax.ShapeDtypeStruct((seq, M2_D_INNER), BF16),
        grid=(M2_GROUPS, seq // T),
        in_specs=[pl.BlockSpec((T, gw), lambda g, c: (c, g)),
                  pl.BlockSpec((T, n), lambda g, c: (c, bb + g)),
                  pl.BlockSpec((T, n), lambda g, c: (c, cb + g)),
                  pl.BlockSpec((T, gw), lambda g, c: (c, Z_M2 // gw + g)),
                  pl.BlockSpec((M2_HPG, T), lambda g, c: (g, c)),
                  pl.BlockSpec((None, T, LANE), lambda g, c: (g, c, 0)),
                  pl.BlockSpec((None, T, 2 * LANE), lambda g, c: (g, c, 0)),
                  pl.BlockSpec((2, 2 * LANE, gw), lambda g, c: (0, 0, 0)),
                  pl.BlockSpec((1, gw), lambda g, c: (0, g)),
                  pl.BlockSpec((1, gw), lambda g, c: (0, g))],
        out_specs=pl.BlockSpec((T, gw), lambda g, c: (c, g)),
        scratch_shapes=[pltpu.VMEM((n, gw), F32)],
        compiler_params=_params("arbitrary", "arbitrary"),
        name="ssd",
    )(xbc, xbc, xbc, zs, arow, acol, ehm, expand,
      jnp.repeat(d_skip.astype(F32), M2_HEADDIM).reshape(1, M2_D_INNER),
      norm_g.astype(F32).reshape(1, M2_D_INNER))


def _merge_body(a1_ref, w1_ref, a2_ref, w2_ref, g1_ref, g2_ref, o_ref):
    m1 = jnp.dot(a1_ref[...], w1_ref[...], preferred_element_type=F32)
    m2 = jnp.dot(a2_ref[...], w2_ref[...], preferred_element_type=F32)
    o_ref[...] = (g1_ref[...].astype(F32) * m1 + g2_ref[...].astype(F32) * m2).astype(BF16)


def _merge(y1, w1, y2, w2, gates, tm, tn):
    seq = y1.shape[0]
    nb = D_MODEL // tn
    return pl.pallas_call(
        _merge_body,
        out_shape=jax.ShapeDtypeStruct((seq, D_MODEL), BF16),
        grid=(seq // tm, nb),
        in_specs=[pl.BlockSpec((tm, D_MODEL), lambda i, j: (i, 0)),
                  pl.BlockSpec((D_MODEL, tn), lambda i, j: (0, j)),
                  pl.BlockSpec((tm, M2_D_INNER), lambda i, j: (i, 0)),
                  pl.BlockSpec((M2_D_INNER, tn), lambda i, j: (0, j)),
                  pl.BlockSpec((tm, tn), lambda i, j: (i, j)),
                  pl.BlockSpec((tm, tn), lambda i, j: (i, nb + j))],
        out_specs=pl.BlockSpec((tm, tn), lambda i, j: (i, j)),
        compiler_params=_params("arbitrary", "arbitrary"),
        name="merge",
    )(y1, w1, y2, w2, gates, gates)


def _out_body(a_ref, w_hbm, x_ref, gate_ref, o_ref, wf32, wbf, sems, *, n_j):
    @pl.when(pl.program_id(1) == 0)
    def _():
        _fetch_cast_weights(w_hbm, (0,), wf32, wbf, sems, pl.program_id(0), n_j)

    m = jnp.dot(a_ref[...], wbf[...], preferred_element_type=F32)
    o_ref[...] = x_ref[...] + gate_ref[...] * m


def _out_proj(merged, w_out3, x2, mod, tm, tn):
    seq = merged.shape[0]
    nb = D_MODEL // tn
    return pl.pallas_call(
        functools.partial(_out_body, n_j=nb),
        out_shape=jax.ShapeDtypeStruct((seq, D_MODEL), F32),
        grid=(nb, seq // tm),
        in_specs=[pl.BlockSpec((tm, D_MODEL), lambda j, i: (i, 0)),
                  pl.BlockSpec(memory_space=pl.ANY),
                  pl.BlockSpec((tm, tn), lambda j, i: (i, j)),
                  pl.BlockSpec((1, tn), lambda j, i: (0, 2 * nb + j))],
        out_specs=pl.BlockSpec((tm, tn), lambda j, i: (i, j)),
        scratch_shapes=[pltpu.VMEM((D_MODEL, tn), F32), pltpu.VMEM((D_MODEL, tn), BF16),
                        pltpu.SemaphoreType.DMA((1,))],
        compiler_params=_params("arbitrary", "arbitrary"),
        name="outproj",
    )(merged, w_out3, x2, mod)


def _fnorm_body(x_ref, g_ref, o_ref):
    o_ref[...] = _rms(x_ref[...]) * g_ref[...]


def _final_norm(x2, g, tm):
    seq = x2.shape[0]
    return pl.pallas_call(
        _fnorm_body,
        out_shape=jax.ShapeDtypeStruct((seq, D_MODEL), F32),
        grid=(seq // tm,),
        in_specs=[pl.BlockSpec((tm, D_MODEL), lambda i: (i, 0)),
                  pl.BlockSpec((1, D_MODEL), lambda i: (0, 0))],
        out_specs=pl.BlockSpec((tm, D_MODEL), lambda i: (i, 0)),
        compiler_params=_params("arbitrary"),
        name="fnorm",
    )(x2, g)


def _tiles(seq):
    return dict(tm=min(1024, seq), tm_merge=min(512, seq), tm_norm=min(256, seq), tn=512, tn_in=1024, tn_glu=256)


def kernel(x, c, w_ada, b_ada, norm_g, w_in, s5_lambda_re, s5_lambda_im, s5_log_step, s5_b_re, s5_b_im,
           s5_c_re, s5_c_im, s5_d, s5_w_glu, s5_b_glu, m2_conv_w, m2_conv_b, m2_dt_bias, m2_a_log, m2_d,
           m2_norm_g, w_br_s5, w_br_m2, w_out, final_g):
    bsz, seq, _ = x.shape
    assert bsz == 1 and w_ada.shape[0] == 1 and seq % (S5_Q * SUBLANE) == 0 and seq % (M2_CHUNK * M2_CHUNKS_PER_STEP) == 0
    tl = _tiles(seq)
    tm, tn = tl["tm"], tl["tn"]
    x2 = x.reshape(seq, D_MODEL)

    w1, w2 = w_br_s5[0].astype(BF16), w_br_m2[0].astype(BF16)
    conv = (m2_conv_w[0].astype(F32), m2_conv_b[0].astype(F32).reshape(1, M2_CONV_DIM))

    mod = _ada_mod(c, w_ada[0], b_ada[0])
    h = _norm_mod(x2, norm_g[0].reshape(1, D_MODEL), mod, tl["tm_norm"])
    tn_in = tl["tn_in"]
    u = _proj(h, w_in, 0, D_MODEL, F32, tm, tn_in, "inproj_u")
    zs = _proj(h, w_in, WIN_S5Z, WIN_XBC - WIN_S5Z, BF16, tm, tn_in, "inproj_z", act=jax.nn.silu)
    xbc = _proj(h, w_in, WIN_XBC, M2_CONV_DIM, BF16, tm, tn_in, "inproj_x", conv=conv)
    gates = _proj(h, w_in, WIN_GATES, 2 * D_MODEL, BF16, tm, tn_in, "inproj_g", act=jax.nn.sigmoid)

    rows = _s5_rows(s5_lambda_re[0], s5_lambda_im[0], s5_log_step[0])
    wb, wct, w2t = _s5_prep(rows, s5_b_re[0], s5_b_im[0], s5_c_re[0], s5_c_im[0])
    ys5 = _s5(u, rows, wb, wct, w2t, s5_d[0])
    y1 = _glu(ys5, s5_w_glu, s5_b_glu[0].astype(F32).reshape(1, -1), zs, tm, tl["tn_glu"])

    arow, acol, ehm = _dt_prep(h, w_in, m2_dt_bias[0], m2_a_log[0])
    y2 = _ssd(xbc, zs, arow, acol, ehm, m2_d[0], m2_norm_g[0])

    merged = _merge(y1, w1, y2, w2, gates, tl["tm_merge"], tn)
    xn = _out_proj(merged, w_out, x2, mod, tm, tn)
    out = _final_norm(xn, final_g.astype(F32).reshape(1, D_MODEL), tl["tm_norm"])
    return out.reshape(bsz, seq, D_MODEL)
```

```python
import functools
import math

import jax
import jax.numpy as jnp
from jax import lax
from jax.experimental import pallas as pl
from jax.experimental.pallas import tpu as pltpu

F32 = jnp.float32
BF16 = jnp.bfloat16

LANE = 128
SUBLANE = 8
VMEM_LIMIT_BYTES = 60 * 1024 * 1024

D_MODEL = 4096
EPS = 1e-6

S5_GROUP = 16
S5_STATE = 64
S5_Q = 8
S5_TILE_GROUPS = LANE // S5_GROUP
S5_TILE_STATE = S5_TILE_GROUPS * S5_STATE
S5_TILES = D_MODEL // LANE

M2_D_INNER = 2 * D_MODEL
M2_HEADDIM = 64
M2_HEADS = M2_D_INNER // M2_HEADDIM
M2_GROUPS = 8
M2_HPG = M2_HEADS // M2_GROUPS
M2_STATE = 128
M2_CONV = 4
M2_CHUNK = 128
M2_CHUNKS_PER_STEP = 8
M2_GW = M2_HPG * M2_HEADDIM
M2_GN = M2_GROUPS * M2_STATE
M2_CONV_DIM = M2_D_INNER + 2 * M2_GN

WIN_S5Z = D_MODEL
WIN_XBC = 2 * D_MODEL + M2_D_INNER
WIN_DT = WIN_XBC + M2_CONV_DIM
WIN_GATES = WIN_DT + M2_HEADS
Z_S5 = 0
Z_M2 = D_MODEL


def _params(*sem):
    return pltpu.CompilerParams(dimension_semantics=sem, vmem_limit_bytes=VMEM_LIMIT_BYTES)


MOD_TN = 512


def _mod_body(cb_ref, w_ref, b_ref, o_ref):
    cb = cb_ref[...]
    for q in range(MOD_TN // LANE):
        sl = slice(q * LANE, (q + 1) * LANE)
        o_ref[:, sl] = jnp.sum(w_ref[:, sl] * cb, axis=0, keepdims=True) + b_ref[:, sl]


def _ada_mod(c, w_ada, b_ada):
    n = w_ada.shape[1]
    cb = jnp.broadcast_to(c.reshape(D_MODEL, 1), (D_MODEL, LANE))
    return pl.pallas_call(
        _mod_body,
        out_shape=jax.ShapeDtypeStruct((1, n), F32),
        grid=(n // MOD_TN,),
        in_specs=[pl.BlockSpec((D_MODEL, LANE), lambda j: (0, 0)),
                  pl.BlockSpec((D_MODEL, MOD_TN), lambda j: (0, j)),
                  pl.BlockSpec((1, MOD_TN), lambda j: (0, j))],
        out_specs=pl.BlockSpec((1, MOD_TN), lambda j: (0, j)),
        compiler_params=_params("arbitrary"),
        name="mod",
    )(cb, w_ada, b_ada.reshape(1, n))


def _rms(x):
    return x * lax.rsqrt(jnp.mean(x * x, axis=-1, keepdims=True) + EPS)


def _norm_body(x_ref, g_ref, shift_ref, scale_ref, o_ref):
    h = _rms(x_ref[...]) * g_ref[...] * (1.0 + scale_ref[...]) + shift_ref[...]
    o_ref[...] = h.astype(BF16)


def _norm_mod(x2, g, mod, tm):
    seq = x2.shape[0]
    return pl.pallas_call(
        _norm_body,
        out_shape=jax.ShapeDtypeStruct((seq, D_MODEL), BF16),
        grid=(seq // tm,),
        in_specs=[pl.BlockSpec((tm, D_MODEL), lambda i: (i, 0)),
                  pl.BlockSpec((1, D_MODEL), lambda i: (0, 0)),
                  pl.BlockSpec((1, D_MODEL), lambda i: (0, 0)),
                  pl.BlockSpec((1, D_MODEL), lambda i: (0, 1))],
        out_specs=pl.BlockSpec((tm, D_MODEL), lambda i: (i, 0)),
        compiler_params=_params("arbitrary"),
        name="norm",
    )(x2, g, mod, mod)


CAST_ROWS = 512


def _fetch_cast_weights(w_hbm, col_starts, wf32, wbf, sems, j, n_j):
    tn = wf32.shape[1] // len(col_starts)

    def copies(jj):
        return [pltpu.make_async_copy(w_hbm.at[0, :, pl.ds(pl.multiple_of(c0 + jj * tn, LANE), tn)],
                                      wf32.at[:, pl.ds(q * tn, tn)], sems.at[q])
                for q, c0 in enumerate(col_starts)]

    @pl.when(j == 0)
    def _():
        for cp in copies(0):
            cp.start()

    for cp in copies(j):
        cp.wait()
    for r0 in range(0, wf32.shape[0], CAST_ROWS):
        wbf[r0:r0 + CAST_ROWS, :] = wf32[r0:r0 + CAST_ROWS, :].astype(BF16)

    @pl.when(j + 1 < n_j)
    def _():
        for cp in copies(j + 1):
            cp.start()


def _proj_body(*refs, col0, n_j, act, conv):
    if conv:
        a_ref, w_hbm, cw_ref, cb_ref, o_ref, wf32, wbf, sems, cbuf = refs
    else:
        a_ref, w_hbm, o_ref, wf32, wbf, sems = refs
    tm, tn = o_ref.shape

    @pl.when(pl.program_id(1) == 0)
    def _():
        _fetch_cast_weights(w_hbm, (col0,), wf32, wbf, sems, pl.program_id(0), n_j)
        if conv:
            cbuf[:SUBLANE, :] = jnp.zeros((SUBLANE, tn), F32)

    r = jnp.dot(a_ref[...], wbf[...], preferred_element_type=F32)
    if conv:
        k = cw_ref.shape[0]
        cbuf[SUBLANE:, :] = r
        y = cb_ref[...] + cw_ref[k - 1:k, :] * r
        for d in range(1, k):
            y = y + cw_ref[k - 1 - d:k - d, :] * cbuf[pl.ds(SUBLANE - d, tm), :]
        o_ref[...] = jax.nn.silu(y).astype(o_ref.dtype)
        cbuf[:SUBLANE, :] = r[tm - SUBLANE:]
    else:
        o_ref[...] = (act(r) if act else r).astype(o_ref.dtype)


def _proj(h, w3, col0, n, out_dtype, tm, tn, name, act=None, conv=None):
    seq, k = h.shape
    assert col0 % LANE == 0 and n % tn == 0
    extra, extra_specs = [], []
    scratch = [pltpu.VMEM((k, tn), F32), pltpu.VMEM((k, tn), BF16), pltpu.SemaphoreType.DMA((1,))]
    if conv:
        extra = list(conv)
        extra_specs = [pl.BlockSpec((conv[0].shape[0], tn), lambda j, i: (0, j)),
                       pl.BlockSpec((1, tn), lambda j, i: (0, j))]
        scratch.append(pltpu.VMEM((tm + SUBLANE, tn), F32))
    return pl.pallas_call(
        functools.partial(_proj_body, col0=col0, n_j=n // tn, act=act, conv=bool(conv)),
        out_shape=jax.ShapeDtypeStruct((seq, n), out_dtype),
        grid=(n // tn, seq // tm),
        in_specs=[pl.BlockSpec((tm, k), lambda j, i: (i, 0)), pl.BlockSpec(memory_space=pl.ANY)] + extra_specs,
        out_specs=pl.BlockSpec((tm, tn), lambda j, i: (i, j)),
        scratch_shapes=scratch,
        compiler_params=_params("arbitrary", "arbitrary"),
        name=name,
    )(h, w3, *extra)


def _lbar(lre, lim, log_step):
    step = jnp.exp(log_step)
    mag = jnp.exp(lre * step)
    return mag * jnp.cos(lim * step), mag * jnp.sin(lim * step)


def _s5prep_body(lre_ref, lim_ref, st_ref, btr_ref, bti_ref, ctr_ref, cti_ref,
                 wb_ref, wct_ref, w2_ref):
    ns = S5_TILE_STATE
    lre, lim = lre_ref[...], lim_ref[...]
    lbr, lbi = _lbar(lre, lim, st_ref[...])
    den = lre * lre + lim * lim
    nr, ni = lbr - 1.0, lbi
    cr = (nr * lre + ni * lim) / den
    ci = (ni * lre - nr * lim) / den
    row = jnp.right_shift(lax.broadcasted_iota(jnp.int32, (LANE, ns), 0), int(math.log2(S5_GROUP)))
    col = jnp.right_shift(lax.broadcasted_iota(jnp.int32, (LANE, ns), 1), int(math.log2(S5_STATE)))
    same = row == col
    btr = jnp.where(same, btr_ref[...], 0.0)
    bti = jnp.where(same, bti_ref[...], 0.0)
    ctr = jnp.where(same, ctr_ref[...], 0.0)
    cti = jnp.where(same, cti_ref[...], 0.0)
    bbr = cr * btr - ci * bti
    bbi = cr * bti + ci * btr
    pows = [(jnp.ones_like(lbr), jnp.zeros_like(lbr))]
    for _ in range(S5_Q):
        pr, pi = pows[-1]
        pows.append((pr * lbr - pi * lbi, pr * lbi + pi * lbr))
    for s in range(S5_Q):
        pr, pi = pows[S5_Q - 1 - s]
        rows = slice(s * LANE, (s + 1) * LANE)
        wb_ref[rows, :ns] = (pr * bbr - pi * bbi).astype(BF16)
        wb_ref[rows, ns:] = (pr * bbi + pi * bbr).astype(BF16)
    for t in range(S5_Q):
        pr, pi = pows[t + 1]
        rows = slice(t * LANE, (t + 1) * LANE)
        wct_ref[rows, :ns] = (ctr * pr - cti * pi).astype(BF16)
        wct_ref[rows, ns:] = (-(ctr * pi + cti * pr)).astype(BF16)
    ccat = jnp.concatenate([ctr, -cti], axis=1).astype(BF16)
    kall = lax.dot_general(wb_ref[...], ccat, (((1,), (1,)), ((), ())),
                           preferred_element_type=F32)
    w2_ref[LANE:, :LANE] = jnp.zeros((LANE, LANE), BF16)
    for tau in range(S5_Q):
        s = S5_Q - 1 - tau
        k_tau = kall[s * LANE:(s + 1) * LANE, :].astype(BF16)
        w2_ref[:LANE, tau * LANE:(tau + 1) * LANE] = k_tau
        if tau + 1 < S5_Q:
            w2_ref[LANE:, (tau + 1) * LANE:(tau + 2) * LANE] = k_tau


def _s5_rows(lam_re, lam_im, log_step):
    row = lambda a: a.astype(F32).reshape(S5_TILES, 1, S5_TILE_STATE)
    return row(lam_re), row(lam_im), row(jnp.broadcast_to(log_step[:, None], lam_re.shape))


def _s5_prep(rows, b_re, b_im, c_re, c_im):
    ns = S5_TILE_STATE

    def tiled(a):
        a = a.astype(F32).reshape(S5_TILES, LANE, S5_STATE)
        return jnp.tile(a, (1, 1, S5_TILE_GROUPS))

    row_spec = pl.BlockSpec((None, 1, ns), lambda j: (j, 0, 0))
    mat_spec = pl.BlockSpec((None, LANE, ns), lambda j: (j, 0, 0))
    return pl.pallas_call(
        _s5prep_body,
        out_shape=(jax.ShapeDtypeStruct((S5_TILES, S5_Q * LANE, 2 * ns), BF16),
                   jax.ShapeDtypeStruct((S5_TILES, S5_Q * LANE, 2 * ns), BF16),
                   jax.ShapeDtypeStruct((S5_TILES, 2 * LANE, S5_Q * LANE), BF16)),
        grid=(S5_TILES,),
        in_specs=[row_spec, row_spec, row_spec, mat_spec, mat_spec, mat_spec, mat_spec],
        out_specs=(pl.BlockSpec((None, S5_Q * LANE, 2 * ns), lambda j: (j, 0, 0)),
                   pl.BlockSpec((None, S5_Q * LANE, 2 * ns), lambda j: (j, 0, 0)),
                   pl.BlockSpec((None, 2 * LANE, S5_Q * LANE), lambda j: (j, 0, 0))),
        compiler_params=_params("arbitrary"),
        name="s5prep",
    )(*rows, tiled(b_re.swapaxes(1, 2)), tiled(b_im.swapaxes(1, 2)), tiled(c_re), tiled(c_im))


def _s5_body(u_ref, lre_ref, lim_ref, st_ref, wb_ref, wct_ref, w2_ref, d_ref, o_ref, e_scr, f_scr, acc, ynat):
    ns = S5_TILE_STATE
    nc = e_scr.shape[0]
    half = nc // 2
    even_rows = lambda s: pl.ds(s, half, stride=2 * S5_Q)
    odd_rows = lambda s: pl.ds(S5_Q + s, half, stride=2 * S5_Q)
    slabs = [jnp.concatenate([u_ref[even_rows(s), :], u_ref[odd_rows(s), :]], axis=0)
             for s in range(S5_Q)]
    sb = [x.astype(BF16) for x in slabs]

    e_scr[...] = jnp.dot(jnp.concatenate(sb, axis=1), wb_ref[...], preferred_element_type=F32)
    ar, ai = _lbar(lre_ref[...], lim_ref[...], st_ref[...])
    for _ in range(int(math.log2(S5_Q))):
        ar, ai = ar * ar - ai * ai, 2.0 * ar * ai

    ee_r, ee_i = e_scr[:half, :ns], e_scr[:half, ns:]
    f_scr[:, :ns] = ar * ee_r - ai * ee_i + e_scr[half:, :ns]
    f_scr[:, ns:] = ar * ee_i + ai * ee_r + e_scr[half:, ns:]
    a2r, a2i = ar * ar - ai * ai, 2.0 * ar * ai

    n_pairs = S5_Q // 2
    seg = half // n_pairs
    hr = hi = jnp.zeros((1, ns), F32)
    for p in range(n_pairs):
        for c in range(p * seg, (p + 1) * seg):
            er, ei = f_scr[c:c + 1, :ns], f_scr[c:c + 1, ns:]
            f_scr[c:c + 1, :ns] = hr
            f_scr[c:c + 1, ns:] = hi
            hr, hi = a2r * hr - a2i * hi + er, a2r * hi + a2i * hr + ei
        s = 2 * p
        pair = jnp.concatenate([sb[s], sb[s + 1]], axis=1)
        width = (S5_Q - s) * LANE
        toep = jnp.dot(pair, w2_ref[:, :width], preferred_element_type=F32)
        if p == 0:
            acc[...] = toep
        else:
            acc[:, s * LANE:] += toep

    g_r, g_i = f_scr[:, :ns], f_scr[:, ns:]
    ee_r, ee_i = e_scr[:half, :ns], e_scr[:half, ns:]
    e_scr[half:, :ns] = ar * g_r - ai * g_i + ee_r
    e_scr[half:, ns:] = ar * g_i + ai * g_r + ee_i
    e_scr[:half, :ns] = g_r
    e_scr[:half, ns:] = g_i
    acc[...] += lax.dot_general(e_scr[...].astype(BF16), wct_ref[...], (((1,), (1,)), ((), ())),
                                preferred_element_type=F32)
    d = d_ref[...]
    for t in range(S5_Q):
        y = jax.nn.gelu(acc[:, t * LANE:(t + 1) * LANE] + d * slabs[t])
        ynat[even_rows(t), :] = y[:half]
        ynat[odd_rows(t), :] = y[half:]
    o_ref[...] = ynat[...].astype(BF16)


def _s5(u, rows, wb, wct, w2, d_skip):
    seq = u.shape[0]
    nc, ns = seq // S5_Q, S5_TILE_STATE
    row_spec = pl.BlockSpec((None, 1, ns), lambda j: (j, 0, 0))
    return pl.pallas_call(
        _s5_body,
        out_shape=jax.ShapeDtypeStruct((seq, D_MODEL), BF16),
        grid=(S5_TILES,),
        in_specs=[pl.BlockSpec((seq, LANE), lambda j: (0, j)), row_spec, row_spec, row_spec,
                  pl.BlockSpec((None, S5_Q * LANE, 2 * ns), lambda j: (j, 0, 0)),
                  pl.BlockSpec((None, S5_Q * LANE, 2 * ns), lambda j: (j, 0, 0)),
                  pl.BlockSpec((None, 2 * LANE, S5_Q * LANE), lambda j: (j, 0, 0)),
                  pl.BlockSpec((1, LANE), lambda j: (0, j))],
        out_specs=pl.BlockSpec((seq, LANE), lambda j: (0, j)),
        scratch_shapes=[pltpu.VMEM((nc, 2 * ns), F32), pltpu.VMEM((nc // 2, 2 * ns), F32),
                        pltpu.VMEM((nc, S5_Q * LANE), F32), pltpu.VMEM((seq, LANE), F32)],
        compiler_params=_params("arbitrary"),
        name="s5",
    )(u, *rows, wb, wct, w2, d_skip.astype(F32).reshape(1, D_MODEL))


def _glu_body(a_ref, w_hbm, ba_ref, bb_ref, z_ref, o_ref, wf32, wbf, sems, *, n_j):
    tn = o_ref.shape[1]

    @pl.when(pl.program_id(1) == 0)
    def _():
        _fetch_cast_weights(w_hbm, (0, n_j * tn), wf32, wbf, sems, pl.program_id(0), n_j)

    g = jnp.dot(a_ref[...], wbf[...], preferred_element_type=F32)
    ga = g[:, :tn] + ba_ref[...]
    gb = g[:, tn:] + bb_ref[...]
    o_ref[...] = (ga * jax.nn.sigmoid(gb) * z_ref[...].astype(F32)).astype(BF16)


def _glu(ys5, w_glu3, b_glu, zs, tm, tn):
    seq = ys5.shape[0]
    nb = D_MODEL // tn
    return pl.pallas_call(
        functools.partial(_glu_body, n_j=nb),
        out_shape=jax.ShapeDtypeStruct((seq, D_MODEL), BF16),
        grid=(nb, seq // tm),
        in_specs=[pl.BlockSpec((tm, D_MODEL), lambda j, i: (i, 0)),
                  pl.BlockSpec(memory_space=pl.ANY),
                  pl.BlockSpec((1, tn), lambda j, i: (0, j)),
                  pl.BlockSpec((1, tn), lambda j, i: (0, nb + j)),
                  pl.BlockSpec((tm, tn), lambda j, i: (i, Z_S5 // tn + j))],
        out_specs=pl.BlockSpec((tm, tn), lambda j, i: (i, j)),
        scratch_shapes=[pltpu.VMEM((D_MODEL, 2 * tn), F32), pltpu.VMEM((D_MODEL, 2 * tn), BF16),
                        pltpu.SemaphoreType.DMA((2,))],
        compiler_params=_params("arbitrary", "arbitrary"),
        name="glu",
    )(ys5, w_glu3, b_glu, b_glu, zs)


def _split3(x):
    hi = x.astype(BF16)
    r1 = x - hi.astype(F32)
    mid = r1.astype(BF16)
    lo = (r1 - mid.astype(F32)).astype(BF16)
    return hi, mid, lo


def _split2(x):
    hi = x.astype(BF16)
    mid = (x - hi.astype(F32)).astype(BF16)
    return jnp.concatenate([hi, mid], axis=1)


def _dtprep_body(h_ref, w_ref, dtb_ref, alog_ref, tri_ref, arow_ref, acol_ref, ehm_ref, wbf):
    T = M2_CHUNK

    @pl.when(pl.program_id(0) == 0)
    def _():
        wbf[...] = w_ref[...].astype(BF16)

    raw = jnp.dot(h_ref[...].astype(F32), wbf[...].astype(F32), preferred_element_type=F32)
    tri = tri_ref[...]
    neg_a = -jnp.exp(alog_ref[...])
    pad1 = jnp.zeros((T - M2_HPG, T), F32)
    pad2 = jnp.zeros((T - 2 * M2_HPG, T), F32)
    for ci in range(M2_CHUNKS_PER_STEP):
        rows = slice(ci * T, (ci + 1) * T)
        dt = jax.nn.softplus(raw[rows, :].T + dtb_ref[...])
        hi, mid, lo = _split3(dt * neg_a)
        acum = (jnp.dot(hi, tri, preferred_element_type=F32) + jnp.dot(mid, tri, preferred_element_type=F32)
                + jnp.dot(lo, tri, preferred_element_type=F32))
        rem = acum[:, T - 1:T] - acum
        arow_ref[:, rows] = acum - jnp.log(dt)
        dte = dt * jnp.exp(rem)
        eac = jnp.exp(acum)
        for g in range(M2_GROUPS):
            hs = slice(g * M2_HPG, (g + 1) * M2_HPG)
            acol_ref[g, rows, :] = jnp.concatenate([acum[hs], pad1], axis=0).T
            ehm_ref[g, rows, :] = _split2(jnp.concatenate([dte[hs], eac[hs], pad2], axis=0).T)


def _dt_prep(h, w_in3, dt_bias, a_log):
    seq = h.shape[0]
    T, ts = M2_CHUNK, M2_CHUNK * M2_CHUNKS_PER_STEP
    lanes = lambda v: jnp.broadcast_to(v.astype(F32)[:, None], (M2_HEADS, T))
    tri = (jnp.arange(T)[:, None] <= jnp.arange(T)[None, :]).astype(BF16)
    return pl.pallas_call(
        _dtprep_body,
        out_shape=(jax.ShapeDtypeStruct((M2_HEADS, seq), F32),
                   jax.ShapeDtypeStruct((M2_GROUPS, seq, LANE), F32),
                   jax.ShapeDtypeStruct((M2_GROUPS, seq, 2 * LANE), BF16)),
        grid=(seq // ts,),
        in_specs=[pl.BlockSpec((ts, D_MODEL), lambda c: (c, 0)),
                  pl.BlockSpec((None, D_MODEL, M2_HEADS), lambda c: (0, 0, WIN_DT // M2_HEADS)),
                  pl.BlockSpec((M2_HEADS, T), lambda c: (0, 0)),
                  pl.BlockSpec((M2_HEADS, T), lambda c: (0, 0)),
                  pl.BlockSpec((T, T), lambda c: (0, 0))],
        out_specs=(pl.BlockSpec((M2_HEADS, ts), lambda c: (0, c)),
                   pl.BlockSpec((M2_GROUPS, ts, LANE), lambda c: (0, c, 0)),
                   pl.BlockSpec((M2_GROUPS, ts, 2 * LANE), lambda c: (0, c, 0))),
        scratch_shapes=[pltpu.VMEM((D_MODEL, M2_HEADS), BF16)],
        compiler_params=_params("arbitrary"),
        name="dtprep",
    )(h, w_in3, lanes(dt_bias), lanes(a_log), tri)


def _ssd_body(x_ref, b_ref, c_ref, z_ref, arow_ref, acol_ref, ehm_ref, exp_ref, dskip_ref, ng_ref, o_ref, state):
    @pl.when(pl.program_id(1) == 0)
    def _():
        state[...] = jnp.zeros_like(state)

    for ci in range(M2_CHUNKS_PER_STEP):
        rows = slice(ci * M2_CHUNK, (ci + 1) * M2_CHUNK)
        o_ref[rows, :] = _ssd_chunk(x_ref[rows, :], b_ref[rows, :], c_ref[rows, :], z_ref[rows, :],
                                    arow_ref[:, rows], acol_ref[rows, :], ehm_ref[rows, :],
                                    exp_ref, dskip_ref, ng_ref, state)


def _ssd_chunk(xb, bm_b, cm_b, zb, arow, acol, ehm, exp_ref, dskip_ref, ng_ref, state):
    T = M2_CHUNK
    xs = xb.astype(F32)
    dte_e = jnp.dot(ehm, exp_ref[0], preferred_element_type=F32)
    eac_e = jnp.dot(ehm, exp_ref[1], preferred_element_type=F32)

    scores = lax.dot_general(cm_b, bm_b, (((1,), (1,)), ((), ())), preferred_element_type=F32)
    y_off = jnp.dot(cm_b, state[...].astype(BF16), preferred_element_type=F32) * eac_e

    causal = (lax.broadcasted_iota(jnp.int32, (T, T), 0) >= lax.broadcasted_iota(jnp.int32, (T, T), 1))
    left = lax.broadcasted_iota(jnp.int32, (T, LANE), 1) < M2_HEADDIM
    zero = jnp.zeros((T, LANE), BF16)
    ys = []
    for p in range(M2_HPG // 2):
        lhs = []
        for h in (2 * p, 2 * p + 1):
            seg = jnp.broadcast_to(acol[:, h:h + 1], (T, T)) - jnp.broadcast_to(arow[h:h + 1, :], (T, T))
            lhs.append((scores * jnp.exp(jnp.where(causal, seg, -1e30))).astype(BF16))
        xp = xb[:, p * LANE:(p + 1) * LANE]
        rhs = jnp.concatenate([jnp.where(left, xp, zero), jnp.where(left, zero, xp)], axis=0)
        ys.append(jnp.dot(jnp.concatenate(lhs, axis=1), rhs, preferred_element_type=F32))
    y = jnp.concatenate(ys, axis=1) + y_off + dskip_ref[...] * xs

    xw = (xs * dte_e).astype(BF16)
    state[...] = (state[...] * eac_e[T - 1:T, :]
                  + jnp.dot(bm_b.astype(F32).T.astype(BF16), xw, preferred_element_type=F32))

    y = y * zb.astype(F32)
    return (_rms(y) * ng_ref[...]).astype(BF16)


def _ssd(xbc, zs, arow, acol, ehm, d_skip, norm_g):
    seq = xbc.shape[0]
    T, gw, n = M2_CHUNK * M2_CHUNKS_PER_STEP, M2_GW, M2_STATE
    bb = M2_D_INNER // n
    cb = bb + M2_GROUPS
    col = jnp.arange(2 * LANE)[:, None] % LANE
    head = jnp.arange(gw)[None, :] // M2_HEADDIM
    expand = jnp.stack([col == head, col == head + M2_HPG]).astype(BF16)
    return pl.pallas_call(
        _ssd_body,
        out_shape=jax.ShapeDtypeStruct((seq, M2_D_INNER), BF16),
        grid=(M2_GROUPS, seq // T),
        in_specs=[pl.BlockSpec((T, gw), lambda g, c: (c, g)),
                  pl.BlockSpec((T, n), lambda g, c: (c, bb + g)),
                  pl.BlockSpec((T, n), lambda g, c: (c, cb + g)),
                  pl.BlockSpec((T, gw), lambda g, c: (c, Z_M2 // gw + g)),
                  pl.BlockSpec((M2_HPG, T), lambda g, c: (g, c)),
                  pl.BlockSpec((None, T, LANE), lambda g, c: (g, c, 0)),
                  pl.BlockSpec((None, T, 2 * LANE), lambda g, c: (g, c, 0)),
                  pl.BlockSpec((2, 2 * LANE, gw), lambda g, c: (0, 0, 0)),
                  pl.BlockSpec((1, gw), lambda g, c: (0, g)),
                  pl.BlockSpec((1, gw), lambda g, c: (0, g))],
        out_specs=pl.BlockSpec((T, gw), lambda g, c: (c, g)),
        scratch_shapes=[pltpu.VMEM((n, gw), F32)],
        compiler_params=_params("arbitrary", "arbitrary"),
        name="ssd",
    )(xbc, xbc, xbc, zs, arow, acol, ehm, expand,
      jnp.repeat(d_skip.astype(F32), M2_HEADDIM).reshape(1, M2_D_INNER),
      norm_g.astype(F32).reshape(1, M2_D_INNER))


def _merge_body(a1_ref, w1_ref, a2_ref, w2_ref, g1_ref, g2_ref, o_ref):
    m1 = jnp.dot(a1_ref[...], w1_ref[...], preferred_element_type=F32)
    m2 = jnp.dot(a2_ref[...], w2_ref[...], preferred_element_type=F32)
    o_ref[...] = (g1_ref[...].astype(F32) * m1 + g2_ref[...].astype(F32) * m2).astype(BF16)


def _merge(y1, w1, y2, w2, gates, tm, tn):
    seq = y1.shape[0]
    nb = D_MODEL // tn
    return pl.pallas_call(
        _merge_body,
        out_shape=jax.ShapeDtypeStruct((seq, D_MODEL), BF16),
        grid=(seq // tm, nb),
        in_specs=[pl.BlockSpec((tm, D_MODEL), lambda i, j: (i, 0)),
                  pl.BlockSpec((D_MODEL, tn), lambda i, j: (0, j)),
                  pl.BlockSpec((tm, M2_D_INNER), lambda i, j: (i, 0)),
                  pl.BlockSpec((M2_D_INNER, tn), lambda i, j: (0, j)),
                  pl.BlockSpec((tm, tn), lambda i, j: (i, j)),
                  pl.BlockSpec((tm, tn), lambda i, j: (i, nb + j))],
        out_specs=pl.BlockSpec((tm, tn), lambda i, j: (i, j)),
        compiler_params=_params("arbitrary", "arbitrary"),
        name="merge",
    )(y1, w1, y2, w2, gates, gates)


def _out_body(a_ref, w_hbm, x_ref, gate_ref, o_ref, wf32, wbf, sems, *, n_j):
    @pl.when(pl.program_id(1) == 0)
    def _():
        _fetch_cast_weights(w_hbm, (0,), wf32, wbf, sems, pl.program_id(0), n_j)

    m = jnp.dot(a_ref[...], wbf[...], preferred_element_type=F32)
    o_ref[...] = x_ref[...] + gate_ref[...] * m


def _out_proj(merged, w_out3, x2, mod, tm, tn):
    seq = merged.shape[0]
    nb = D_MODEL // tn
    return pl.pallas_call(
        functools.partial(_out_body, n_j=nb),
        out_shape=jax.ShapeDtypeStruct((seq, D_MODEL), F32),
        grid=(nb, seq // tm),
        in_specs=[pl.BlockSpec((tm, D_MODEL), lambda j, i: (i, 0)),
                  pl.BlockSpec(memory_space=pl.ANY),
                  pl.BlockSpec((tm, tn), lambda j, i: (i, j)),
                  pl.BlockSpec((1, tn), lambda j, i: (0, 2 * nb + j))],
        out_specs=pl.BlockSpec((tm, tn), lambda j, i: (i, j)),
        scratch_shapes=[pltpu.VMEM((D_MODEL, tn), F32), pltpu.VMEM((D_MODEL, tn), BF16),
                        pltpu.SemaphoreType.DMA((1,))],
        compiler_params=_params("arbitrary", "arbitrary"),
        name="outproj",
    )(merged, w_out3, x2, mod)


def _fnorm_body(x_ref, g_ref, o_ref):
    o_ref[...] = _rms(x_ref[...]) * g_ref[...]


def _final_norm(x2, g, tm):
    seq = x2.shape[0]
    return pl.pallas_call(
        _fnorm_body,
        out_shape=jax.ShapeDtypeStruct((seq, D_MODEL), F32),
        grid=(seq // tm,),
        in_specs=[pl.BlockSpec((tm, D_MODEL), lambda i: (i, 0)),
                  pl.BlockSpec((1, D_MODEL), lambda i: (0, 0))],
        out_specs=pl.BlockSpec((tm, D_MODEL), lambda i: (i, 0)),
        compiler_params=_params("arbitrary"),
        name="fnorm",
    )(x2, g)


def _tiles(seq):
    return dict(tm=min(1024, seq), tm_merge=min(512, seq), tm_norm=min(256, seq), tm_glu=min(2048, seq),
                tn=512, tn_in=1024, tn_glu=256)


def kernel(x, c, w_ada, b_ada, norm_g, w_in, s5_lambda_re, s5_lambda_im, s5_log_step, s5_b_re, s5_b_im,
           s5_c_re, s5_c_im, s5_d, s5_w_glu, s5_b_glu, m2_conv_w, m2_conv_b, m2_dt_bias, m2_a_log, m2_d,
           m2_norm_g, w_br_s5, w_br_m2, w_out, final_g):
    bsz, seq, _ = x.shape
    assert bsz == 1 and w_ada.shape[0] == 1 and seq % (S5_Q * SUBLANE) == 0 and seq % (M2_CHUNK * M2_CHUNKS_PER_STEP) == 0
    tl = _tiles(seq)
    tm, tn = tl["tm"], tl["tn"]
    x2 = x.reshape(seq, D_MODEL)

    w1, w2 = w_br_s5[0].astype(BF16), w_br_m2[0].astype(BF16)
    conv = (m2_conv_w[0].astype(F32), m2_conv_b[0].astype(F32).reshape(1, M2_CONV_DIM))

    mod = _ada_mod(c, w_ada[0], b_ada[0])
    h = _norm_mod(x2, norm_g[0].reshape(1, D_MODEL), mod, tl["tm_norm"])
    tn_in = tl["tn_in"]
    u = _proj(h, w_in, 0, D_MODEL, F32, tm, tn_in, "inproj_u")
    zs = _proj(h, w_in, WIN_S5Z, WIN_XBC - WIN_S5Z, BF16, tm, tn_in, "inproj_z", act=jax.nn.silu)
    xbc = _proj(h, w_in, WIN_XBC, M2_CONV_DIM, BF16, tm, tn_in, "inproj_x", conv=conv)
    gates = _proj(h, w_in, WIN_GATES, 2 * D_MODEL, BF16, tm, tn_in, "inproj_g", act=jax.nn.sigmoid)

    rows = _s5_rows(s5_lambda_re[0], s5_lambda_im[0], s5_log_step[0])
    wb, wct, w2t = _s5_prep(rows, s5_b_re[0], s5_b_im[0], s5_c_re[0], s5_c_im[0])
    ys5 = _s5(u, rows, wb, wct, w2t, s5_d[0])
    y1 = _glu(ys5, s5_w_glu, s5_b_glu[0].astype(F32).reshape(1, -1), zs, tl["tm_glu"], tl["tn_glu"])

    arow, acol, ehm = _dt_prep(h, w_in, m2_dt_bias[0], m2_a_log[0])
    y2 = _ssd(xbc, zs, arow, acol, ehm, m2_d[0], m2_norm_g[0])

    merged = _merge(y1, w1, y2, w2, gates, tl["tm_merge"], tn)
    xn = _out_proj(merged, w_out, x2, mod, tm, tn)
    out = _final_norm(xn, final_g.astype(F32).reshape(1, D_MODEL), tl["tm_norm"])
    return out.reshape(bsz, seq, D_MODEL)
```

```python
import functools
import math

import jax
import jax.numpy as jnp
from jax import lax
from jax.experimental import pallas as pl
from jax.experimental.pallas import tpu as pltpu

F32 = jnp.float32
BF16 = jnp.bfloat16

LANE = 128
SUBLANE = 8
VMEM_LIMIT_BYTES = 60 * 1024 * 1024

D_MODEL = 4096
EPS = 1e-6

S5_GROUP = 16
S5_STATE = 64
S5_Q = 8
S5_TILE_GROUPS = LANE // S5_GROUP
S5_TILE_STATE = S5_TILE_GROUPS * S5_STATE
S5_TILES = D_MODEL // LANE

M2_D_INNER = 2 * D_MODEL
M2_HEADDIM = 64
M2_HEADS = M2_D_INNER // M2_HEADDIM
M2_GROUPS = 8
M2_HPG = M2_HEADS // M2_GROUPS
M2_STATE = 128
M2_CHUNK = 128
M2_CHUNKS_PER_STEP = 8
M2_GW = M2_HPG * M2_HEADDIM
M2_GN = M2_GROUPS * M2_STATE
M2_CONV_DIM = M2_D_INNER + 2 * M2_GN

WIN_S5Z = D_MODEL
WIN_XBC = 2 * D_MODEL + M2_D_INNER
WIN_DT = WIN_XBC + M2_CONV_DIM
WIN_GATES = WIN_DT + M2_HEADS
Z_S5 = 0
Z_M2 = D_MODEL


def _params(*sem):
    return pltpu.CompilerParams(dimension_semantics=sem, vmem_limit_bytes=VMEM_LIMIT_BYTES)


MOD_TN = 512


def _mod_body(cb_ref, w_ref, b_ref, o_ref):
    cb = cb_ref[...]
    for q in range(MOD_TN // LANE):
        sl = slice(q * LANE, (q + 1) * LANE)
        o_ref[:, sl] = jnp.sum(w_ref[:, sl] * cb, axis=0, keepdims=True) + b_ref[:, sl]


def _ada_mod(c, w_ada, b_ada):
    n = w_ada.shape[1]
    cb = jnp.broadcast_to(c.reshape(D_MODEL, 1), (D_MODEL, LANE))
    return pl.pallas_call(
        _mod_body,
        out_shape=jax.ShapeDtypeStruct((1, n), F32),
        grid=(n // MOD_TN,),
        in_specs=[pl.BlockSpec((D_MODEL, LANE), lambda j: (0, 0)),
                  pl.BlockSpec((D_MODEL, MOD_TN), lambda j: (0, j)),
                  pl.BlockSpec((1, MOD_TN), lambda j: (0, j))],
        out_specs=pl.BlockSpec((1, MOD_TN), lambda j: (0, j)),
        compiler_params=_params("arbitrary"),
        name="mod",
    )(cb, w_ada, b_ada.reshape(1, n))


def _rms(x):
    return x * lax.rsqrt(jnp.mean(x * x, axis=-1, keepdims=True) + EPS)


def _norm_body(x_ref, g_ref, shift_ref, scale_ref, o_ref):
    h = _rms(x_ref[...]) * g_ref[...] * (1.0 + scale_ref[...]) + shift_ref[...]
    o_ref[...] = h.astype(BF16)


def _norm_mod(x2, g, mod, tm):
    seq = x2.shape[0]
    return pl.pallas_call(
        _norm_body,
        out_shape=jax.ShapeDtypeStruct((seq, D_MODEL), BF16),
        grid=(seq // tm,),
        in_specs=[pl.BlockSpec((tm, D_MODEL), lambda i: (i, 0)),
                  pl.BlockSpec((1, D_MODEL), lambda i: (0, 0)),
                  pl.BlockSpec((1, D_MODEL), lambda i: (0, 0)),
                  pl.BlockSpec((1, D_MODEL), lambda i: (0, 1))],
        out_specs=pl.BlockSpec((tm, D_MODEL), lambda i: (i, 0)),
        compiler_params=_params("arbitrary"),
        name="norm",
    )(x2, g, mod, mod)


CAST_ROWS = 512


def _fetch_cast_weights(w_hbm, col_starts, wf32, wbf, sems, j, n_j):
    tn = wf32.shape[1] // len(col_starts)

    def copies(jj):
        return [pltpu.make_async_copy(w_hbm.at[0, :, pl.ds(pl.multiple_of(c0 + jj * tn, LANE), tn)],
                                      wf32.at[:, pl.ds(q * tn, tn)], sems.at[q])
                for q, c0 in enumerate(col_starts)]

    @pl.when(j == 0)
    def _():
        for cp in copies(0):
            cp.start()

    for cp in copies(j):
        cp.wait()
    for r0 in range(0, wf32.shape[0], CAST_ROWS):
        wbf[r0:r0 + CAST_ROWS, :] = wf32[r0:r0 + CAST_ROWS, :].astype(BF16)

    @pl.when(j + 1 < n_j)
    def _():
        for cp in copies(j + 1):
            cp.start()


def _proj_body(*refs, col0, n_j, act, conv):
    if conv:
        a_ref, w_hbm, cw_ref, cb_ref, o_ref, wf32, wbf, sems, cbuf = refs
    else:
        a_ref, w_hbm, o_ref, wf32, wbf, sems = refs
    tm, tn = o_ref.shape

    @pl.when(pl.program_id(1) == 0)
    def _():
        _fetch_cast_weights(w_hbm, (col0,), wf32, wbf, sems, pl.program_id(0), n_j)
        if conv:
            cbuf[:SUBLANE, :] = jnp.zeros((SUBLANE, tn), F32)

    r = jnp.dot(a_ref[...], wbf[...], preferred_element_type=F32)
    if conv:
        k = cw_ref.shape[0]
        cbuf[SUBLANE:, :] = r
        y = cb_ref[...] + cw_ref[k - 1:k, :] * r
        for d in range(1, k):
            y = y + cw_ref[k - 1 - d:k - d, :] * cbuf[pl.ds(SUBLANE - d, tm), :]
        o_ref[...] = jax.nn.silu(y).astype(o_ref.dtype)
        cbuf[:SUBLANE, :] = r[tm - SUBLANE:]
    else:
        o_ref[...] = (act(r) if act else r).astype(o_ref.dtype)


def _proj(h, w3, col0, n, out_dtype, tm, tn, name, act=None, conv=None):
    seq, k = h.shape
    assert col0 % LANE == 0 and n % tn == 0
    extra, extra_specs = [], []
    scratch = [pltpu.VMEM((k, tn), F32), pltpu.VMEM((k, tn), BF16), pltpu.SemaphoreType.DMA((1,))]
    if conv:
        extra = list(conv)
        extra_specs = [pl.BlockSpec((conv[0].shape[0], tn), lambda j, i: (0, j)),
                       pl.BlockSpec((1, tn), lambda j, i: (0, j))]
        scratch.append(pltpu.VMEM((tm + SUBLANE, tn), F32))
    return pl.pallas_call(
        functools.partial(_proj_body, col0=col0, n_j=n // tn, act=act, conv=bool(conv)),
        out_shape=jax.ShapeDtypeStruct((seq, n), out_dtype),
        grid=(n // tn, seq // tm),
        in_specs=[pl.BlockSpec((tm, k), lambda j, i: (i, 0)), pl.BlockSpec(memory_space=pl.ANY)] + extra_specs,
        out_specs=pl.BlockSpec((tm, tn), lambda j, i: (i, j)),
        scratch_shapes=scratch,
        compiler_params=_params("arbitrary", "arbitrary"),
        name=name,
    )(h, w3, *extra)


def _lbar(lre, lim, log_step):
    step = jnp.exp(log_step)
    mag = jnp.exp(lre * step)
    return mag * jnp.cos(lim * step), mag * jnp.sin(lim * step)


def _s5prep_body(lre_ref, lim_ref, st_ref, btr_ref, bti_ref, ctr_ref, cti_ref,
                 wb_ref, wct_ref, w2_ref):
    ns = S5_TILE_STATE
    lre, lim = lre_ref[...], lim_ref[...]
    lbr, lbi = _lbar(lre, lim, st_ref[...])
    den = lre * lre + lim * lim
    nr, ni = lbr - 1.0, lbi
    cr = (nr * lre + ni * lim) / den
    ci = (ni * lre - nr * lim) / den
    row = jnp.right_shift(lax.broadcasted_iota(jnp.int32, (LANE, ns), 0), int(math.log2(S5_GROUP)))
    col = jnp.right_shift(lax.broadcasted_iota(jnp.int32, (LANE, ns), 1), int(math.log2(S5_STATE)))
    same = row == col
    wide = lambda ref: jnp.where(same, jnp.tile(ref[...], (1, ns // LANE)), 0.0)
    btr, bti, ctr, cti = wide(btr_ref), wide(bti_ref), wide(ctr_ref), wide(cti_ref)
    bbr = cr * btr - ci * bti
    bbi = cr * bti + ci * btr
    pows = [(jnp.ones_like(lbr), jnp.zeros_like(lbr))]
    for _ in range(S5_Q):
        pr, pi = pows[-1]
        pows.append((pr * lbr - pi * lbi, pr * lbi + pi * lbr))
    for s in range(S5_Q):
        pr, pi = pows[S5_Q - 1 - s]
        rows = slice(s * LANE, (s + 1) * LANE)
        wb_ref[rows, :ns] = (pr * bbr - pi * bbi).astype(BF16)
        wb_ref[rows, ns:] = (pr * bbi + pi * bbr).astype(BF16)
    for t in range(S5_Q):
        pr, pi = pows[t + 1]
        rows = slice(t * LANE, (t + 1) * LANE)
        wct_ref[rows, :ns] = (ctr * pr - cti * pi).astype(BF16)
        wct_ref[rows, ns:] = (-(ctr * pi + cti * pr)).astype(BF16)
    ccat = jnp.concatenate([ctr, -cti], axis=1).astype(BF16)
    kall = lax.dot_general(wb_ref[...], ccat, (((1,), (1,)), ((), ())),
                           preferred_element_type=F32)
    w2_ref[LANE:, :LANE] = jnp.zeros((LANE, LANE), BF16)
    for tau in range(S5_Q):
        s = S5_Q - 1 - tau
        k_tau = kall[s * LANE:(s + 1) * LANE, :].astype(BF16)
        w2_ref[:LANE, tau * LANE:(tau + 1) * LANE] = k_tau
        if tau + 1 < S5_Q:
            w2_ref[LANE:, (tau + 1) * LANE:(tau + 2) * LANE] = k_tau


def _s5_rows(lam_re, lam_im, log_step):
    row = lambda a: a.astype(F32).reshape(S5_TILES, 1, S5_TILE_STATE)
    return row(lam_re), row(lam_im), row(jnp.broadcast_to(log_step[:, None], lam_re.shape))


def _s5_prep(rows, b_re, b_im, c_re, c_im):
    ns = S5_TILE_STATE

    def tiled(a):
        a = a.astype(F32).reshape(S5_TILES, LANE, S5_STATE)
        return jnp.tile(a, (1, 1, LANE // S5_STATE))

    row_spec = pl.BlockSpec((None, 1, ns), lambda j: (j, 0, 0))
    mat_spec = pl.BlockSpec((None, LANE, LANE), lambda j: (j, 0, 0))
    return pl.pallas_call(
        _s5prep_body,
        out_shape=(jax.ShapeDtypeStruct((S5_TILES, S5_Q * LANE, 2 * ns), BF16),
                   jax.ShapeDtypeStruct((S5_TILES, S5_Q * LANE, 2 * ns), BF16),
                   jax.ShapeDtypeStruct((S5_TILES, 2 * LANE, S5_Q * LANE), BF16)),
        grid=(S5_TILES,),
        in_specs=[row_spec, row_spec, row_spec, mat_spec, mat_spec, mat_spec, mat_spec],
        out_specs=(pl.BlockSpec((None, S5_Q * LANE, 2 * ns), lambda j: (j, 0, 0)),
                   pl.BlockSpec((None, S5_Q * LANE, 2 * ns), lambda j: (j, 0, 0)),
                   pl.BlockSpec((None, 2 * LANE, S5_Q * LANE), lambda j: (j, 0, 0))),
        compiler_params=_params("arbitrary"),
        name="s5prep",
    )(*rows, tiled(b_re.swapaxes(1, 2)), tiled(b_im.swapaxes(1, 2)), tiled(c_re), tiled(c_im))


def _s5_body(u_ref, lre_ref, lim_ref, st_ref, wb_ref, wct_ref, w2_ref, d_ref, o_ref, e_scr, f_scr, acc, ynat):
    ns = S5_TILE_STATE
    nc = e_scr.shape[0]
    half = nc // 2
    even_rows = lambda s: pl.ds(s, half, stride=2 * S5_Q)
    odd_rows = lambda s: pl.ds(S5_Q + s, half, stride=2 * S5_Q)
    slabs = [jnp.concatenate([u_ref[even_rows(s), :], u_ref[odd_rows(s), :]], axis=0)
             for s in range(S5_Q)]
    sb = [x.astype(BF16) for x in slabs]

    e_scr[...] = jnp.dot(jnp.concatenate(sb, axis=1), wb_ref[...], preferred_element_type=F32)
    ar, ai = _lbar(lre_ref[...], lim_ref[...], st_ref[...])
    for _ in range(int(math.log2(S5_Q))):
        ar, ai = ar * ar - ai * ai, 2.0 * ar * ai

    ee_r, ee_i = e_scr[:half, :ns], e_scr[:half, ns:]
    f_scr[:, :ns] = ar * ee_r - ai * ee_i + e_scr[half:, :ns]
    f_scr[:, ns:] = ar * ee_i + ai * ee_r + e_scr[half:, ns:]
    a2r, a2i = ar * ar - ai * ai, 2.0 * ar * ai

    n_pairs = S5_Q // 2
    seg = half // n_pairs
    hr = hi = jnp.zeros((1, ns), F32)
    for p in range(n_pairs):
        for c in range(p * seg, (p + 1) * seg):
            er, ei = f_scr[c:c + 1, :ns], f_scr[c:c + 1, ns:]
            f_scr[c:c + 1, :ns] = hr
            f_scr[c:c + 1, ns:] = hi
            hr, hi = a2r * hr - a2i * hi + er, a2r * hi + a2i * hr + ei
        s = 2 * p
        pair = jnp.concatenate([sb[s], sb[s + 1]], axis=1)
        width = (S5_Q - s) * LANE
        toep = jnp.dot(pair, w2_ref[:, :width], preferred_element_type=F32)
        if p == 0:
            acc[...] = toep
        else:
            acc[:, s * LANE:] += toep

    g_r, g_i = f_scr[:, :ns], f_scr[:, ns:]
    ee_r, ee_i = e_scr[:half, :ns], e_scr[:half, ns:]
    e_scr[half:, :ns] = ar * g_r - ai * g_i + ee_r
    e_scr[half:, ns:] = ar * g_i + ai * g_r + ee_i
    e_scr[:half, :ns] = g_r
    e_scr[:half, ns:] = g_i
    acc[...] += lax.dot_general(e_scr[...].astype(BF16), wct_ref[...], (((1,), (1,)), ((), ())),
                                preferred_element_type=F32)
    d = d_ref[...]
    for t in range(S5_Q):
        y = jax.nn.gelu(acc[:, t * LANE:(t + 1) * LANE] + d * slabs[t])
        ynat[even_rows(t), :] = y[:half]
        ynat[odd_rows(t), :] = y[half:]
    o_ref[...] = ynat[...].astype(BF16)


def _s5(u, rows, wb, wct, w2, d_skip):
    seq = u.shape[0]
    nc, ns = seq // S5_Q, S5_TILE_STATE
    row_spec = pl.BlockSpec((None, 1, ns), lambda j: (j, 0, 0))
    return pl.pallas_call(
        _s5_body,
        out_shape=jax.ShapeDtypeStruct((seq, D_MODEL), BF16),
        grid=(S5_TILES,),
        in_specs=[pl.BlockSpec((seq, LANE), lambda j: (0, j)), row_spec, row_spec, row_spec,
                  pl.BlockSpec((None, S5_Q * LANE, 2 * ns), lambda j: (j, 0, 0)),
                  pl.BlockSpec((None, S5_Q * LANE, 2 * ns), lambda j: (j, 0, 0)),
                  pl.BlockSpec((None, 2 * LANE, S5_Q * LANE), lambda j: (j, 0, 0)),
                  pl.BlockSpec((1, LANE), lambda j: (0, j))],
        out_specs=pl.BlockSpec((seq, LANE), lambda j: (0, j)),
        scratch_shapes=[pltpu.VMEM((nc, 2 * ns), F32), pltpu.VMEM((nc // 2, 2 * ns), F32),
                        pltpu.VMEM((nc, S5_Q * LANE), F32), pltpu.VMEM((seq, LANE), F32)],
        compiler_params=_params("arbitrary"),
        name="s5",
    )(u, *rows, wb, wct, w2, d_skip.astype(F32).reshape(1, D_MODEL))


def _glu_body(a_ref, w_hbm, ba_ref, bb_ref, z_ref, o_ref, wf32, wbf, sems, *, n_j):
    tn = o_ref.shape[1]

    @pl.when(pl.program_id(1) == 0)
    def _():
        _fetch_cast_weights(w_hbm, (0, n_j * tn), wf32, wbf, sems, pl.program_id(0), n_j)

    g = jnp.dot(a_ref[...], wbf[...], preferred_element_type=F32)
    ga = g[:, :tn] + ba_ref[...]
    gb = g[:, tn:] + bb_ref[...]
    o_ref[...] = (ga * jax.nn.sigmoid(gb) * z_ref[...].astype(F32)).astype(BF16)


def _glu(ys5, w_glu3, b_glu, zs, tm, tn):
    seq = ys5.shape[0]
    nb = D_MODEL // tn
    return pl.pallas_call(
        functools.partial(_glu_body, n_j=nb),
        out_shape=jax.ShapeDtypeStruct((seq, D_MODEL), BF16),
        grid=(nb, seq // tm),
        in_specs=[pl.BlockSpec((tm, D_MODEL), lambda j, i: (i, 0)),
                  pl.BlockSpec(memory_space=pl.ANY),
                  pl.BlockSpec((1, tn), lambda j, i: (0, j)),
                  pl.BlockSpec((1, tn), lambda j, i: (0, nb + j)),
                  pl.BlockSpec((tm, tn), lambda j, i: (i, Z_S5 // tn + j))],
        out_specs=pl.BlockSpec((tm, tn), lambda j, i: (i, j)),
        scratch_shapes=[pltpu.VMEM((D_MODEL, 2 * tn), F32), pltpu.VMEM((D_MODEL, 2 * tn), BF16),
                        pltpu.SemaphoreType.DMA((2,))],
        compiler_params=_params("arbitrary", "arbitrary"),
        name="glu",
    )(ys5, w_glu3, b_glu, b_glu, zs)


def _split3(x):
    hi = x.astype(BF16)
    r1 = x - hi.astype(F32)
    mid = r1.astype(BF16)
    lo = (r1 - mid.astype(F32)).astype(BF16)
    return hi, mid, lo


def _split2(x):
    hi = x.astype(BF16)
    mid = (x - hi.astype(F32)).astype(BF16)
    return jnp.concatenate([hi, mid], axis=1)


def _dtprep_body(h_ref, w_ref, dtb_ref, alog_ref, tri_ref, arow_ref, acol_ref, ehm_ref, wbf):
    T = M2_CHUNK

    @pl.when(pl.program_id(0) == 0)
    def _():
        wbf[...] = w_ref[...].astype(BF16)

    raw = jnp.dot(h_ref[...].astype(F32), wbf[...].astype(F32), preferred_element_type=F32)
    tri = tri_ref[...]
    neg_a = -jnp.exp(alog_ref[...])
    pad1 = jnp.zeros((T - M2_HPG, T), F32)
    pad2 = jnp.zeros((T - 2 * M2_HPG, T), F32)
    for ci in range(M2_CHUNKS_PER_STEP):
        rows = slice(ci * T, (ci + 1) * T)
        dt = jax.nn.softplus(raw[rows, :].T + dtb_ref[...])
        hi, mid, lo = _split3(dt * neg_a)
        acum = (jnp.dot(hi, tri, preferred_element_type=F32) + jnp.dot(mid, tri, preferred_element_type=F32)
                + jnp.dot(lo, tri, preferred_element_type=F32))
        rem = acum[:, T - 1:T] - acum
        arow_ref[:, rows] = acum - jnp.log(dt)
        dte = dt * jnp.exp(rem)
        eac = jnp.exp(acum)
        for g in range(M2_GROUPS):
            hs = slice(g * M2_HPG, (g + 1) * M2_HPG)
            acol_ref[g, rows, :] = jnp.concatenate([acum[hs], pad1], axis=0).T
            ehm_ref[g, rows, :] = _split2(jnp.concatenate([dte[hs], eac[hs], pad2], axis=0).T)


def _dt_prep(h, w_in3, dt_bias, a_log):
    seq = h.shape[0]
    T, ts = M2_CHUNK, M2_CHUNK * M2_CHUNKS_PER_STEP
    lanes = lambda v: jnp.broadcast_to(v.astype(F32)[:, None], (M2_HEADS, T))
    tri = (jnp.arange(T)[:, None] <= jnp.arange(T)[None, :]).astype(BF16)
    return pl.pallas_call(
        _dtprep_body,
        out_shape=(jax.ShapeDtypeStruct((M2_HEADS, seq), F32),
                   jax.ShapeDtypeStruct((M2_GROUPS, seq, LANE), F32),
                   jax.ShapeDtypeStruct((M2_GROUPS, seq, 2 * LANE), BF16)),
        grid=(seq // ts,),
        in_specs=[pl.BlockSpec((ts, D_MODEL), lambda c: (c, 0)),
                  pl.BlockSpec((None, D_MODEL, M2_HEADS), lambda c: (0, 0, WIN_DT // M2_HEADS)),
                  pl.BlockSpec((M2_HEADS, T), lambda c: (0, 0)),
                  pl.BlockSpec((M2_HEADS, T), lambda c: (0, 0)),
                  pl.BlockSpec((T, T), lambda c: (0, 0))],
        out_specs=(pl.BlockSpec((M2_HEADS, ts), lambda c: (0, c)),
                   pl.BlockSpec((M2_GROUPS, ts, LANE), lambda c: (0, c, 0)),
                   pl.BlockSpec((M2_GROUPS, ts, 2 * LANE), lambda c: (0, c, 0))),
        scratch_shapes=[pltpu.VMEM((D_MODEL, M2_HEADS), BF16)],
        compiler_params=_params("arbitrary"),
        name="dtprep",
    )(h, w_in3, lanes(dt_bias), lanes(a_log), tri)


def _ssd_body(x_ref, b_ref, c_ref, z_ref, arow_ref, acol_ref, ehm_ref, exp_ref, dskip_ref, ng_ref, o_ref, state):
    @pl.when(pl.program_id(1) == 0)
    def _():
        state[...] = jnp.zeros_like(state)

    for ci in range(M2_CHUNKS_PER_STEP):
        rows = slice(ci * M2_CHUNK, (ci + 1) * M2_CHUNK)
        o_ref[rows, :] = _ssd_chunk(x_ref[rows, :], b_ref[rows, :], c_ref[rows, :], z_ref[rows, :],
                                    arow_ref[:, rows], acol_ref[rows, :], ehm_ref[rows, :],
                                    exp_ref, dskip_ref, ng_ref, state)


def _ssd_chunk(xb, bm_b, cm_b, zb, arow, acol, ehm, exp_ref, dskip_ref, ng_ref, state):
    T = M2_CHUNK
    xs = xb.astype(F32)
    dte_e = jnp.dot(ehm, exp_ref[0], preferred_element_type=F32)
    eac_e = jnp.dot(ehm, exp_ref[1], preferred_element_type=F32)

    scores = lax.dot_general(cm_b, bm_b, (((1,), (1,)), ((), ())), preferred_element_type=F32)
    y_off = jnp.dot(cm_b, state[...].astype(BF16), preferred_element_type=F32) * eac_e

    causal = (lax.broadcasted_iota(jnp.int32, (T, T), 0) >= lax.broadcasted_iota(jnp.int32, (T, T), 1))
    left = lax.broadcasted_iota(jnp.int32, (T, LANE), 1) < M2_HEADDIM
    zero = jnp.zeros((T, LANE), BF16)
    ys = []
    for p in range(M2_HPG // 2):
        lhs = []
        for h in (2 * p, 2 * p + 1):
            seg = jnp.broadcast_to(acol[:, h:h + 1], (T, T)) - jnp.broadcast_to(arow[h:h + 1, :], (T, T))
            lhs.append((scores * jnp.exp(jnp.where(causal, seg, -1e30))).astype(BF16))
        xp = xb[:, p * LANE:(p + 1) * LANE]
        rhs = jnp.concatenate([jnp.where(left, xp, zero), jnp.where(left, zero, xp)], axis=0)
        ys.append(jnp.dot(jnp.concatenate(lhs, axis=1), rhs, preferred_element_type=F32))
    y = jnp.concatenate(ys, axis=1) + y_off + dskip_ref[...] * xs

    xw = (xs * dte_e).astype(BF16)
    state[...] = (state[...] * eac_e[T - 1:T, :]
                  + jnp.dot(bm_b.astype(F32).T.astype(BF16), xw, preferred_element_type=F32))

    y = y * zb.astype(F32)
    return (_rms(y) * ng_ref[...]).astype(BF16)


def _ssd(xbc, zs, arow, acol, ehm, d_skip, norm_g):
    seq = xbc.shape[0]
    T, gw, n = M2_CHUNK * M2_CHUNKS_PER_STEP, M2_GW, M2_STATE
    bb = M2_D_INNER // n
    cb = bb + M2_GROUPS
    col = jnp.arange(2 * LANE)[:, None] % LANE
    head = jnp.arange(gw)[None, :] // M2_HEADDIM
    expand = jnp.stack([col == head, col == head + M2_HPG]).astype(BF16)
    return pl.pallas_call(
        _ssd_body,
        out_shape=jax.ShapeDtypeStruct((seq, M2_D_INNER), BF16),
        grid=(M2_GROUPS, seq // T),
        in_specs=[pl.BlockSpec((T, gw), lambda g, c: (c, g)),
                  pl.BlockSpec((T, n), lambda g, c: (c, bb + g)),
                  pl.BlockSpec((T, n), lambda g, c: (c, cb + g)),
                  pl.BlockSpec((T, gw), lambda g, c: (c, Z_M2 // gw + g)),
                  pl.BlockSpec((M2_HPG, T), lambda g, c: (g, c)),
                  pl.BlockSpec((None, T, LANE), lambda g, c: (g, c, 0)),
                  pl.BlockSpec((None, T, 2 * LANE), lambda g, c: (g, c, 0)),
                  pl.BlockSpec((2, 2 * LANE, gw), lambda g, c: (0, 0, 0)),
                  pl.BlockSpec((1, gw), lambda g, c: (0, g)),
                  pl.BlockSpec((1, gw), lambda g, c: (0, g))],
        out_specs=pl.BlockSpec((T, gw), lambda g, c: (c, g)),
        scratch_shapes=[pltpu.VMEM((n, gw), F32)],
        compiler_params=_params("arbitrary", "arbitrary"),
        name="ssd",
    )(xbc, xbc, xbc, zs, arow, acol, ehm, expand,
      jnp.repeat(d_skip.astype(F32), M2_HEADDIM).reshape(1, M2_D_INNER),
      norm_g.astype(F32).reshape(1, M2_D_INNER))


def _merge_body(a1_ref, w1_ref, a2_ref, w2_ref, g1_ref, g2_ref, o_ref):
    m1 = jnp.dot(a1_ref[...], w1_ref[...], preferred_element_type=F32)
    m2 = jnp.dot(a2_ref[...], w2_ref[...], preferred_element_type=F32)
    o_ref[...] = (g1_ref[...].astype(F32) * m1 + g2_ref[...].astype(F32) * m2).astype(BF16)


def _merge(y1, w1, y2, w2, gates, tm, tn):
    seq = y1.shape[0]
    nb = D_MODEL // tn
    return pl.pallas_call(
        _merge_body,
        out_shape=jax.ShapeDtypeStruct((seq, D_MODEL), BF16),
        grid=(seq // tm, nb),
        in_specs=[pl.BlockSpec((tm, D_MODEL), lambda i, j: (i, 0)),
                  pl.BlockSpec((D_MODEL, tn), lambda i, j: (0, j)),
                  pl.BlockSpec((tm, M2_D_INNER), lambda i, j: (i, 0)),
                  pl.BlockSpec((M2_D_INNER, tn), lambda i, j: (0, j)),
                  pl.BlockSpec((tm, tn), lambda i, j: (i, j)),
                  pl.BlockSpec((tm, tn), lambda i, j: (i, nb + j))],
        out_specs=pl.BlockSpec((tm, tn), lambda i, j: (i, j)),
        compiler_params=_params("arbitrary", "arbitrary"),
        name="merge",
    )(y1, w1, y2, w2, gates, gates)


def _out_body(a_ref, w_hbm, x_ref, gate_ref, o_ref, wf32, wbf, sems, *, n_j):
    @pl.when(pl.program_id(1) == 0)
    def _():
        _fetch_cast_weights(w_hbm, (0,), wf32, wbf, sems, pl.program_id(0), n_j)

    m = jnp.dot(a_ref[...], wbf[...], preferred_element_type=F32)
    o_ref[...] = x_ref[...] + gate_ref[...] * m


def _out_proj(merged, w_out3, x2, mod, tm, tn):
    seq = merged.shape[0]
    nb = D_MODEL // tn
    return pl.pallas_call(
        functools.partial(_out_body, n_j=nb),
        out_shape=jax.ShapeDtypeStruct((seq, D_MODEL), F32),
        grid=(nb, seq // tm),
        in_specs=[pl.BlockSpec((tm, D_MODEL), lambda j, i: (i, 0)),
                  pl.BlockSpec(memory_space=pl.ANY),
                  pl.BlockSpec((tm, tn), lambda j, i: (i, j)),
                  pl.BlockSpec((1, tn), lambda j, i: (0, 2 * nb + j))],
        out_specs=pl.BlockSpec((tm, tn), lambda j, i: (i, j)),
        scratch_shapes=[pltpu.VMEM((D_MODEL, tn), F32), pltpu.VMEM((D_MODEL, tn), BF16),
                        pltpu.SemaphoreType.DMA((1,))],
        compiler_params=_params("arbitrary", "arbitrary"),
        name="outproj",
    )(merged, w_out3, x2, mod)


def _fnorm_body(x_ref, g_ref, o_ref):
    o_ref[...] = _rms(x_ref[...]) * g_ref[...]


def _final_norm(x2, g, tm):
    seq = x2.shape[0]
    return pl.pallas_call(
        _fnorm_body,
        out_shape=jax.ShapeDtypeStruct((seq, D_MODEL), F32),
        grid=(seq // tm,),
        in_specs=[pl.BlockSpec((tm, D_MODEL), lambda i: (i, 0)),
                  pl.BlockSpec((1, D_MODEL), lambda i: (0, 0))],
        out_specs=pl.BlockSpec((tm, D_MODEL), lambda i: (i, 0)),
        compiler_params=_params("arbitrary"),
        name="fnorm",
    )(x2, g)


def _tiles(seq):
    return dict(tm=min(1024, seq), tm_merge=min(512, seq), tm_norm=min(512, seq), tn=512, tn_in=1024, tn_glu=256)


def kernel(x, c, w_ada, b_ada, norm_g, w_in, s5_lambda_re, s5_lambda_im, s5_log_step, s5_b_re, s5_b_im,
           s5_c_re, s5_c_im, s5_d, s5_w_glu, s5_b_glu, m2_conv_w, m2_conv_b, m2_dt_bias, m2_a_log, m2_d,
           m2_norm_g, w_br_s5, w_br_m2, w_out, final_g):
    bsz, seq, _ = x.shape
    assert bsz == 1 and w_ada.shape[0] == 1 and seq % (S5_Q * SUBLANE) == 0 and seq % (M2_CHUNK * M2_CHUNKS_PER_STEP) == 0
    tl = _tiles(seq)
    tm, tn = tl["tm"], tl["tn"]
    x2 = x.reshape(seq, D_MODEL)

    w1, w2 = w_br_s5[0].astype(BF16), w_br_m2[0].astype(BF16)
    conv = (m2_conv_w[0].astype(F32), m2_conv_b[0].astype(F32).reshape(1, M2_CONV_DIM))

    mod = _ada_mod(c, w_ada[0], b_ada[0])
    h = _norm_mod(x2, norm_g[0].reshape(1, D_MODEL), mod, tl["tm_norm"])
    tn_in = tl["tn_in"]
    u = _proj(h, w_in, 0, D_MODEL, F32, tm, tn_in, "inproj_u")
    zs = _proj(h, w_in, WIN_S5Z, WIN_XBC - WIN_S5Z, BF16, tm, tn_in, "inproj_z", act=jax.nn.silu)
    xbc = _proj(h, w_in, WIN_XBC, M2_CONV_DIM, BF16, tm, tn_in, "inproj_x", conv=conv)
    gates = _proj(h, w_in, WIN_GATES, 2 * D_MODEL, BF16, tm, tn_in, "inproj_g", act=jax.nn.sigmoid)

    rows = _s5_rows(s5_lambda_re[0], s5_lambda_im[0], s5_log_step[0])
    wb, wct, w2t = _s5_prep(rows, s5_b_re[0], s5_b_im[0], s5_c_re[0], s5_c_im[0])
    ys5 = _s5(u, rows, wb, wct, w2t, s5_d[0])
    y1 = _glu(ys5, s5_w_glu, s5_b_glu[0].astype(F32).reshape(1, -1), zs, tm, tl["tn_glu"])

    arow, acol, ehm = _dt_prep(h, w_in, m2_dt_bias[0], m2_a_log[0])
    y2 = _ssd(xbc, zs, arow, acol, ehm, m2_d[0], m2_norm_g[0])

    merged = _merge(y1, w1, y2, w2, gates, tl["tm_merge"], tn)
    xn = _out_proj(merged, w_out, x2, mod, tm, tn)
    out = _final_norm(xn, final_g.astype(F32).reshape(1, D_MODEL), tl["tm_norm"])
    return out.reshape(bsz, seq, D_MODEL)
```

```python
import functools
import math

import jax
import jax.numpy as jnp
from jax import lax
from jax.experimental import pallas as pl
from jax.experimental.pallas import tpu as pltpu

F32 = jnp.float32
BF16 = jnp.bfloat16

LANE = 128
SUBLANE = 8
VMEM_LIMIT_BYTES = 60 * 1024 * 1024

D_MODEL = 4096
EPS = 1e-6

S5_GROUP = 16
S5_STATE = 64
S5_Q = 8
S5_TILE_GROUPS = LANE // S5_GROUP
S5_TILE_STATE = S5_TILE_GROUPS * S5_STATE
S5_TILES = D_MODEL // LANE

M2_D_INNER = 2 * D_MODEL
M2_HEADDIM = 64
M2_HEADS = M2_D_INNER // M2_HEADDIM
M2_GROUPS = 8
M2_HPG = M2_HEADS // M2_GROUPS
M2_STATE = 128
M2_CHUNK = 128
M2_CHUNKS_PER_STEP = 8
M2_GW = M2_HPG * M2_HEADDIM
M2_GN = M2_GROUPS * M2_STATE
M2_CONV_DIM = M2_D_INNER + 2 * M2_GN

WIN_S5Z = D_MODEL
WIN_XBC = 2 * D_MODEL + M2_D_INNER
WIN_DT = WIN_XBC + M2_CONV_DIM
WIN_GATES = WIN_DT + M2_HEADS
Z_S5 = 0
Z_M2 = D_MODEL


def _params(*sem):
    return pltpu.CompilerParams(dimension_semantics=sem, vmem_limit_bytes=VMEM_LIMIT_BYTES)


MOD_TN = 512


def _mod_body(cb_ref, w_ref, b_ref, o_ref):
    cb = cb_ref[...]
    for q in range(MOD_TN // LANE):
        sl = slice(q * LANE, (q + 1) * LANE)
        o_ref[:, sl] = jnp.sum(w_ref[:, sl] * cb, axis=0, keepdims=True) + b_ref[:, sl]


def _ada_mod(c, w_ada, b_ada):
    n = w_ada.shape[1]
    cb = jnp.broadcast_to(c.reshape(D_MODEL, 1), (D_MODEL, LANE))
    return pl.pallas_call(
        _mod_body,
        out_shape=jax.ShapeDtypeStruct((1, n), F32),
        grid=(n // MOD_TN,),
        in_specs=[pl.BlockSpec((D_MODEL, LANE), lambda j: (0, 0)),
                  pl.BlockSpec((D_MODEL, MOD_TN), lambda j: (0, j)),
                  pl.BlockSpec((1, MOD_TN), lambda j: (0, j))],
        out_specs=pl.BlockSpec((1, MOD_TN), lambda j: (0, j)),
        compiler_params=_params("arbitrary"),
        name="mod",
    )(cb, w_ada, b_ada.reshape(1, n))


def _rms(x):
    return x * lax.rsqrt(jnp.mean(x * x, axis=-1, keepdims=True) + EPS)


def _norm_body(x_ref, g_ref, shift_ref, scale_ref, o_ref):
    h = _rms(x_ref[...]) * g_ref[...] * (1.0 + scale_ref[...]) + shift_ref[...]
    o_ref[...] = h.astype(BF16)


def _norm_mod(x2, g, mod, tm):
    seq = x2.shape[0]
    return pl.pallas_call(
        _norm_body,
        out_shape=jax.ShapeDtypeStruct((seq, D_MODEL), BF16),
        grid=(seq // tm,),
        in_specs=[pl.BlockSpec((tm, D_MODEL), lambda i: (i, 0)),
                  pl.BlockSpec((1, D_MODEL), lambda i: (0, 0)),
                  pl.BlockSpec((1, D_MODEL), lambda i: (0, 0)),
                  pl.BlockSpec((1, D_MODEL), lambda i: (0, 1))],
        out_specs=pl.BlockSpec((tm, D_MODEL), lambda i: (i, 0)),
        compiler_params=_params("arbitrary"),
        name="norm",
    )(x2, g, mod, mod)


CAST_ROWS = 512


def _fetch_cast_weights(w_hbm, col_starts, wf32, wbf, sems, j, n_j):
    tn = wf32.shape[1] // len(col_starts)

    def copies(jj):
        return [pltpu.make_async_copy(w_hbm.at[0, :, pl.ds(pl.multiple_of(c0 + jj * tn, LANE), tn)],
                                      wf32.at[:, pl.ds(q * tn, tn)], sems.at[q])
                for q, c0 in enumerate(col_starts)]

    @pl.when(j == 0)
    def _():
        for cp in copies(0):
            cp.start()

    for cp in copies(j):
        cp.wait()
    for r0 in range(0, wf32.shape[0], CAST_ROWS):
        wbf[r0:r0 + CAST_ROWS, :] = wf32[r0:r0 + CAST_ROWS, :].astype(BF16)

    @pl.when(j + 1 < n_j)
    def _():
        for cp in copies(j + 1):
            cp.start()


def _proj_body(*refs, col0, n_j, act, conv, gated, added):
    g_ref = p_ref = None
    if conv:
        a_ref, w_hbm, cw_ref, cb_ref, o_ref, wf32, wbf, sems, cbuf = refs
    else:
        a_ref, w_hbm, *ins, o_ref, wf32, wbf, sems = refs
        g_ref = ins.pop(0) if gated else None
        p_ref = ins.pop(0) if added else None
    tm, tn = o_ref.shape

    @pl.when(pl.program_id(1) == 0)
    def _():
        _fetch_cast_weights(w_hbm, (col0,), wf32, wbf, sems, pl.program_id(0), n_j)
        if conv:
            cbuf[:SUBLANE, :] = jnp.zeros((SUBLANE, tn), F32)

    r = jnp.dot(a_ref[...], wbf[...], preferred_element_type=F32)
    if conv:
        k = cw_ref.shape[0]
        cbuf[SUBLANE:, :] = r
        y = cb_ref[...] + cw_ref[k - 1:k, :] * r
        for d in range(1, k):
            y = y + cw_ref[k - 1 - d:k - d, :] * cbuf[pl.ds(SUBLANE - d, tm), :]
        o_ref[...] = jax.nn.silu(y).astype(o_ref.dtype)
        cbuf[:SUBLANE, :] = r[tm - SUBLANE:]
    else:
        r = act(r) if act else r
        if gated:
            r = g_ref[...].astype(F32) * r
        if added:
            r = p_ref[...].astype(F32) + r
        o_ref[...] = r.astype(o_ref.dtype)


def _proj(h, w3, col0, n, out_dtype, tm, tn, name, act=None, conv=None, gate=None, addend=None):
    seq, k = h.shape
    assert col0 % LANE == 0 and n % tn == 0
    extra, extra_specs = [], []
    scratch = [pltpu.VMEM((k, tn), F32), pltpu.VMEM((k, tn), BF16), pltpu.SemaphoreType.DMA((1,))]
    if conv:
        extra = list(conv)
        extra_specs = [pl.BlockSpec((conv[0].shape[0], tn), lambda j, i: (0, j)),
                       pl.BlockSpec((1, tn), lambda j, i: (0, j))]
        scratch.append(pltpu.VMEM((tm + SUBLANE, tn), F32))
    if gate is not None:
        g, g_col0 = gate
        assert g_col0 % tn == 0
        extra.append(g)
        extra_specs.append(pl.BlockSpec((tm, tn), lambda j, i: (i, g_col0 // tn + j)))
    if addend is not None:
        extra.append(addend)
        extra_specs.append(pl.BlockSpec((tm, tn), lambda j, i: (i, j)))
    return pl.pallas_call(
        functools.partial(_proj_body, col0=col0, n_j=n // tn, act=act, conv=bool(conv),
                          gated=gate is not None, added=addend is not None),
        out_shape=jax.ShapeDtypeStruct((seq, n), out_dtype),
        grid=(n // tn, seq // tm),
        in_specs=[pl.BlockSpec((tm, k), lambda j, i: (i, 0)), pl.BlockSpec(memory_space=pl.ANY)] + extra_specs,
        out_specs=pl.BlockSpec((tm, tn), lambda j, i: (i, j)),
        scratch_shapes=scratch,
        compiler_params=_params("arbitrary", "arbitrary"),
        name=name,
    )(h, w3, *extra)


def _lbar(lre, lim, log_step):
    step = jnp.exp(log_step)
    mag = jnp.exp(lre * step)
    return mag * jnp.cos(lim * step), mag * jnp.sin(lim * step)


def _s5prep_body(lre_ref, lim_ref, st_ref, btr_ref, bti_ref, ctr_ref, cti_ref,
                 wb_ref, wct_ref, w2_ref):
    ns = S5_TILE_STATE
    lre, lim = lre_ref[...], lim_ref[...]
    lbr, lbi = _lbar(lre, lim, st_ref[...])
    den = lre * lre + lim * lim
    nr, ni = lbr - 1.0, lbi
    cr = (nr * lre + ni * lim) / den
    ci = (ni * lre - nr * lim) / den
    row = jnp.right_shift(lax.broadcasted_iota(jnp.int32, (LANE, ns), 0), int(math.log2(S5_GROUP)))
    col = jnp.right_shift(lax.broadcasted_iota(jnp.int32, (LANE, ns), 1), int(math.log2(S5_STATE)))
    same = row == col
    wide = lambda ref: jnp.where(same, jnp.tile(ref[...], (1, ns // LANE)), 0.0)
    btr, bti, ctr, cti = wide(btr_ref), wide(bti_ref), wide(ctr_ref), wide(cti_ref)
    bbr = cr * btr - ci * bti
    bbi = cr * bti + ci * btr
    pows = [(jnp.ones_like(lbr), jnp.zeros_like(lbr))]
    for _ in range(S5_Q):
        pr, pi = pows[-1]
        pows.append((pr * lbr - pi * lbi, pr * lbi + pi * lbr))
    for s in range(S5_Q):
        pr, pi = pows[S5_Q - 1 - s]
        rows = slice(s * LANE, (s + 1) * LANE)
        wb_ref[rows, :ns] = (pr * bbr - pi * bbi).astype(BF16)
        wb_ref[rows, ns:] = (pr * bbi + pi * bbr).astype(BF16)
    for t in range(S5_Q):
        pr, pi = pows[t + 1]
        rows = slice(t * LANE, (t + 1) * LANE)
        wct_ref[rows, :ns] = (ctr * pr - cti * pi).astype(BF16)
        wct_ref[rows, ns:] = (-(ctr * pi + cti * pr)).astype(BF16)
    ccat = jnp.concatenate([ctr, -cti], axis=1).astype(BF16)
    kall = lax.dot_general(wb_ref[...], ccat, (((1,), (1,)), ((), ())),
                           preferred_element_type=F32)
    w2_ref[LANE:, :LANE] = jnp.zeros((LANE, LANE), BF16)
    for tau in range(S5_Q):
        s = S5_Q - 1 - tau
        k_tau = kall[s * LANE:(s + 1) * LANE, :].astype(BF16)
        w2_ref[:LANE, tau * LANE:(tau + 1) * LANE] = k_tau
        if tau + 1 < S5_Q:
            w2_ref[LANE:, (tau + 1) * LANE:(tau + 2) * LANE] = k_tau


def _s5_rows(lam_re, lam_im, log_step):
    row = lambda a: a.astype(F32).reshape(S5_TILES, 1, S5_TILE_STATE)
    return row(lam_re), row(lam_im), row(jnp.broadcast_to(log_step[:, None], lam_re.shape))


def _s5_prep(rows, b_re, b_im, c_re, c_im):
    ns = S5_TILE_STATE

    def tiled(a):
        a = a.astype(F32).reshape(S5_TILES, LANE, S5_STATE)
        return jnp.tile(a, (1, 1, LANE // S5_STATE))

    row_spec = pl.BlockSpec((None, 1, ns), lambda j: (j, 0, 0))
    mat_spec = pl.BlockSpec((None, LANE, LANE), lambda j: (j, 0, 0))
    return pl.pallas_call(
        _s5prep_body,
        out_shape=(jax.ShapeDtypeStruct((S5_TILES, S5_Q * LANE, 2 * ns), BF16),
                   jax.ShapeDtypeStruct((S5_TILES, S5_Q * LANE, 2 * ns), BF16),
                   jax.ShapeDtypeStruct((S5_TILES, 2 * LANE, S5_Q * LANE), BF16)),
        grid=(S5_TILES,),
        in_specs=[row_spec, row_spec, row_spec, mat_spec, mat_spec, mat_spec, mat_spec],
        out_specs=(pl.BlockSpec((None, S5_Q * LANE, 2 * ns), lambda j: (j, 0, 0)),
                   pl.BlockSpec((None, S5_Q * LANE, 2 * ns), lambda j: (j, 0, 0)),
                   pl.BlockSpec((None, 2 * LANE, S5_Q * LANE), lambda j: (j, 0, 0))),
        compiler_params=_params("arbitrary"),
        name="s5prep",
    )(*rows, tiled(b_re.swapaxes(1, 2)), tiled(b_im.swapaxes(1, 2)), tiled(c_re), tiled(c_im))


def _s5_body(u_ref, lre_ref, lim_ref, st_ref, wb_ref, wct_ref, w2_ref, d_ref, o_ref, e_scr, f_scr, acc, ynat):
    ns = S5_TILE_STATE
    nc = e_scr.shape[0]
    half = nc // 2
    even_rows = lambda s: pl.ds(s, half, stride=2 * S5_Q)
    odd_rows = lambda s: pl.ds(S5_Q + s, half, stride=2 * S5_Q)
    slabs = [jnp.concatenate([u_ref[even_rows(s), :], u_ref[odd_rows(s), :]], axis=0)
             for s in range(S5_Q)]
    sb = [x.astype(BF16) for x in slabs]

    e_scr[...] = jnp.dot(jnp.concatenate(sb, axis=1), wb_ref[...], preferred_element_type=F32)
    ar, ai = _lbar(lre_ref[...], lim_ref[...], st_ref[...])
    for _ in range(int(math.log2(S5_Q))):
        ar, ai = ar * ar - ai * ai, 2.0 * ar * ai

    ee_r, ee_i = e_scr[:half, :ns], e_scr[:half, ns:]
    f_scr[:, :ns] = ar * ee_r - ai * ee_i + e_scr[half:, :ns]
    f_scr[:, ns:] = ar * ee_i + ai * ee_r + e_scr[half:, ns:]
    a2r, a2i = ar * ar - ai * ai, 2.0 * ar * ai

    n_pairs = S5_Q // 2
    seg = half // n_pairs
    hr = hi = jnp.zeros((1, ns), F32)
    for p in range(n_pairs):
        for c in range(p * seg, (p + 1) * seg):
            er, ei = f_scr[c:c + 1, :ns], f_scr[c:c + 1, ns:]
            f_scr[c:c + 1, :ns] = hr
            f_scr[c:c + 1, ns:] = hi
            hr, hi = a2r * hr - a2i * hi + er, a2r * hi + a2i * hr + ei
        s = 2 * p
        pair = jnp.concatenate([sb[s], sb[s + 1]], axis=1)
        width = (S5_Q - s) * LANE
        toep = jnp.dot(pair, w2_ref[:, :width], preferred_element_type=F32)
        if p == 0:
            acc[...] = toep
        else:
            acc[:, s * LANE:] += toep

    g_r, g_i = f_scr[:, :ns], f_scr[:, ns:]
    ee_r, ee_i = e_scr[:half, :ns], e_scr[:half, ns:]
    e_scr[half:, :ns] = ar * g_r - ai * g_i + ee_r
    e_scr[half:, ns:] = ar * g_i + ai * g_r + ee_i
    e_scr[:half, :ns] = g_r
    e_scr[:half, ns:] = g_i
    acc[...] += lax.dot_general(e_scr[...].astype(BF16), wct_ref[...], (((1,), (1,)), ((), ())),
                                preferred_element_type=F32)
    d = d_ref[...]
    for t in range(S5_Q):
        y = jax.nn.gelu(acc[:, t * LANE:(t + 1) * LANE] + d * slabs[t])
        ynat[even_rows(t), :] = y[:half]
        ynat[odd_rows(t), :] = y[half:]
    o_ref[...] = ynat[...].astype(BF16)


def _s5(u, rows, wb, wct, w2, d_skip):
    seq = u.shape[0]
    nc, ns = seq // S5_Q, S5_TILE_STATE
    row_spec = pl.BlockSpec((None, 1, ns), lambda j: (j, 0, 0))
    return pl.pallas_call(
        _s5_body,
        out_shape=jax.ShapeDtypeStruct((seq, D_MODEL), BF16),
        grid=(S5_TILES,),
        in_specs=[pl.BlockSpec((seq, LANE), lambda j: (0, j)), row_spec, row_spec, row_spec,
                  pl.BlockSpec((None, S5_Q * LANE, 2 * ns), lambda j: (j, 0, 0)),
                  pl.BlockSpec((None, S5_Q * LANE, 2 * ns), lambda j: (j, 0, 0)),
                  pl.BlockSpec((None, 2 * LANE, S5_Q * LANE), lambda j: (j, 0, 0)),
                  pl.BlockSpec((1, LANE), lambda j: (0, j))],
        out_specs=pl.BlockSpec((seq, LANE), lambda j: (0, j)),
        scratch_shapes=[pltpu.VMEM((nc, 2 * ns), F32), pltpu.VMEM((nc // 2, 2 * ns), F32),
                        pltpu.VMEM((nc, S5_Q * LANE), F32), pltpu.VMEM((seq, LANE), F32)],
        compiler_params=_params("arbitrary"),
        name="s5",
    )(u, *rows, wb, wct, w2, d_skip.astype(F32).reshape(1, D_MODEL))


def _glu_body(a_ref, w_hbm, ba_ref, bb_ref, z_ref, o_ref, wf32, wbf, sems, *, n_j):
    tn = o_ref.shape[1]

    @pl.when(pl.program_id(1) == 0)
    def _():
        _fetch_cast_weights(w_hbm, (0, n_j * tn), wf32, wbf, sems, pl.program_id(0), n_j)

    g = jnp.dot(a_ref[...], wbf[...], preferred_element_type=F32)
    ga = g[:, :tn] + ba_ref[...]
    gb = g[:, tn:] + bb_ref[...]
    o_ref[...] = (ga * jax.nn.sigmoid(gb) * z_ref[...].astype(F32)).astype(BF16)


def _glu(ys5, w_glu3, b_glu, zs, tm, tn):
    seq = ys5.shape[0]
    nb = D_MODEL // tn
    return pl.pallas_call(
        functools.partial(_glu_body, n_j=nb),
        out_shape=jax.ShapeDtypeStruct((seq, D_MODEL), BF16),
        grid=(nb, seq // tm),
        in_specs=[pl.BlockSpec((tm, D_MODEL), lambda j, i: (i, 0)),
                  pl.BlockSpec(memory_space=pl.ANY),
                  pl.BlockSpec((1, tn), lambda j, i: (0, j)),
                  pl.BlockSpec((1, tn), lambda j, i: (0, nb + j)),
                  pl.BlockSpec((tm, tn), lambda j, i: (i, Z_S5 // tn + j))],
        out_specs=pl.BlockSpec((tm, tn), lambda j, i: (i, j)),
        scratch_shapes=[pltpu.VMEM((D_MODEL, 2 * tn), F32), pltpu.VMEM((D_MODEL, 2 * tn), BF16),
                        pltpu.SemaphoreType.DMA((2,))],
        compiler_params=_params("arbitrary", "arbitrary"),
        name="glu",
    )(ys5, w_glu3, b_glu, b_glu, zs)


def _split3(x):
    hi = x.astype(BF16)
    r1 = x - hi.astype(F32)
    mid = r1.astype(BF16)
    lo = (r1 - mid.astype(F32)).astype(BF16)
    return hi, mid, lo


def _split2(x):
    hi = x.astype(BF16)
    mid = (x - hi.astype(F32)).astype(BF16)
    return jnp.concatenate([hi, mid], axis=1)


def _dtprep_body(h_ref, w_ref, dtb_ref, alog_ref, tri_ref, arow_ref, acol_ref, ehm_ref, wbf):
    T = M2_CHUNK

    @pl.when(pl.program_id(0) == 0)
    def _():
        wbf[...] = w_ref[...].astype(BF16)

    raw = jnp.dot(h_ref[...].astype(F32), wbf[...].astype(F32), preferred_element_type=F32)
    tri = tri_ref[...]
    neg_a = -jnp.exp(alog_ref[...])
    pad1 = jnp.zeros((T - M2_HPG, T), F32)
    pad2 = jnp.zeros((T - 2 * M2_HPG, T), F32)
    for ci in range(M2_CHUNKS_PER_STEP):
        rows = slice(ci * T, (ci + 1) * T)
        dt = jax.nn.softplus(raw[rows, :].T + dtb_ref[...])
        hi, mid, lo = _split3(dt * neg_a)
        acum = (jnp.dot(hi, tri, preferred_element_type=F32) + jnp.dot(mid, tri, preferred_element_type=F32)
                + jnp.dot(lo, tri, preferred_element_type=F32))
        rem = acum[:, T - 1:T] - acum
        arow_ref[:, rows] = acum - jnp.log(dt)
        dte = dt * jnp.exp(rem)
        eac = jnp.exp(acum)
        for g in range(M2_GROUPS):
            hs = slice(g * M2_HPG, (g + 1) * M2_HPG)
            acol_ref[g, rows, :] = jnp.concatenate([acum[hs], pad1], axis=0).T
            ehm_ref[g, rows, :] = _split2(jnp.concatenate([dte[hs], eac[hs], pad2], axis=0).T)


def _dt_prep(h, w_in3, dt_bias, a_log):
    seq = h.shape[0]
    T, ts = M2_CHUNK, M2_CHUNK * M2_CHUNKS_PER_STEP
    lanes = lambda v: jnp.broadcast_to(v.astype(F32)[:, None], (M2_HEADS, T))
    tri = (jnp.arange(T)[:, None] <= jnp.arange(T)[None, :]).astype(BF16)
    return pl.pallas_call(
        _dtprep_body,
        out_shape=(jax.ShapeDtypeStruct((M2_HEADS, seq), F32),
                   jax.ShapeDtypeStruct((M2_GROUPS, seq, LANE), F32),
                   jax.ShapeDtypeStruct((M2_GROUPS, seq, 2 * LANE), BF16)),
        grid=(seq // ts,),
        in_specs=[pl.BlockSpec((ts, D_MODEL), lambda c: (c, 0)),
                  pl.BlockSpec((None, D_MODEL, M2_HEADS), lambda c: (0, 0, WIN_DT // M2_HEADS)),
                  pl.BlockSpec((M2_HEADS, T), lambda c: (0, 0)),
                  pl.BlockSpec((M2_HEADS, T), lambda c: (0, 0)),
                  pl.BlockSpec((T, T), lambda c: (0, 0))],
        out_specs=(pl.BlockSpec((M2_HEADS, ts), lambda c: (0, c)),
                   pl.BlockSpec((M2_GROUPS, ts, LANE), lambda c: (0, c, 0)),
                   pl.BlockSpec((M2_GROUPS, ts, 2 * LANE), lambda c: (0, c, 0))),
        scratch_shapes=[pltpu.VMEM((D_MODEL, M2_HEADS), BF16)],
        compiler_params=_params("arbitrary"),
        name="dtprep",
    )(h, w_in3, lanes(dt_bias), lanes(a_log), tri)


def _ssd_body(x_ref, b_ref, c_ref, z_ref, arow_ref, acol_ref, ehm_ref, exp_ref, dskip_ref, ng_ref, o_ref, state):
    @pl.when(pl.program_id(1) == 0)
    def _():
        state[...] = jnp.zeros_like(state)

    for ci in range(M2_CHUNKS_PER_STEP):
        rows = slice(ci * M2_CHUNK, (ci + 1) * M2_CHUNK)
        o_ref[rows, :] = _ssd_chunk(x_ref[rows, :], b_ref[rows, :], c_ref[rows, :], z_ref[rows, :],
                                    arow_ref[:, rows], acol_ref[rows, :], ehm_ref[rows, :],
                                    exp_ref, dskip_ref, ng_ref, state)


def _ssd_chunk(xb, bm_b, cm_b, zb, arow, acol, ehm, exp_ref, dskip_ref, ng_ref, state):
    T = M2_CHUNK
    xs = xb.astype(F32)
    dte_e = jnp.dot(ehm, exp_ref[0], preferred_element_type=F32)
    eac_e = jnp.dot(ehm, exp_ref[1], preferred_element_type=F32)

    scores = lax.dot_general(cm_b, bm_b, (((1,), (1,)), ((), ())), preferred_element_type=F32)
    y_off = jnp.dot(cm_b, state[...].astype(BF16), preferred_element_type=F32) * eac_e

    causal = (lax.broadcasted_iota(jnp.int32, (T, T), 0) >= lax.broadcasted_iota(jnp.int32, (T, T), 1))
    left = lax.broadcasted_iota(jnp.int32, (T, LANE), 1) < M2_HEADDIM
    zero = jnp.zeros((T, LANE), BF16)
    ys = []
    for p in range(M2_HPG // 2):
        lhs = []
        for h in (2 * p, 2 * p + 1):
            seg = jnp.broadcast_to(acol[:, h:h + 1], (T, T)) - jnp.broadcast_to(arow[h:h + 1, :], (T, T))
            lhs.append((scores * jnp.exp(jnp.where(causal, seg, -1e30))).astype(BF16))
        xp = xb[:, p * LANE:(p + 1) * LANE]
        rhs = jnp.concatenate([jnp.where(left, xp, zero), jnp.where(left, zero, xp)], axis=0)
        ys.append(jnp.dot(jnp.concatenate(lhs, axis=1), rhs, preferred_element_type=F32))
    y = jnp.concatenate(ys, axis=1) + y_off + dskip_ref[...] * xs

    xw = (xs * dte_e).astype(BF16)
    state[...] = (state[...] * eac_e[T - 1:T, :]
                  + jnp.dot(bm_b.astype(F32).T.astype(BF16), xw, preferred_element_type=F32))

    y = y * zb.astype(F32)
    return (_rms(y) * ng_ref[...]).astype(BF16)


def _ssd(xbc, zs, arow, acol, ehm, d_skip, norm_g):
    seq = xbc.shape[0]
    T, gw, n = M2_CHUNK * M2_CHUNKS_PER_STEP, M2_GW, M2_STATE
    bb = M2_D_INNER // n
    cb = bb + M2_GROUPS
    col = jnp.arange(2 * LANE)[:, None] % LANE
    head = jnp.arange(gw)[None, :] // M2_HEADDIM
    expand = jnp.stack([col == head, col == head + M2_HPG]).astype(BF16)
    return pl.pallas_call(
        _ssd_body,
        out_shape=jax.ShapeDtypeStruct((seq, M2_D_INNER), BF16),
        grid=(M2_GROUPS, seq // T),
        in_specs=[pl.BlockSpec((T, gw), lambda g, c: (c, g)),
                  pl.BlockSpec((T, n), lambda g, c: (c, bb + g)),
                  pl.BlockSpec((T, n), lambda g, c: (c, cb + g)),
                  pl.BlockSpec((T, gw), lambda g, c: (c, Z_M2 // gw + g)),
                  pl.BlockSpec((M2_HPG, T), lambda g, c: (g, c)),
                  pl.BlockSpec((None, T, LANE), lambda g, c: (g, c, 0)),
                  pl.BlockSpec((None, T, 2 * LANE), lambda g, c: (g, c, 0)),
                  pl.BlockSpec((2, 2 * LANE, gw), lambda g, c: (0, 0, 0)),
                  pl.BlockSpec((1, gw), lambda g, c: (0, g)),
                  pl.BlockSpec((1, gw), lambda g, c: (0, g))],
        out_specs=pl.BlockSpec((T, gw), lambda g, c: (c, g)),
        scratch_shapes=[pltpu.VMEM((n, gw), F32)],
        compiler_params=_params("arbitrary", "arbitrary"),
        name="ssd",
    )(xbc, xbc, xbc, zs, arow, acol, ehm, expand,
      jnp.repeat(d_skip.astype(F32), M2_HEADDIM).reshape(1, M2_D_INNER),
      norm_g.astype(F32).reshape(1, M2_D_INNER))


def _merge(y1, w_s5, y2, w_m2, gates, tm, tm_wide, tn_in, tn):
    part = _proj(y1, w_s5, 0, D_MODEL, BF16, tm, tn_in, "merge_s5", gate=(gates, 0))
    return _proj(y2, w_m2, 0, D_MODEL, BF16, tm_wide, tn, "merge_m2", gate=(gates, D_MODEL), addend=part)


def _out_body(a_ref, w_hbm, x_ref, gate_ref, o_ref, wf32, wbf, sems, *, n_j):
    @pl.when(pl.program_id(1) == 0)
    def _():
        _fetch_cast_weights(w_hbm, (0,), wf32, wbf, sems, pl.program_id(0), n_j)

    m = jnp.dot(a_ref[...], wbf[...], preferred_element_type=F32)
    o_ref[...] = x_ref[...] + gate_ref[...] * m


def _out_proj(merged, w_out3, x2, mod, tm, tn):
    seq = merged.shape[0]
    nb = D_MODEL // tn
    return pl.pallas_call(
        functools.partial(_out_body, n_j=nb),
        out_shape=jax.ShapeDtypeStruct((seq, D_MODEL), F32),
        grid=(nb, seq // tm),
        in_specs=[pl.BlockSpec((tm, D_MODEL), lambda j, i: (i, 0)),
                  pl.BlockSpec(memory_space=pl.ANY),
                  pl.BlockSpec((tm, tn), lambda j, i: (i, j)),
                  pl.BlockSpec((1, tn), lambda j, i: (0, 2 * nb + j))],
        out_specs=pl.BlockSpec((tm, tn), lambda j, i: (i, j)),
        scratch_shapes=[pltpu.VMEM((D_MODEL, tn), F32), pltpu.VMEM((D_MODEL, tn), BF16),
                        pltpu.SemaphoreType.DMA((1,))],
        compiler_params=_params("arbitrary", "arbitrary"),
        name="outproj",
    )(merged, w_out3, x2, mod)


def _fnorm_body(x_ref, g_ref, o_ref):
    o_ref[...] = _rms(x_ref[...]) * g_ref[...]


def _final_norm(x2, g, tm):
    seq = x2.shape[0]
    return pl.pallas_call(
        _fnorm_body,
        out_shape=jax.ShapeDtypeStruct((seq, D_MODEL), F32),
        grid=(seq // tm,),
        in_specs=[pl.BlockSpec((tm, D_MODEL), lambda i: (i, 0)),
                  pl.BlockSpec((1, D_MODEL), lambda i: (0, 0))],
        out_specs=pl.BlockSpec((tm, D_MODEL), lambda i: (i, 0)),
        compiler_params=_params("arbitrary"),
        name="fnorm",
    )(x2, g)


def _tiles(seq):
    return dict(tm=min(1024, seq), tm_merge=min(512, seq), tm_norm=min(512, seq), tn=512, tn_in=1024, tn_glu=256)


def kernel(x, c, w_ada, b_ada, norm_g, w_in, s5_lambda_re, s5_lambda_im, s5_log_step, s5_b_re, s5_b_im,
           s5_c_re, s5_c_im, s5_d, s5_w_glu, s5_b_glu, m2_conv_w, m2_conv_b, m2_dt_bias, m2_a_log, m2_d,
           m2_norm_g, w_br_s5, w_br_m2, w_out, final_g):
    bsz, seq, _ = x.shape
    assert bsz == 1 and w_ada.shape[0] == 1 and seq % (S5_Q * SUBLANE) == 0 and seq % (M2_CHUNK * M2_CHUNKS_PER_STEP) == 0
    tl = _tiles(seq)
    tm, tn = tl["tm"], tl["tn"]
    x2 = x.reshape(seq, D_MODEL)

    conv = (m2_conv_w[0].astype(F32), m2_conv_b[0].astype(F32).reshape(1, M2_CONV_DIM))

    mod = _ada_mod(c, w_ada[0], b_ada[0])
    h = _norm_mod(x2, norm_g[0].reshape(1, D_MODEL), mod, tl["tm_norm"])
    tn_in = tl["tn_in"]
    u = _proj(h, w_in, 0, D_MODEL, F32, tm, tn_in, "inproj_u")
    zs = _proj(h, w_in, WIN_S5Z, WIN_XBC - WIN_S5Z, BF16, tm, tn_in, "inproj_z", act=jax.nn.silu)
    xbc = _proj(h, w_in, WIN_XBC, M2_CONV_DIM, BF16, tm, tn_in, "inproj_x", conv=conv)
    gates = _proj(h, w_in, WIN_GATES, 2 * D_MODEL, BF16, tm, tn_in, "inproj_g", act=jax.nn.sigmoid)

    rows = _s5_rows(s5_lambda_re[0], s5_lambda_im[0], s5_log_step[0])
    wb, wct, w2t = _s5_prep(rows, s5_b_re[0], s5_b_im[0], s5_c_re[0], s5_c_im[0])
    ys5 = _s5(u, rows, wb, wct, w2t, s5_d[0])
    y1 = _glu(ys5, s5_w_glu, s5_b_glu[0].astype(F32).reshape(1, -1), zs, tm, tl["tn_glu"])

    arow, acol, ehm = _dt_prep(h, w_in, m2_dt_bias[0], m2_a_log[0])
    y2 = _ssd(xbc, zs, arow, acol, ehm, m2_d[0], m2_norm_g[0])

    merged = _merge(y1, w_br_s5, y2, w_br_m2, gates, tm, tl["tm_merge"], tn_in, tn)
    xn = _out_proj(merged, w_out, x2, mod, tm, tn)
    out = _final_norm(xn, final_g.astype(F32).reshape(1, D_MODEL), tl["tm_norm"])
    return out.reshape(bsz, seq, D_MODEL)
```

```python
import functools
import math

import jax
import jax.numpy as jnp
from jax import lax
from jax.experimental import pallas as pl
from jax.experimental.pallas import tpu as pltpu

F32 = jnp.float32
BF16 = jnp.bfloat16

LANE = 128
SUBLANE = 8
VMEM_LIMIT_BYTES = 60 * 1024 * 1024

D_MODEL = 4096
EPS = 1e-6

S5_GROUP = 16
S5_STATE = 64
S5_Q = 8
S5_TILE_GROUPS = LANE // S5_GROUP
S5_TILE_STATE = S5_TILE_GROUPS * S5_STATE
S5_TILES = D_MODEL // LANE

M2_D_INNER = 2 * D_MODEL
M2_HEADDIM = 64
M2_HEADS = M2_D_INNER // M2_HEADDIM
M2_GROUPS = 8
M2_HPG = M2_HEADS // M2_GROUPS
M2_STATE = 128
M2_CHUNK = 128
M2_CHUNKS_PER_STEP = 8
M2_GW = M2_HPG * M2_HEADDIM
M2_GN = M2_GROUPS * M2_STATE
M2_CONV_DIM = M2_D_INNER + 2 * M2_GN

WIN_S5Z = D_MODEL
WIN_XBC = 2 * D_MODEL + M2_D_INNER
WIN_DT = WIN_XBC + M2_CONV_DIM
WIN_GATES = WIN_DT + M2_HEADS
Z_S5 = 0
Z_M2 = D_MODEL


def _params(*sem):
    return pltpu.CompilerParams(dimension_semantics=sem, vmem_limit_bytes=VMEM_LIMIT_BYTES)


MOD_TN = 512


def _mod_body(cb_ref, w_ref, b_ref, o_ref):
    cb = cb_ref[...]
    for q in range(MOD_TN // LANE):
        sl = slice(q * LANE, (q + 1) * LANE)
        o_ref[:, sl] = jnp.sum(w_ref[:, sl] * cb, axis=0, keepdims=True) + b_ref[:, sl]


def _ada_mod(c, w_ada, b_ada):
    n = w_ada.shape[1]
    cb = jnp.broadcast_to(c.reshape(D_MODEL, 1), (D_MODEL, LANE))
    return pl.pallas_call(
        _mod_body,
        out_shape=jax.ShapeDtypeStruct((1, n), F32),
        grid=(n // MOD_TN,),
        in_specs=[pl.BlockSpec((D_MODEL, LANE), lambda j: (0, 0)),
                  pl.BlockSpec((D_MODEL, MOD_TN), lambda j: (0, j)),
                  pl.BlockSpec((1, MOD_TN), lambda j: (0, j))],
        out_specs=pl.BlockSpec((1, MOD_TN), lambda j: (0, j)),
        compiler_params=_params("arbitrary"),
        name="mod",
    )(cb, w_ada, b_ada.reshape(1, n))


def _rms(x):
    return x * lax.rsqrt(jnp.mean(x * x, axis=-1, keepdims=True) + EPS)


def _norm_body(x_ref, g_ref, shift_ref, scale_ref, o_ref):
    h = _rms(x_ref[...]) * g_ref[...] * (1.0 + scale_ref[...]) + shift_ref[...]
    o_ref[...] = h.astype(BF16)


def _norm_mod(x2, g, mod, tm):
    seq = x2.shape[0]
    return pl.pallas_call(
        _norm_body,
        out_shape=jax.ShapeDtypeStruct((seq, D_MODEL), BF16),
        grid=(seq // tm,),
        in_specs=[pl.BlockSpec((tm, D_MODEL), lambda i: (i, 0)),
                  pl.BlockSpec((1, D_MODEL), lambda i: (0, 0)),
                  pl.BlockSpec((1, D_MODEL), lambda i: (0, 0)),
                  pl.BlockSpec((1, D_MODEL), lambda i: (0, 1))],
        out_specs=pl.BlockSpec((tm, D_MODEL), lambda i: (i, 0)),
        compiler_params=_params("arbitrary"),
        name="norm",
    )(x2, g, mod, mod)


CAST_ROWS = 512


def _fetch_cast_weights(w_hbm, col_starts, wf32, wbf, sems, j, n_j):
    tn = wf32.shape[1] // len(col_starts)

    def copies(jj):
        return [pltpu.make_async_copy(w_hbm.at[0, :, pl.ds(pl.multiple_of(c0 + jj * tn, LANE), tn)],
                                      wf32.at[:, pl.ds(q * tn, tn)], sems.at[q])
                for q, c0 in enumerate(col_starts)]

    @pl.when(j == 0)
    def _():
        for cp in copies(0):
            cp.start()

    for cp in copies(j):
        cp.wait()
    for r0 in range(0, wf32.shape[0], CAST_ROWS):
        wbf[r0:r0 + CAST_ROWS, :] = wf32[r0:r0 + CAST_ROWS, :].astype(BF16)

    @pl.when(j + 1 < n_j)
    def _():
        for cp in copies(j + 1):
            cp.start()


def _proj_body(*refs, col0, n_j, act, conv, biased, gated, added):
    if conv:
        a_ref, w_hbm, cw_ref, cb_ref, o_ref, wf32, wbf, sems, cbuf = refs
    else:
        a_ref, w_hbm, *ins, o_ref, wf32, wbf, sems = refs
        b_ref = ins.pop(0) if biased else None
        g_ref = ins.pop(0) if gated else None
        p_ref = ins.pop(0) if added else None
    tm, tn = o_ref.shape

    @pl.when(pl.program_id(1) == 0)
    def _():
        _fetch_cast_weights(w_hbm, (col0,), wf32, wbf, sems, pl.program_id(0), n_j)
        if conv:
            cbuf[:SUBLANE, :] = jnp.zeros((SUBLANE, tn), F32)

    r = jnp.dot(a_ref[...], wbf[...], preferred_element_type=F32)
    if conv:
        k = cw_ref.shape[0]
        cbuf[SUBLANE:, :] = r
        y = cb_ref[...] + cw_ref[k - 1:k, :] * r
        for d in range(1, k):
            y = y + cw_ref[k - 1 - d:k - d, :] * cbuf[pl.ds(SUBLANE - d, tm), :]
        o_ref[...] = jax.nn.silu(y).astype(o_ref.dtype)
        cbuf[:SUBLANE, :] = r[tm - SUBLANE:]
    else:
        if biased:
            r = r + b_ref[...]
        r = act(r) if act else r
        if gated:
            r = g_ref[...].astype(F32) * r
        if added:
            r = p_ref[...].astype(F32) + r
        o_ref[...] = r.astype(o_ref.dtype)


def _proj(h, w3, col0, n, out_dtype, tm, tn, name, act=None, conv=None, bias=None, gate=None, addend=None):
    seq, k = h.shape
    assert col0 % LANE == 0 and n % tn == 0
    extra, extra_specs = [], []
    scratch = [pltpu.VMEM((k, tn), F32), pltpu.VMEM((k, tn), BF16), pltpu.SemaphoreType.DMA((1,))]
    if conv:
        extra = list(conv)
        extra_specs = [pl.BlockSpec((conv[0].shape[0], tn), lambda j, i: (0, j)),
                       pl.BlockSpec((1, tn), lambda j, i: (0, j))]
        scratch.append(pltpu.VMEM((tm + SUBLANE, tn), F32))
    if bias is not None:
        b, b_col0 = bias
        assert b_col0 % tn == 0
        extra.append(b)
        extra_specs.append(pl.BlockSpec((1, tn), lambda j, i: (0, b_col0 // tn + j)))
    if gate is not None:
        g, g_col0 = gate
        assert g_col0 % tn == 0
        extra.append(g)
        extra_specs.append(pl.BlockSpec((tm, tn), lambda j, i: (i, g_col0 // tn + j)))
    if addend is not None:
        extra.append(addend)
        extra_specs.append(pl.BlockSpec((tm, tn), lambda j, i: (i, j)))
    return pl.pallas_call(
        functools.partial(_proj_body, col0=col0, n_j=n // tn, act=act, conv=bool(conv), biased=bias is not None,
                          gated=gate is not None, added=addend is not None),
        out_shape=jax.ShapeDtypeStruct((seq, n), out_dtype),
        grid=(n // tn, seq // tm),
        in_specs=[pl.BlockSpec((tm, k), lambda j, i: (i, 0)), pl.BlockSpec(memory_space=pl.ANY)] + extra_specs,
        out_specs=pl.BlockSpec((tm, tn), lambda j, i: (i, j)),
        scratch_shapes=scratch,
        compiler_params=_params("arbitrary", "arbitrary"),
        name=name,
    )(h, w3, *extra)


def _lbar(lre, lim, log_step):
    step = jnp.exp(log_step)
    mag = jnp.exp(lre * step)
    return mag * jnp.cos(lim * step), mag * jnp.sin(lim * step)


def _s5prep_body(lre_ref, lim_ref, st_ref, btr_ref, bti_ref, ctr_ref, cti_ref,
                 wb_ref, wct_ref, w2_ref):
    ns = S5_TILE_STATE
    lre, lim = lre_ref[...], lim_ref[...]
    lbr, lbi = _lbar(lre, lim, st_ref[...])
    den = lre * lre + lim * lim
    nr, ni = lbr - 1.0, lbi
    cr = (nr * lre + ni * lim) / den
    ci = (ni * lre - nr * lim) / den
    row = jnp.right_shift(lax.broadcasted_iota(jnp.int32, (LANE, ns), 0), int(math.log2(S5_GROUP)))
    col = jnp.right_shift(lax.broadcasted_iota(jnp.int32, (LANE, ns), 1), int(math.log2(S5_STATE)))
    same = row == col
    wide = lambda ref: jnp.where(same, jnp.tile(ref[...], (1, ns // LANE)), 0.0)
    btr, bti, ctr, cti = wide(btr_ref), wide(bti_ref), wide(ctr_ref), wide(cti_ref)
    bbr = cr * btr - ci * bti
    bbi = cr * bti + ci * btr
    pows = [(jnp.ones_like(lbr), jnp.zeros_like(lbr))]
    for _ in range(S5_Q):
        pr, pi = pows[-1]
        pows.append((pr * lbr - pi * lbi, pr * lbi + pi * lbr))
    for s in range(S5_Q):
        pr, pi = pows[S5_Q - 1 - s]
        rows = slice(s * LANE, (s + 1) * LANE)
        wb_ref[rows, :ns] = (pr * bbr - pi * bbi).astype(BF16)
        wb_ref[rows, ns:] = (pr * bbi + pi * bbr).astype(BF16)
    for t in range(S5_Q):
        pr, pi = pows[t + 1]
        rows = slice(t * LANE, (t + 1) * LANE)
        wct_ref[rows, :ns] = (ctr * pr - cti * pi).astype(BF16)
        wct_ref[rows, ns:] = (-(ctr * pi + cti * pr)).astype(BF16)
    ccat = jnp.concatenate([ctr, -cti], axis=1).astype(BF16)
    kall = lax.dot_general(wb_ref[...], ccat, (((1,), (1,)), ((), ())),
                           preferred_element_type=F32)
    w2_ref[LANE:, :LANE] = jnp.zeros((LANE, LANE), BF16)
    for tau in range(S5_Q):
        s = S5_Q - 1 - tau
        k_tau = kall[s * LANE:(s + 1) * LANE, :].astype(BF16)
        w2_ref[:LANE, tau * LANE:(tau + 1) * LANE] = k_tau
        if tau + 1 < S5_Q:
            w2_ref[LANE:, (tau + 1) * LANE:(tau + 2) * LANE] = k_tau


def _s5_rows(lam_re, lam_im, log_step):
    row = lambda a: a.astype(F32).reshape(S5_TILES, 1, S5_TILE_STATE)
    return row(lam_re), row(lam_im), row(jnp.broadcast_to(log_step[:, None], lam_re.shape))


def _s5_prep(rows, b_re, b_im, c_re, c_im):
    ns = S5_TILE_STATE

    def tiled(a):
        a = a.astype(F32).reshape(S5_TILES, LANE, S5_STATE)
        return jnp.tile(a, (1, 1, LANE // S5_STATE))

    row_spec = pl.BlockSpec((None, 1, ns), lambda j: (j, 0, 0))
    mat_spec = pl.BlockSpec((None, LANE, LANE), lambda j: (j, 0, 0))
    return pl.pallas_call(
        _s5prep_body,
        out_shape=(jax.ShapeDtypeStruct((S5_TILES, S5_Q * LANE, 2 * ns), BF16),
                   jax.ShapeDtypeStruct((S5_TILES, S5_Q * LANE, 2 * ns), BF16),
                   jax.ShapeDtypeStruct((S5_TILES, 2 * LANE, S5_Q * LANE), BF16)),
        grid=(S5_TILES,),
        in_specs=[row_spec, row_spec, row_spec, mat_spec, mat_spec, mat_spec, mat_spec],
        out_specs=(pl.BlockSpec((None, S5_Q * LANE, 2 * ns), lambda j: (j, 0, 0)),
                   pl.BlockSpec((None, S5_Q * LANE, 2 * ns), lambda j: (j, 0, 0)),
                   pl.BlockSpec((None, 2 * LANE, S5_Q * LANE), lambda j: (j, 0, 0))),
        compiler_params=_params("arbitrary"),
        name="s5prep",
    )(*rows, tiled(b_re.swapaxes(1, 2)), tiled(b_im.swapaxes(1, 2)), tiled(c_re), tiled(c_im))


def _s5_body(u_ref, lre_ref, lim_ref, st_ref, wb_ref, wct_ref, w2_ref, d_ref, o_ref, e_scr, f_scr, acc, ynat):
    ns = S5_TILE_STATE
    nc = e_scr.shape[0]
    half = nc // 2
    even_rows = lambda s: pl.ds(s, half, stride=2 * S5_Q)
    odd_rows = lambda s: pl.ds(S5_Q + s, half, stride=2 * S5_Q)
    slabs = [jnp.concatenate([u_ref[even_rows(s), :], u_ref[odd_rows(s), :]], axis=0)
             for s in range(S5_Q)]
    sb = [x.astype(BF16) for x in slabs]

    e_scr[...] = jnp.dot(jnp.concatenate(sb, axis=1), wb_ref[...], preferred_element_type=F32)
    ar, ai = _lbar(lre_ref[...], lim_ref[...], st_ref[...])
    for _ in range(int(math.log2(S5_Q))):
        ar, ai = ar * ar - ai * ai, 2.0 * ar * ai

    ee_r, ee_i = e_scr[:half, :ns], e_scr[:half, ns:]
    f_scr[:, :ns] = ar * ee_r - ai * ee_i + e_scr[half:, :ns]
    f_scr[:, ns:] = ar * ee_i + ai * ee_r + e_scr[half:, ns:]
    a2r, a2i = ar * ar - ai * ai, 2.0 * ar * ai

    n_pairs = S5_Q // 2
    seg = half // n_pairs
    hr = hi = jnp.zeros((1, ns), F32)
    for p in range(n_pairs):
        for c in range(p * seg, (p + 1) * seg):
            er, ei = f_scr[c:c + 1, :ns], f_scr[c:c + 1, ns:]
            f_scr[c:c + 1, :ns] = hr
            f_scr[c:c + 1, ns:] = hi
            hr, hi = a2r * hr - a2i * hi + er, a2r * hi + a2i * hr + ei
        s = 2 * p
        pair = jnp.concatenate([sb[s], sb[s + 1]], axis=1)
        width = (S5_Q - s) * LANE
        toep = jnp.dot(pair, w2_ref[:, :width], preferred_element_type=F32)
        if p == 0:
            acc[...] = toep
        else:
            acc[:, s * LANE:] += toep

    g_r, g_i = f_scr[:, :ns], f_scr[:, ns:]
    ee_r, ee_i = e_scr[:half, :ns], e_scr[:half, ns:]
    e_scr[half:, :ns] = ar * g_r - ai * g_i + ee_r
    e_scr[half:, ns:] = ar * g_i + ai * g_r + ee_i
    e_scr[:half, :ns] = g_r
    e_scr[:half, ns:] = g_i
    acc[...] += lax.dot_general(e_scr[...].astype(BF16), wct_ref[...], (((1,), (1,)), ((), ())),
                                preferred_element_type=F32)
    d = d_ref[...]
    for t in range(S5_Q):
        y = jax.nn.gelu(acc[:, t * LANE:(t + 1) * LANE] + d * slabs[t])
        ynat[even_rows(t), :] = y[:half]
        ynat[odd_rows(t), :] = y[half:]
    o_ref[...] = ynat[...].astype(BF16)


def _s5(u, rows, wb, wct, w2, d_skip):
    seq = u.shape[0]
    nc, ns = seq // S5_Q, S5_TILE_STATE
    row_spec = pl.BlockSpec((None, 1, ns), lambda j: (j, 0, 0))
    return pl.pallas_call(
        _s5_body,
        out_shape=jax.ShapeDtypeStruct((seq, D_MODEL), BF16),
        grid=(S5_TILES,),
        in_specs=[pl.BlockSpec((seq, LANE), lambda j: (0, j)), row_spec, row_spec, row_spec,
                  pl.BlockSpec((None, S5_Q * LANE, 2 * ns), lambda j: (j, 0, 0)),
                  pl.BlockSpec((None, S5_Q * LANE, 2 * ns), lambda j: (j, 0, 0)),
                  pl.BlockSpec((None, 2 * LANE, S5_Q * LANE), lambda j: (j, 0, 0)),
                  pl.BlockSpec((1, LANE), lambda j: (0, j))],
        out_specs=pl.BlockSpec((seq, LANE), lambda j: (0, j)),
        scratch_shapes=[pltpu.VMEM((nc, 2 * ns), F32), pltpu.VMEM((nc // 2, 2 * ns), F32),
                        pltpu.VMEM((nc, S5_Q * LANE), F32), pltpu.VMEM((seq, LANE), F32)],
        compiler_params=_params("arbitrary"),
        name="s5",
    )(u, *rows, wb, wct, w2, d_skip.astype(F32).reshape(1, D_MODEL))


def _glu(ys5, w_glu3, b_glu, zs, tm, tn):
    gate = _proj(ys5, w_glu3, D_MODEL, D_MODEL, BF16, tm, tn, "glu_gate", act=jax.nn.sigmoid,
                 bias=(b_glu, D_MODEL), gate=(zs, Z_S5))
    return _proj(ys5, w_glu3, 0, D_MODEL, BF16, tm, tn, "glu_value", bias=(b_glu, 0), gate=(gate, 0))


def _split3(x):
    hi = x.astype(BF16)
    r1 = x - hi.astype(F32)
    mid = r1.astype(BF16)
    lo = (r1 - mid.astype(F32)).astype(BF16)
    return hi, mid, lo


def _split2(x):
    hi = x.astype(BF16)
    mid = (x - hi.astype(F32)).astype(BF16)
    return jnp.concatenate([hi, mid], axis=1)


def _dtprep_body(h_ref, w_ref, dtb_ref, alog_ref, tri_ref, arow_ref, acol_ref, ehm_ref, wbf):
    T = M2_CHUNK

    @pl.when(pl.program_id(0) == 0)
    def _():
        wbf[...] = w_ref[...].astype(BF16)

    raw = jnp.dot(h_ref[...].astype(F32), wbf[...].astype(F32), preferred_element_type=F32)
    tri = tri_ref[...]
    neg_a = -jnp.exp(alog_ref[...])
    pad1 = jnp.zeros((T - M2_HPG, T), F32)
    pad2 = jnp.zeros((T - 2 * M2_HPG, T), F32)
    for ci in range(M2_CHUNKS_PER_STEP):
        rows = slice(ci * T, (ci + 1) * T)
        dt = jax.nn.softplus(raw[rows, :].T + dtb_ref[...])
        hi, mid, lo = _split3(dt * neg_a)
        acum = (jnp.dot(hi, tri, preferred_element_type=F32) + jnp.dot(mid, tri, preferred_element_type=F32)
                + jnp.dot(lo, tri, preferred_element_type=F32))
        rem = acum[:, T - 1:T] - acum
        arow_ref[:, rows] = acum - jnp.log(dt)
        dte = dt * jnp.exp(rem)
        eac = jnp.exp(acum)
        for g in range(M2_GROUPS):
            hs = slice(g * M2_HPG, (g + 1) * M2_HPG)
            acol_ref[g, rows, :] = jnp.concatenate([acum[hs], pad1], axis=0).T
            ehm_ref[g, rows, :] = _split2(jnp.concatenate([dte[hs], eac[hs], pad2], axis=0).T)


def _dt_prep(h, w_in3, dt_bias, a_log):
    seq = h.shape[0]
    T, ts = M2_CHUNK, M2_CHUNK * M2_CHUNKS_PER_STEP
    lanes = lambda v: jnp.broadcast_to(v.astype(F32)[:, None], (M2_HEADS, T))
    tri = (jnp.arange(T)[:, None] <= jnp.arange(T)[None, :]).astype(BF16)
    return pl.pallas_call(
        _dtprep_body,
        out_shape=(jax.ShapeDtypeStruct((M2_HEADS, seq), F32),
                   jax.ShapeDtypeStruct((M2_GROUPS, seq, LANE), F32),
                   jax.ShapeDtypeStruct((M2_GROUPS, seq, 2 * LANE), BF16)),
        grid=(seq // ts,),
        in_specs=[pl.BlockSpec((ts, D_MODEL), lambda c: (c, 0)),
                  pl.BlockSpec((None, D_MODEL, M2_HEADS), lambda c: (0, 0, WIN_DT // M2_HEADS)),
                  pl.BlockSpec((M2_HEADS, T), lambda c: (0, 0)),
                  pl.BlockSpec((M2_HEADS, T), lambda c: (0, 0)),
                  pl.BlockSpec((T, T), lambda c: (0, 0))],
        out_specs=(pl.BlockSpec((M2_HEADS, ts), lambda c: (0, c)),
                   pl.BlockSpec((M2_GROUPS, ts, LANE), lambda c: (0, c, 0)),
                   pl.BlockSpec((M2_GROUPS, ts, 2 * LANE), lambda c: (0, c, 0))),
        scratch_shapes=[pltpu.VMEM((D_MODEL, M2_HEADS), BF16)],
        compiler_params=_params("arbitrary"),
        name="dtprep",
    )(h, w_in3, lanes(dt_bias), lanes(a_log), tri)


def _ssd_body(x_ref, b_ref, c_ref, z_ref, arow_ref, acol_ref, ehm_ref, exp_ref, dskip_ref, ng_ref, o_ref, state):
    @pl.when(pl.program_id(1) == 0)
    def _():
        state[...] = jnp.zeros_like(state)

    for ci in range(M2_CHUNKS_PER_STEP):
        rows = slice(ci * M2_CHUNK, (ci + 1) * M2_CHUNK)
        o_ref[rows, :] = _ssd_chunk(x_ref[rows, :], b_ref[rows, :], c_ref[rows, :], z_ref[rows, :],
                                    arow_ref[:, rows], acol_ref[rows, :], ehm_ref[rows, :],
                                    exp_ref, dskip_ref, ng_ref, state)


def _ssd_chunk(xb, bm_b, cm_b, zb, arow, acol, ehm, exp_ref, dskip_ref, ng_ref, state):
    T = M2_CHUNK
    xs = xb.astype(F32)
    dte_e = jnp.dot(ehm, exp_ref[0], preferred_element_type=F32)
    eac_e = jnp.dot(ehm, exp_ref[1], preferred_element_type=F32)

    scores = lax.dot_general(cm_b, bm_b, (((1,), (1,)), ((), ())), preferred_element_type=F32)
    y_off = jnp.dot(cm_b, state[...].astype(BF16), preferred_element_type=F32) * eac_e

    causal = (lax.broadcasted_iota(jnp.int32, (T, T), 0) >= lax.broadcasted_iota(jnp.int32, (T, T), 1))
    left = lax.broadcasted_iota(jnp.int32, (T, LANE), 1) < M2_HEADDIM
    zero = jnp.zeros((T, LANE), BF16)
    ys = []
    for p in range(M2_HPG // 2):
        lhs = []
        for h in (2 * p, 2 * p + 1):
            seg = jnp.broadcast_to(acol[:, h:h + 1], (T, T)) - jnp.broadcast_to(arow[h:h + 1, :], (T, T))
            lhs.append((scores * jnp.exp(jnp.where(causal, seg, -1e30))).astype(BF16))
        xp = xb[:, p * LANE:(p + 1) * LANE]
        rhs = jnp.concatenate([jnp.where(left, xp, zero), jnp.where(left, zero, xp)], axis=0)
        ys.append(jnp.dot(jnp.concatenate(lhs, axis=1), rhs, preferred_element_type=F32))
    y = jnp.concatenate(ys, axis=1) + y_off + dskip_ref[...] * xs

    xw = (xs * dte_e).astype(BF16)
    state[...] = (state[...] * eac_e[T - 1:T, :]
                  + jnp.dot(bm_b.astype(F32).T.astype(BF16), xw, preferred_element_type=F32))

    y = y * zb.astype(F32)
    return (_rms(y) * ng_ref[...]).astype(BF16)


def _ssd(xbc, zs, arow, acol, ehm, d_skip, norm_g):
    seq = xbc.shape[0]
    T, gw, n = M2_CHUNK * M2_CHUNKS_PER_STEP, M2_GW, M2_STATE
    bb = M2_D_INNER // n
    cb = bb + M2_GROUPS
    col = jnp.arange(2 * LANE)[:, None] % LANE
    head = jnp.arange(gw)[None, :] // M2_HEADDIM
    expand = jnp.stack([col == head, col == head + M2_HPG]).astype(BF16)
    return pl.pallas_call(
        _ssd_body,
        out_shape=jax.ShapeDtypeStruct((seq, M2_D_INNER), BF16),
        grid=(M2_GROUPS, seq // T),
        in_specs=[pl.BlockSpec((T, gw), lambda g, c: (c, g)),
                  pl.BlockSpec((T, n), lambda g, c: (c, bb + g)),
                  pl.BlockSpec((T, n), lambda g, c: (c, cb + g)),
                  pl.BlockSpec((T, gw), lambda g, c: (c, Z_M2 // gw + g)),
                  pl.BlockSpec((M2_HPG, T), lambda g, c: (g, c)),
                  pl.BlockSpec((None, T, LANE), lambda g, c: (g, c, 0)),
                  pl.BlockSpec((None, T, 2 * LANE), lambda g, c: (g, c, 0)),
                  pl.BlockSpec((2, 2 * LANE, gw), lambda g, c: (0, 0, 0)),
                  pl.BlockSpec((1, gw), lambda g, c: (0, g)),
                  pl.BlockSpec((1, gw), lambda g, c: (0, g))],
        out_specs=pl.BlockSpec((T, gw), lambda g, c: (c, g)),
        scratch_shapes=[pltpu.VMEM((n, gw), F32)],
        compiler_params=_params("arbitrary", "arbitrary"),
        name="ssd",
    )(xbc, xbc, xbc, zs, arow, acol, ehm, expand,
      jnp.repeat(d_skip.astype(F32), M2_HEADDIM).reshape(1, M2_D_INNER),
      norm_g.astype(F32).reshape(1, M2_D_INNER))


def _merge(y1, w_s5, y2, w_m2, gates, tm, tm_wide, tn_in, tn):
    part = _proj(y1, w_s5, 0, D_MODEL, BF16, tm, tn_in, "merge_s5", gate=(gates, 0))
    return _proj(y2, w_m2, 0, D_MODEL, BF16, tm_wide, tn, "merge_m2", gate=(gates, D_MODEL), addend=part)


def _out_body(a_ref, w_hbm, x_ref, gate_ref, o_ref, wf32, wbf, sems, *, n_j):
    @pl.when(pl.program_id(1) == 0)
    def _():
        _fetch_cast_weights(w_hbm, (0,), wf32, wbf, sems, pl.program_id(0), n_j)

    m = jnp.dot(a_ref[...], wbf[...], preferred_element_type=F32)
    o_ref[...] = x_ref[...] + gate_ref[...] * m


def _out_proj(merged, w_out3, x2, mod, tm, tn):
    seq = merged.shape[0]
    nb = D_MODEL // tn
    return pl.pallas_call(
        functools.partial(_out_body, n_j=nb),
        out_shape=jax.ShapeDtypeStruct((seq, D_MODEL), F32),
        grid=(nb, seq // tm),
        in_specs=[pl.BlockSpec((tm, D_MODEL), lambda j, i: (i, 0)),
                  pl.BlockSpec(memory_space=pl.ANY),
                  pl.BlockSpec((tm, tn), lambda j, i: (i, j)),
                  pl.BlockSpec((1, tn), lambda j, i: (0, 2 * nb + j))],
        out_specs=pl.BlockSpec((tm, tn), lambda j, i: (i, j)),
        scratch_shapes=[pltpu.VMEM((D_MODEL, tn), F32), pltpu.VMEM((D_MODEL, tn), BF16),
                        pltpu.SemaphoreType.DMA((1,))],
        compiler_params=_params("arbitrary", "arbitrary"),
        name="outproj",
    )(merged, w_out3, x2, mod)


def _fnorm_body(x_ref, g_ref, o_ref):
    o_ref[...] = _rms(x_ref[...]) * g_ref[...]


def _final_norm(x2, g, tm):
    seq = x2.shape[0]
    return pl.pallas_call(
        _fnorm_body,
        out_shape=jax.ShapeDtypeStruct((seq, D_MODEL), F32),
        grid=(seq // tm,),
        in_specs=[pl.BlockSpec((tm, D_MODEL), lambda i: (i, 0)),
                  pl.BlockSpec((1, D_MODEL), lambda i: (0, 0))],
        out_specs=pl.BlockSpec((tm, D_MODEL), lambda i: (i, 0)),
        compiler_params=_params("arbitrary"),
        name="fnorm",
    )(x2, g)


def _tiles(seq):
    return dict(tm=min(1024, seq), tm_merge=min(512, seq), tm_norm=min(512, seq), tn=512, tn_in=1024)


def kernel(x, c, w_ada, b_ada, norm_g, w_in, s5_lambda_re, s5_lambda_im, s5_log_step, s5_b_re, s5_b_im,
           s5_c_re, s5_c_im, s5_d, s5_w_glu, s5_b_glu, m2_conv_w, m2_conv_b, m2_dt_bias, m2_a_log, m2_d,
           m2_norm_g, w_br_s5, w_br_m2, w_out, final_g):
    bsz, seq, _ = x.shape
    assert bsz == 1 and w_ada.shape[0] == 1 and seq % (S5_Q * SUBLANE) == 0 and seq % (M2_CHUNK * M2_CHUNKS_PER_STEP) == 0
    tl = _tiles(seq)
    tm, tn = tl["tm"], tl["tn"]
    x2 = x.reshape(seq, D_MODEL)

    conv = (m2_conv_w[0].astype(F32), m2_conv_b[0].astype(F32).reshape(1, M2_CONV_DIM))

    mod = _ada_mod(c, w_ada[0], b_ada[0])
    h = _norm_mod(x2, norm_g[0].reshape(1, D_MODEL), mod, tl["tm_norm"])
    tn_in = tl["tn_in"]
    u = _proj(h, w_in, 0, D_MODEL, F32, tm, tn_in, "inproj_u")
    zs = _proj(h, w_in, WIN_S5Z, WIN_XBC - WIN_S5Z, BF16, tm, tn_in, "inproj_z", act=jax.nn.silu)
    xbc = _proj(h, w_in, WIN_XBC, M2_CONV_DIM, BF16, tm, tn_in, "inproj_x", conv=conv)
    gates = _proj(h, w_in, WIN_GATES, 2 * D_MODEL, BF16, tm, tn_in, "inproj_g", act=jax.nn.sigmoid)

    rows = _s5_rows(s5_lambda_re[0], s5_lambda_im[0], s5_log_step[0])
    wb, wct, w2t = _s5_prep(rows, s5_b_re[0], s5_b_im[0], s5_c_re[0], s5_c_im[0])
    ys5 = _s5(u, rows, wb, wct, w2t, s5_d[0])
    y1 = _glu(ys5, s5_w_glu, s5_b_glu[0].astype(F32).reshape(1, -1), zs, tm, tn_in)

    arow, acol, ehm = _dt_prep(h, w_in, m2_dt_bias[0], m2_a_log[0])
    y2 = _ssd(xbc, zs, arow, acol, ehm, m2_d[0], m2_norm_g[0])

    merged = _merge(y1, w_br_s5, y2, w_br_m2, gates, tm, tl["tm_merge"], tn_in, tn)
    xn = _out_proj(merged, w_out, x2, mod, tm, tn)
    out = _final_norm(xn, final_g.astype(F32).reshape(1, D_MODEL), tl["tm_norm"])
    return out.reshape(bsz, seq, D_MODEL)
```

```python
import functools
import math

import jax
import jax.numpy as jnp
from jax import lax
from jax.experimental import pallas as pl
from jax.experimental.pallas import tpu as pltpu

F32 = jnp.float32
BF16 = jnp.bfloat16

LANE = 128
SUBLANE = 8
VMEM_LIMIT_BYTES = 60 * 1024 * 1024

D_MODEL = 4096
EPS = 1e-6

S5_GROUP = 16
S5_STATE = 64
S5_Q = 8
S5_TILE_GROUPS = LANE // S5_GROUP
S5_TILE_STATE = S5_TILE_GROUPS * S5_STATE
S5_TILES = D_MODEL // LANE

M2_D_INNER = 2 * D_MODEL
M2_HEADDIM = 64
M2_HEADS = M2_D_INNER // M2_HEADDIM
M2_GROUPS = 8
M2_HPG = M2_HEADS // M2_GROUPS
M2_STATE = 128
M2_CHUNK = 128
M2_CHUNKS_PER_STEP = 8
M2_GW = M2_HPG * M2_HEADDIM
M2_GN = M2_GROUPS * M2_STATE
M2_CONV_DIM = M2_D_INNER + 2 * M2_GN

WIN_S5Z = D_MODEL
WIN_XBC = 2 * D_MODEL + M2_D_INNER
WIN_DT = WIN_XBC + M2_CONV_DIM
WIN_GATES = WIN_DT + M2_HEADS
Z_S5 = 0
Z_M2 = D_MODEL


def _params(*sem):
    return pltpu.CompilerParams(dimension_semantics=sem, vmem_limit_bytes=VMEM_LIMIT_BYTES)


MOD_TN = 512


def _mod_body(cb_ref, w_ref, b_ref, o_ref):
    cb = cb_ref[...]
    for q in range(MOD_TN // LANE):
        sl = slice(q * LANE, (q + 1) * LANE)
        o_ref[:, sl] = jnp.sum(w_ref[:, sl] * cb, axis=0, keepdims=True) + b_ref[:, sl]


def _ada_mod(c, w_ada, b_ada):
    n = w_ada.shape[1]
    cb = jnp.broadcast_to(c.reshape(D_MODEL, 1), (D_MODEL, LANE))
    return pl.pallas_call(
        _mod_body,
        out_shape=jax.ShapeDtypeStruct((1, n), F32),
        grid=(n // MOD_TN,),
        in_specs=[pl.BlockSpec((D_MODEL, LANE), lambda j: (0, 0)),
                  pl.BlockSpec((D_MODEL, MOD_TN), lambda j: (0, j)),
                  pl.BlockSpec((1, MOD_TN), lambda j: (0, j))],
        out_specs=pl.BlockSpec((1, MOD_TN), lambda j: (0, j)),
        compiler_params=_params("arbitrary"),
        name="mod",
    )(cb, w_ada, b_ada.reshape(1, n))


def _rms(x):
    return x * lax.rsqrt(jnp.mean(x * x, axis=-1, keepdims=True) + EPS)


def _norm_body(x_ref, g_ref, shift_ref, scale_ref, o_ref):
    h = _rms(x_ref[...]) * g_ref[...] * (1.0 + scale_ref[...]) + shift_ref[...]
    o_ref[...] = h.astype(BF16)


def _norm_mod(x2, g, mod, tm):
    seq = x2.shape[0]
    return pl.pallas_call(
        _norm_body,
        out_shape=jax.ShapeDtypeStruct((seq, D_MODEL), BF16),
        grid=(seq // tm,),
        in_specs=[pl.BlockSpec((tm, D_MODEL), lambda i: (i, 0)),
                  pl.BlockSpec((1, D_MODEL), lambda i: (0, 0)),
                  pl.BlockSpec((1, D_MODEL), lambda i: (0, 0)),
                  pl.BlockSpec((1, D_MODEL), lambda i: (0, 1))],
        out_specs=pl.BlockSpec((tm, D_MODEL), lambda i: (i, 0)),
        compiler_params=_params("arbitrary"),
        name="norm",
    )(x2, g, mod, mod)


CAST_ROWS = 512


def _fetch_cast_weights(w_hbm, col_starts, wf32, wbf, sems, j, n_j):
    tn = wf32.shape[1] // len(col_starts)

    def copies(jj):
        return [pltpu.make_async_copy(w_hbm.at[0, :, pl.ds(pl.multiple_of(c0 + jj * tn, LANE), tn)],
                                      wf32.at[:, pl.ds(q * tn, tn)], sems.at[q])
                for q, c0 in enumerate(col_starts)]

    @pl.when(j == 0)
    def _():
        for cp in copies(0):
            cp.start()

    for cp in copies(j):
        cp.wait()
    for r0 in range(0, wf32.shape[0], CAST_ROWS):
        wbf[r0:r0 + CAST_ROWS, :] = wf32[r0:r0 + CAST_ROWS, :].astype(BF16)

    @pl.when(j + 1 < n_j)
    def _():
        for cp in copies(j + 1):
            cp.start()


def _proj_body(*refs, col0, n_j, act, conv, gated, added):
    g_ref = p_ref = None
    if conv:
        a_ref, w_hbm, cw_ref, cb_ref, o_ref, wf32, wbf, sems, cbuf = refs
    else:
        a_ref, w_hbm, *ins, o_ref, wf32, wbf, sems = refs
        g_ref = ins.pop(0) if gated else None
        p_ref = ins.pop(0) if added else None
    tm, tn = o_ref.shape

    @pl.when(pl.program_id(1) == 0)
    def _():
        _fetch_cast_weights(w_hbm, (col0,), wf32, wbf, sems, pl.program_id(0), n_j)
        if conv:
            cbuf[:SUBLANE, :] = jnp.zeros((SUBLANE, tn), F32)

    r = jnp.dot(a_ref[...], wbf[...], preferred_element_type=F32)
    if conv:
        k = cw_ref.shape[0]
        cbuf[SUBLANE:, :] = r
        y = cb_ref[...] + cw_ref[k - 1:k, :] * r
        for d in range(1, k):
            y = y + cw_ref[k - 1 - d:k - d, :] * cbuf[pl.ds(SUBLANE - d, tm), :]
        o_ref[...] = jax.nn.silu(y).astype(o_ref.dtype)
        cbuf[:SUBLANE, :] = r[tm - SUBLANE:]
    else:
        r = act(r) if act else r
        if gated:
            r = g_ref[...].astype(F32) * r
        if added:
            r = p_ref[...].astype(F32) + r
        o_ref[...] = r.astype(o_ref.dtype)


def _proj(h, w3, col0, n, out_dtype, tm, tn, name, act=None, conv=None, gate=None, addend=None):
    seq, k = h.shape
    assert col0 % LANE == 0 and n % tn == 0
    extra, extra_specs = [], []
    scratch = [pltpu.VMEM((k, tn), F32), pltpu.VMEM((k, tn), BF16), pltpu.SemaphoreType.DMA((1,))]
    if conv:
        extra = list(conv)
        extra_specs = [pl.BlockSpec((conv[0].shape[0], tn), lambda j, i: (0, j)),
                       pl.BlockSpec((1, tn), lambda j, i: (0, j))]
        scratch.append(pltpu.VMEM((tm + SUBLANE, tn), F32))
    if gate is not None:
        g, g_col0 = gate
        assert g_col0 % tn == 0
        extra.append(g)
        extra_specs.append(pl.BlockSpec((tm, tn), lambda j, i: (i, g_col0 // tn + j)))
    if addend is not None:
        extra.append(addend)
        extra_specs.append(pl.BlockSpec((tm, tn), lambda j, i: (i, j)))
    return pl.pallas_call(
        functools.partial(_proj_body, col0=col0, n_j=n // tn, act=act, conv=bool(conv),
                          gated=gate is not None, added=addend is not None),
        out_shape=jax.ShapeDtypeStruct((seq, n), out_dtype),
        grid=(n // tn, seq // tm),
        in_specs=[pl.BlockSpec((tm, k), lambda j, i: (i, 0)), pl.BlockSpec(memory_space=pl.ANY)] + extra_specs,
        out_specs=pl.BlockSpec((tm, tn), lambda j, i: (i, j)),
        scratch_shapes=scratch,
        compiler_params=_params("arbitrary", "arbitrary"),
        name=name,
    )(h, w3, *extra)


def _lbar(lre, lim, log_step):
    step = jnp.exp(log_step)
    mag = jnp.exp(lre * step)
    return mag * jnp.cos(lim * step), mag * jnp.sin(lim * step)


def _s5prep_body(lre_ref, lim_ref, st_ref, btr_ref, bti_ref, ctr_ref, cti_ref,
                 wb_ref, wct_ref, w2_ref):
    ns = S5_TILE_STATE
    lre, lim = lre_ref[...], lim_ref[...]
    lbr, lbi = _lbar(lre, lim, st_ref[...])
    den = lre * lre + lim * lim
    nr, ni = lbr - 1.0, lbi
    cr = (nr * lre + ni * lim) / den
    ci = (ni * lre - nr * lim) / den
    row = jnp.right_shift(lax.broadcasted_iota(jnp.int32, (LANE, ns), 0), int(math.log2(S5_GROUP)))
    col = jnp.right_shift(lax.broadcasted_iota(jnp.int32, (LANE, ns), 1), int(math.log2(S5_STATE)))
    same = row == col
    wide = lambda ref: jnp.where(same, jnp.tile(ref[...], (1, ns // LANE)), 0.0)
    btr, bti, ctr, cti = wide(btr_ref), wide(bti_ref), wide(ctr_ref), wide(cti_ref)
    bbr = cr * btr - ci * bti
    bbi = cr * bti + ci * btr
    pows = [(jnp.ones_like(lbr), jnp.zeros_like(lbr))]
    for _ in range(S5_Q):
        pr, pi = pows[-1]
        pows.append((pr * lbr - pi * lbi, pr * lbi + pi * lbr))
    for s in range(S5_Q):
        pr, pi = pows[S5_Q - 1 - s]
        rows = slice(s * LANE, (s + 1) * LANE)
        wb_ref[rows, :ns] = (pr * bbr - pi * bbi).astype(BF16)
        wb_ref[rows, ns:] = (pr * bbi + pi * bbr).astype(BF16)
    for t in range(S5_Q):
        pr, pi = pows[t + 1]
        rows = slice(t * LANE, (t + 1) * LANE)
        wct_ref[rows, :ns] = (ctr * pr - cti * pi).astype(BF16)
        wct_ref[rows, ns:] = (-(ctr * pi + cti * pr)).astype(BF16)
    ccat = jnp.concatenate([ctr, -cti], axis=1).astype(BF16)
    kall = lax.dot_general(wb_ref[...], ccat, (((1,), (1,)), ((), ())),
                           preferred_element_type=F32)
    w2_ref[LANE:, :LANE] = jnp.zeros((LANE, LANE), BF16)
    for tau in range(S5_Q):
        s = S5_Q - 1 - tau
        k_tau = kall[s * LANE:(s + 1) * LANE, :].astype(BF16)
        w2_ref[:LANE, tau * LANE:(tau + 1) * LANE] = k_tau
        if tau + 1 < S5_Q:
            w2_ref[LANE:, (tau + 1) * LANE:(tau + 2) * LANE] = k_tau


def _s5_rows(lam_re, lam_im, log_step):
    row = lambda a: a.astype(F32).reshape(S5_TILES, 1, S5_TILE_STATE)
    return row(lam_re), row(lam_im), row(jnp.broadcast_to(log_step[:, None], lam_re.shape))


def _s5_prep(rows, b_re, b_im, c_re, c_im):
    ns = S5_TILE_STATE

    def tiled(a):
        a = a.astype(F32).reshape(S5_TILES, LANE, S5_STATE)
        return jnp.tile(a, (1, 1, LANE // S5_STATE))

    row_spec = pl.BlockSpec((None, 1, ns), lambda j: (j, 0, 0))
    mat_spec = pl.BlockSpec((None, LANE, LANE), lambda j: (j, 0, 0))
    return pl.pallas_call(
        _s5prep_body,
        out_shape=(jax.ShapeDtypeStruct((S5_TILES, S5_Q * LANE, 2 * ns), BF16),
                   jax.ShapeDtypeStruct((S5_TILES, S5_Q * LANE, 2 * ns), BF16),
                   jax.ShapeDtypeStruct((S5_TILES, 2 * LANE, S5_Q * LANE), BF16)),
        grid=(S5_TILES,),
        in_specs=[row_spec, row_spec, row_spec, mat_spec, mat_spec, mat_spec, mat_spec],
        out_specs=(pl.BlockSpec((None, S5_Q * LANE, 2 * ns), lambda j: (j, 0, 0)),
                   pl.BlockSpec((None, S5_Q * LANE, 2 * ns), lambda j: (j, 0, 0)),
                   pl.BlockSpec((None, 2 * LANE, S5_Q * LANE), lambda j: (j, 0, 0))),
        compiler_params=_params("arbitrary"),
        name="s5prep",
    )(*rows, tiled(b_re.swapaxes(1, 2)), tiled(b_im.swapaxes(1, 2)), tiled(c_re), tiled(c_im))


def _s5_body(u_hbm, lre_ref, lim_ref, st_ref, wb_ref, wct_ref, w2_ref, d_ref, o_ref, e_scr, f_scr, acc, ynat,
             ubuf, usem):
    ns = S5_TILE_STATE
    nc = e_scr.shape[0]
    half = nc // 2
    even_rows = lambda s: pl.ds(s, half, stride=2 * S5_Q)
    odd_rows = lambda s: pl.ds(S5_Q + s, half, stride=2 * S5_Q)
    j, n_tiles = pl.program_id(0), pl.num_programs(0)
    slot = j % 2

    def gather(tile, sl):
        lanes = pl.ds(pl.multiple_of(tile * LANE, LANE), LANE)
        return [pltpu.make_async_copy(u_hbm.at[:, hlf * S5_Q + s, lanes], ubuf.at[sl, s, pl.ds(hlf * half, half), :],
                                      usem.at[sl, s, hlf])
                for s in range(S5_Q) for hlf in range(2)]

    @pl.when(j == 0)
    def _():
        for cp in gather(0, 0):
            cp.start()

    @pl.when(j + 1 < n_tiles)
    def _():
        for cp in gather(j + 1, 1 - slot):
            cp.start()

    for cp in gather(j, slot):
        cp.wait()
    slabs = [ubuf[slot, s] for s in range(S5_Q)]
    sb = [x.astype(BF16) for x in slabs]

    e_scr[...] = jnp.dot(jnp.concatenate(sb, axis=1), wb_ref[...], preferred_element_type=F32)
    ar, ai = _lbar(lre_ref[...], lim_ref[...], st_ref[...])
    for _ in range(int(math.log2(S5_Q))):
        ar, ai = ar * ar - ai * ai, 2.0 * ar * ai

    ee_r, ee_i = e_scr[:half, :ns], e_scr[:half, ns:]
    f_scr[:, :ns] = ar * ee_r - ai * ee_i + e_scr[half:, :ns]
    f_scr[:, ns:] = ar * ee_i + ai * ee_r + e_scr[half:, ns:]
    a2r, a2i = ar * ar - ai * ai, 2.0 * ar * ai

    n_pairs = S5_Q // 2
    seg = half // n_pairs
    hr = hi = jnp.zeros((1, ns), F32)
    for p in range(n_pairs):
        for c in range(p * seg, (p + 1) * seg):
            er, ei = f_scr[c:c + 1, :ns], f_scr[c:c + 1, ns:]
            f_scr[c:c + 1, :ns] = hr
            f_scr[c:c + 1, ns:] = hi
            hr, hi = a2r * hr - a2i * hi + er, a2r * hi + a2i * hr + ei
        s = 2 * p
        pair = jnp.concatenate([sb[s], sb[s + 1]], axis=1)
        width = (S5_Q - s) * LANE
        toep = jnp.dot(pair, w2_ref[:, :width], preferred_element_type=F32)
        if p == 0:
            acc[...] = toep
        else:
            acc[:, s * LANE:] += toep

    g_r, g_i = f_scr[:, :ns], f_scr[:, ns:]
    ee_r, ee_i = e_scr[:half, :ns], e_scr[:half, ns:]
    e_scr[half:, :ns] = ar * g_r - ai * g_i + ee_r
    e_scr[half:, ns:] = ar * g_i + ai * g_r + ee_i
    e_scr[:half, :ns] = g_r
    e_scr[:half, ns:] = g_i
    acc[...] += lax.dot_general(e_scr[...].astype(BF16), wct_ref[...], (((1,), (1,)), ((), ())),
                                preferred_element_type=F32)
    d = d_ref[...]
    for t in range(S5_Q):
        y = jax.nn.gelu(acc[:, t * LANE:(t + 1) * LANE] + d * slabs[t])
        ynat[even_rows(t), :] = y[:half]
        ynat[odd_rows(t), :] = y[half:]
    o_ref[...] = ynat[...].astype(BF16)


def _s5(u, rows, wb, wct, w2, d_skip):
    seq = u.shape[0]
    nc, ns = seq // S5_Q, S5_TILE_STATE
    row_spec = pl.BlockSpec((None, 1, ns), lambda j: (j, 0, 0))
    return pl.pallas_call(
        _s5_body,
        out_shape=jax.ShapeDtypeStruct((seq, D_MODEL), BF16),
        grid=(S5_TILES,),
        in_specs=[pl.BlockSpec(memory_space=pl.ANY), row_spec, row_spec, row_spec,
                  pl.BlockSpec((None, S5_Q * LANE, 2 * ns), lambda j: (j, 0, 0)),
                  pl.BlockSpec((None, S5_Q * LANE, 2 * ns), lambda j: (j, 0, 0)),
                  pl.BlockSpec((None, 2 * LANE, S5_Q * LANE), lambda j: (j, 0, 0)),
                  pl.BlockSpec((1, LANE), lambda j: (0, j))],
        out_specs=pl.BlockSpec((seq, LANE), lambda j: (0, j)),
        scratch_shapes=[pltpu.VMEM((nc, 2 * ns), F32), pltpu.VMEM((nc // 2, 2 * ns), F32),
                        pltpu.VMEM((nc, S5_Q * LANE), F32), pltpu.VMEM((seq, LANE), F32),
                        pltpu.VMEM((2, S5_Q, nc, LANE), F32), pltpu.SemaphoreType.DMA((2, S5_Q, 2))],
        compiler_params=_params("arbitrary"),
        name="s5",
    )(u.reshape(nc // 2, 2 * S5_Q, D_MODEL), *rows, wb, wct, w2, d_skip.astype(F32).reshape(1, D_MODEL))


def _glu_body(a_ref, w_hbm, ba_ref, bb_ref, z_ref, o_ref, wf32, wbf, sems, *, n_j):
    tn = o_ref.shape[1]

    @pl.when(pl.program_id(1) == 0)
    def _():
        _fetch_cast_weights(w_hbm, (0, n_j * tn), wf32, wbf, sems, pl.program_id(0), n_j)

    g = jnp.dot(a_ref[...], wbf[...], preferred_element_type=F32)
    ga = g[:, :tn] + ba_ref[...]
    gb = g[:, tn:] + bb_ref[...]
    o_ref[...] = (ga * jax.nn.sigmoid(gb) * z_ref[...].astype(F32)).astype(BF16)


def _glu(ys5, w_glu3, b_glu, zs, tm, tn):
    seq = ys5.shape[0]
    nb = D_MODEL // tn
    return pl.pallas_call(
        functools.partial(_glu_body, n_j=nb),
        out_shape=jax.ShapeDtypeStruct((seq, D_MODEL), BF16),
        grid=(nb, seq // tm),
        in_specs=[pl.BlockSpec((tm, D_MODEL), lambda j, i: (i, 0)),
                  pl.BlockSpec(memory_space=pl.ANY),
                  pl.BlockSpec((1, tn), lambda j, i: (0, j)),
                  pl.BlockSpec((1, tn), lambda j, i: (0, nb + j)),
                  pl.BlockSpec((tm, tn), lambda j, i: (i, Z_S5 // tn + j))],
        out_specs=pl.BlockSpec((tm, tn), lambda j, i: (i, j)),
        scratch_shapes=[pltpu.VMEM((D_MODEL, 2 * tn), F32), pltpu.VMEM((D_MODEL, 2 * tn), BF16),
                        pltpu.SemaphoreType.DMA((2,))],
        compiler_params=_params("arbitrary", "arbitrary"),
        name="glu",
    )(ys5, w_glu3, b_glu, b_glu, zs)


def _split3(x):
    hi = x.astype(BF16)
    r1 = x - hi.astype(F32)
    mid = r1.astype(BF16)
    lo = (r1 - mid.astype(F32)).astype(BF16)
    return hi, mid, lo


def _split2(x):
    hi = x.astype(BF16)
    mid = (x - hi.astype(F32)).astype(BF16)
    return jnp.concatenate([hi, mid], axis=1)


def _dtprep_body(h_ref, w_ref, dtb_ref, alog_ref, tri_ref, arow_ref, acol_ref, ehm_ref, wbf):
    T = M2_CHUNK

    @pl.when(pl.program_id(0) == 0)
    def _():
        wbf[...] = w_ref[...].astype(BF16)

    raw = jnp.dot(h_ref[...].astype(F32), wbf[...].astype(F32), preferred_element_type=F32)
    tri = tri_ref[...]
    neg_a = -jnp.exp(alog_ref[...])
    pad1 = jnp.zeros((T - M2_HPG, T), F32)
    pad2 = jnp.zeros((T - 2 * M2_HPG, T), F32)
    for ci in range(M2_CHUNKS_PER_STEP):
        rows = slice(ci * T, (ci + 1) * T)
        dt = jax.nn.softplus(raw[rows, :].T + dtb_ref[...])
        hi, mid, lo = _split3(dt * neg_a)
        acum = (jnp.dot(hi, tri, preferred_element_type=F32) + jnp.dot(mid, tri, preferred_element_type=F32)
                + jnp.dot(lo, tri, preferred_element_type=F32))
        rem = acum[:, T - 1:T] - acum
        arow_ref[:, rows] = acum - jnp.log(dt)
        dte = dt * jnp.exp(rem)
        eac = jnp.exp(acum)
        for g in range(M2_GROUPS):
            hs = slice(g * M2_HPG, (g + 1) * M2_HPG)
            acol_ref[g, rows, :] = jnp.concatenate([acum[hs], pad1], axis=0).T
            ehm_ref[g, rows, :] = _split2(jnp.concatenate([dte[hs], eac[hs], pad2], axis=0).T)


def _dt_prep(h, w_in3, dt_bias, a_log):
    seq = h.shape[0]
    T, ts = M2_CHUNK, M2_CHUNK * M2_CHUNKS_PER_STEP
    lanes = lambda v: jnp.broadcast_to(v.astype(F32)[:, None], (M2_HEADS, T))
    tri = (jnp.arange(T)[:, None] <= jnp.arange(T)[None, :]).astype(BF16)
    return pl.pallas_call(
        _dtprep_body,
        out_shape=(jax.ShapeDtypeStruct((M2_HEADS, seq), F32),
                   jax.ShapeDtypeStruct((M2_GROUPS, seq, LANE), F32),
                   jax.ShapeDtypeStruct((M2_GROUPS, seq, 2 * LANE), BF16)),
        grid=(seq // ts,),
        in_specs=[pl.BlockSpec((ts, D_MODEL), lambda c: (c, 0)),
                  pl.BlockSpec((None, D_MODEL, M2_HEADS), lambda c: (0, 0, WIN_DT // M2_HEADS)),
                  pl.BlockSpec((M2_HEADS, T), lambda c: (0, 0)),
                  pl.BlockSpec((M2_HEADS, T), lambda c: (0, 0)),
                  pl.BlockSpec((T, T), lambda c: (0, 0))],
        out_specs=(pl.BlockSpec((M2_HEADS, ts), lambda c: (0, c)),
                   pl.BlockSpec((M2_GROUPS, ts, LANE), lambda c: (0, c, 0)),
                   pl.BlockSpec((M2_GROUPS, ts, 2 * LANE), lambda c: (0, c, 0))),
        scratch_shapes=[pltpu.VMEM((D_MODEL, M2_HEADS), BF16)],
        compiler_params=_params("arbitrary"),
        name="dtprep",
    )(h, w_in3, lanes(dt_bias), lanes(a_log), tri)


def _ssd_body(x_ref, b_ref, c_ref, z_ref, arow_ref, acol_ref, ehm_ref, exp_ref, dskip_ref, ng_ref, o_ref, state):
    @pl.when(pl.program_id(1) == 0)
    def _():
        state[...] = jnp.zeros_like(state)

    for ci in range(M2_CHUNKS_PER_STEP):
        rows = slice(ci * M2_CHUNK, (ci + 1) * M2_CHUNK)
        o_ref[rows, :] = _ssd_chunk(x_ref[rows, :], b_ref[rows, :], c_ref[rows, :], z_ref[rows, :],
                                    arow_ref[:, rows], acol_ref[rows, :], ehm_ref[rows, :],
                                    exp_ref, dskip_ref, ng_ref, state)


def _ssd_chunk(xb, bm_b, cm_b, zb, arow, acol, ehm, exp_ref, dskip_ref, ng_ref, state):
    T = M2_CHUNK
    xs = xb.astype(F32)
    dte_e = jnp.dot(ehm, exp_ref[0], preferred_element_type=F32)
    eac_e = jnp.dot(ehm, exp_ref[1], preferred_element_type=F32)

    scores = lax.dot_general(cm_b, bm_b, (((1,), (1,)), ((), ())), preferred_element_type=F32)
    y_off = jnp.dot(cm_b, state[...].astype(BF16), preferred_element_type=F32) * eac_e

    causal = (lax.broadcasted_iota(jnp.int32, (T, T), 0) >= lax.broadcasted_iota(jnp.int32, (T, T), 1))
    left = lax.broadcasted_iota(jnp.int32, (T, LANE), 1) < M2_HEADDIM
    zero = jnp.zeros((T, LANE), BF16)
    ys = []
    for p in range(M2_HPG // 2):
        lhs = []
        for h in (2 * p, 2 * p + 1):
            seg = jnp.broadcast_to(acol[:, h:h + 1], (T, T)) - jnp.broadcast_to(arow[h:h + 1, :], (T, T))
            lhs.append((scores * jnp.exp(jnp.where(causal, seg, -1e30))).astype(BF16))
        xp = xb[:, p * LANE:(p + 1) * LANE]
        rhs = jnp.concatenate([jnp.where(left, xp, zero), jnp.where(left, zero, xp)], axis=0)
        ys.append(jnp.dot(jnp.concatenate(lhs, axis=1), rhs, preferred_element_type=F32))
    y = jnp.concatenate(ys, axis=1) + y_off + dskip_ref[...] * xs

    xw = (xs * dte_e).astype(BF16)
    state[...] = (state[...] * eac_e[T - 1:T, :]
                  + jnp.dot(bm_b.astype(F32).T.astype(BF16), xw, preferred_element_type=F32))

    y = y * zb.astype(F32)
    return (_rms(y) * ng_ref[...]).astype(BF16)


def _ssd(xbc, zs, arow, acol, ehm, d_skip, norm_g):
    seq = xbc.shape[0]
    T, gw, n = M2_CHUNK * M2_CHUNKS_PER_STEP, M2_GW, M2_STATE
    bb = M2_D_INNER // n
    cb = bb + M2_GROUPS
    col = jnp.arange(2 * LANE)[:, None] % LANE
    head = jnp.arange(gw)[None, :] // M2_HEADDIM
    expand = jnp.stack([col == head, col == head + M2_HPG]).astype(BF16)
    return pl.pallas_call(
        _ssd_body,
        out_shape=jax.ShapeDtypeStruct((seq, M2_D_INNER), BF16),
        grid=(M2_GROUPS, seq // T),
        in_specs=[pl.BlockSpec((T, gw), lambda g, c: (c, g)),
                  pl.BlockSpec((T, n), lambda g, c: (c, bb + g)),
                  pl.BlockSpec((T, n), lambda g, c: (c, cb + g)),
                  pl.BlockSpec((T, gw), lambda g, c: (c, Z_M2 // gw + g)),
                  pl.BlockSpec((M2_HPG, T), lambda g, c: (g, c)),
                  pl.BlockSpec((None, T, LANE), lambda g, c: (g, c, 0)),
                  pl.BlockSpec((None, T, 2 * LANE), lambda g, c: (g, c, 0)),
                  pl.BlockSpec((2, 2 * LANE, gw), lambda g, c: (0, 0, 0)),
                  pl.BlockSpec((1, gw), lambda g, c: (0, g)),
                  pl.BlockSpec((1, gw), lambda g, c: (0, g))],
        out_specs=pl.BlockSpec((T, gw), lambda g, c: (c, g)),
        scratch_shapes=[pltpu.VMEM((n, gw), F32)],
        compiler_params=_params("arbitrary", "arbitrary"),
        name="ssd",
    )(xbc, xbc, xbc, zs, arow, acol, ehm, expand,
      jnp.repeat(d_skip.astype(F32), M2_HEADDIM).reshape(1, M2_D_INNER),
      norm_g.astype(F32).reshape(1, M2_D_INNER))


def _merge(y1, w_s5, y2, w_m2, gates, tm, tm_wide, tn_in, tn):
    part = _proj(y1, w_s5, 0, D_MODEL, BF16, tm, tn_in, "merge_s5", gate=(gates, 0))
    return _proj(y2, w_m2, 0, D_MODEL, BF16, tm_wide, tn, "merge_m2", gate=(gates, D_MODEL), addend=part)


def _out_body(a_ref, w_hbm, x_ref, gate_ref, o_ref, wf32, wbf, sems, *, n_j):
    @pl.when(pl.program_id(1) == 0)
    def _():
        _fetch_cast_weights(w_hbm, (0,), wf32, wbf, sems, pl.program_id(0), n_j)

    m = jnp.dot(a_ref[...], wbf[...], preferred_element_type=F32)
    o_ref[...] = x_ref[...] + gate_ref[...] * m


def _out_proj(merged, w_out3, x2, mod, tm, tn):
    seq = merged.shape[0]
    nb = D_MODEL // tn
    return pl.pallas_call(
        functools.partial(_out_body, n_j=nb),
        out_shape=jax.ShapeDtypeStruct((seq, D_MODEL), F32),
        grid=(nb, seq // tm),
        in_specs=[pl.BlockSpec((tm, D_MODEL), lambda j, i: (i, 0)),
                  pl.BlockSpec(memory_space=pl.ANY),
                  pl.BlockSpec((tm, tn), lambda j, i: (i, j)),
                  pl.BlockSpec((1, tn), lambda j, i: (0, 2 * nb + j))],
        out_specs=pl.BlockSpec((tm, tn), lambda j, i: (i, j)),
        scratch_shapes=[pltpu.VMEM((D_MODEL, tn), F32), pltpu.VMEM((D_MODEL, tn), BF16),
                        pltpu.SemaphoreType.DMA((1,))],
        compiler_params=_params("arbitrary", "arbitrary"),
        name="outproj",
    )(merged, w_out3, x2, mod)


def _fnorm_body(x_ref, g_ref, o_ref):
    o_ref[...] = _rms(x_ref[...]) * g_ref[...]


def _final_norm(x2, g, tm):
    seq = x2.shape[0]
    return pl.pallas_call(
        _fnorm_body,
        out_shape=jax.ShapeDtypeStruct((seq, D_MODEL), F32),
        grid=(seq // tm,),
        in_specs=[pl.BlockSpec((tm, D_MODEL), lambda i: (i, 0)),
                  pl.BlockSpec((1, D_MODEL), lambda i: (0, 0))],
        out_specs=pl.BlockSpec((tm, D_MODEL), lambda i: (i, 0)),
        compiler_params=_params("arbitrary"),
        name="fnorm",
    )(x2, g)


def _tiles(seq):
    return dict(tm=min(1024, seq), tm_merge=min(512, seq), tm_norm=min(512, seq), tn=512, tn_in=1024, tn_glu=256)


def kernel(x, c, w_ada, b_ada, norm_g, w_in, s5_lambda_re, s5_lambda_im, s5_log_step, s5_b_re, s5_b_im,
           s5_c_re, s5_c_im, s5_d, s5_w_glu, s5_b_glu, m2_conv_w, m2_conv_b, m2_dt_bias, m2_a_log, m2_d,
           m2_norm_g, w_br_s5, w_br_m2, w_out, final_g):
    bsz, seq, _ = x.shape
    assert bsz == 1 and w_ada.shape[0] == 1 and seq % (S5_Q * SUBLANE) == 0 and seq % (M2_CHUNK * M2_CHUNKS_PER_STEP) == 0
    tl = _tiles(seq)
    tm, tn = tl["tm"], tl["tn"]
    x2 = x.reshape(seq, D_MODEL)

    conv = (m2_conv_w[0].astype(F32), m2_conv_b[0].astype(F32).reshape(1, M2_CONV_DIM))

    mod = _ada_mod(c, w_ada[0], b_ada[0])
    h = _norm_mod(x2, norm_g[0].reshape(1, D_MODEL), mod, tl["tm_norm"])
    tn_in = tl["tn_in"]
    u = _proj(h, w_in, 0, D_MODEL, F32, tm, tn_in, "inproj_u")
    zs = _proj(h, w_in, WIN_S5Z, WIN_XBC - WIN_S5Z, BF16, tm, tn_in, "inproj_z", act=jax.nn.silu)
    xbc = _proj(h, w_in, WIN_XBC, M2_CONV_DIM, BF16, tm, tn_in, "inproj_x", conv=conv)
    gates = _proj(h, w_in, WIN_GATES, 2 * D_MODEL, BF16, tm, tn_in, "inproj_g", act=jax.nn.sigmoid)

    rows = _s5_rows(s5_lambda_re[0], s5_lambda_im[0], s5_log_step[0])
    wb, wct, w2t = _s5_prep(rows, s5_b_re[0], s5_b_im[0], s5_c_re[0], s5_c_im[0])
    ys5 = _s5(u, rows, wb, wct, w2t, s5_d[0])
    y1 = _glu(ys5, s5_w_glu, s5_b_glu[0].astype(F32).reshape(1, -1), zs, tm, tl["tn_glu"])

    arow, acol, ehm = _dt_prep(h, w_in, m2_dt_bias[0], m2_a_log[0])
    y2 = _ssd(xbc, zs, arow, acol, ehm, m2_d[0], m2_norm_g[0])

    merged = _merge(y1, w_br_s5, y2, w_br_m2, gates, tm, tl["tm_merge"], tn_in, tn)
    xn = _out_proj(merged, w_out, x2, mod, tm, tn)
    out = _final_norm(xn, final_g.astype(F32).reshape(1, D_MODEL), tl["tm_norm"])
    return out.reshape(bsz, seq, D_MODEL)
```

```python
import functools
import math

import jax
import jax.numpy as jnp
from jax import lax
from jax.experimental import pallas as pl
from jax.experimental.pallas import tpu as pltpu

F32 = jnp.float32
BF16 = jnp.bfloat16

LANE = 128
SUBLANE = 8
VMEM_LIMIT_BYTES = 60 * 1024 * 1024

D_MODEL = 4096
EPS = 1e-6

S5_GROUP = 16
S5_STATE = 64
S5_Q = 8
S5_TILE_GROUPS = LANE // S5_GROUP
S5_TILE_STATE = S5_TILE_GROUPS * S5_STATE
S5_TILES = D_MODEL // LANE

M2_D_INNER = 2 * D_MODEL
M2_HEADDIM = 64
M2_HEADS = M2_D_INNER // M2_HEADDIM
M2_GROUPS = 8
M2_HPG = M2_HEADS // M2_GROUPS
M2_STATE = 128
M2_CHUNK = 128
M2_CHUNKS_PER_STEP = 8
M2_GW = M2_HPG * M2_HEADDIM
M2_GN = M2_GROUPS * M2_STATE
M2_CONV_DIM = M2_D_INNER + 2 * M2_GN

WIN_S5Z = D_MODEL
WIN_XBC = 2 * D_MODEL + M2_D_INNER
WIN_DT = WIN_XBC + M2_CONV_DIM
WIN_GATES = WIN_DT + M2_HEADS
Z_S5 = 0
Z_M2 = D_MODEL


def _params(*sem):
    return pltpu.CompilerParams(dimension_semantics=sem, vmem_limit_bytes=VMEM_LIMIT_BYTES)


MOD_TN = 512


def _mod_body(cb_ref, w_ref, b_ref, o_ref):
    cb = cb_ref[...]
    for q in range(MOD_TN // LANE):
        sl = slice(q * LANE, (q + 1) * LANE)
        o_ref[:, sl] = jnp.sum(w_ref[:, sl] * cb, axis=0, keepdims=True) + b_ref[:, sl]


def _ada_mod(c, w_ada, b_ada):
    n = w_ada.shape[1]
    cb = jnp.broadcast_to(c.reshape(D_MODEL, 1), (D_MODEL, LANE))
    return pl.pallas_call(
        _mod_body,
        out_shape=jax.ShapeDtypeStruct((1, n), F32),
        grid=(n // MOD_TN,),
        in_specs=[pl.BlockSpec((D_MODEL, LANE), lambda j: (0, 0)),
                  pl.BlockSpec((D_MODEL, MOD_TN), lambda j: (0, j)),
                  pl.BlockSpec((1, MOD_TN), lambda j: (0, j))],
        out_specs=pl.BlockSpec((1, MOD_TN), lambda j: (0, j)),
        compiler_params=_params("arbitrary"),
        name="mod",
    )(cb, w_ada, b_ada.reshape(1, n))


def _rms(x):
    return x * lax.rsqrt(jnp.mean(x * x, axis=-1, keepdims=True) + EPS)


def _norm_body(x_ref, g_ref, shift_ref, scale_ref, o_ref):
    h = _rms(x_ref[...]) * g_ref[...] * (1.0 + scale_ref[...]) + shift_ref[...]
    o_ref[...] = h.astype(BF16)


def _norm_mod(x2, g, mod, tm):
    seq = x2.shape[0]
    return pl.pallas_call(
        _norm_body,
        out_shape=jax.ShapeDtypeStruct((seq, D_MODEL), BF16),
        grid=(seq // tm,),
        in_specs=[pl.BlockSpec((tm, D_MODEL), lambda i: (i, 0)),
                  pl.BlockSpec((1, D_MODEL), lambda i: (0, 0)),
                  pl.BlockSpec((1, D_MODEL), lambda i: (0, 0)),
                  pl.BlockSpec((1, D_MODEL), lambda i: (0, 1))],
        out_specs=pl.BlockSpec((tm, D_MODEL), lambda i: (i, 0)),
        compiler_params=_params("arbitrary"),
        name="norm",
    )(x2, g, mod, mod)


CAST_ROWS = 512


def _fetch_cast_weights(w_hbm, col_starts, wf32, wbf, sems, j, n_j):
    tn = wf32.shape[1] // len(col_starts)

    def copies(jj):
        return [pltpu.make_async_copy(w_hbm.at[0, :, pl.ds(pl.multiple_of(c0 + jj * tn, LANE), tn)],
                                      wf32.at[:, pl.ds(q * tn, tn)], sems.at[q])
                for q, c0 in enumerate(col_starts)]

    @pl.when(j == 0)
    def _():
        for cp in copies(0):
            cp.start()

    for cp in copies(j):
        cp.wait()
    for r0 in range(0, wf32.shape[0], CAST_ROWS):
        wbf[r0:r0 + CAST_ROWS, :] = wf32[r0:r0 + CAST_ROWS, :].astype(BF16)

    @pl.when(j + 1 < n_j)
    def _():
        for cp in copies(j + 1):
            cp.start()


def _proj_body(*refs, col0, n_j, act, conv, gated, added):
    g_ref = p_ref = None
    if conv:
        a_ref, w_hbm, cw_ref, cb_ref, o_ref, wf32, wbf, sems, cbuf = refs
    else:
        a_ref, w_hbm, *ins, o_ref, wf32, wbf, sems = refs
        g_ref = ins.pop(0) if gated else None
        p_ref = ins.pop(0) if added else None
    tm, tn = o_ref.shape

    @pl.when(pl.program_id(1) == 0)
    def _():
        _fetch_cast_weights(w_hbm, (col0,), wf32, wbf, sems, pl.program_id(0), n_j)
        if conv:
            cbuf[:SUBLANE, :] = jnp.zeros((SUBLANE, tn), F32)

    r = jnp.dot(a_ref[...], wbf[...], preferred_element_type=F32)
    if conv:
        k = cw_ref.shape[0]
        cbuf[SUBLANE:, :] = r
        y = cb_ref[...] + cw_ref[k - 1:k, :] * r
        for d in range(1, k):
            y = y + cw_ref[k - 1 - d:k - d, :] * cbuf[pl.ds(SUBLANE - d, tm), :]
        o_ref[...] = jax.nn.silu(y).astype(o_ref.dtype)
        cbuf[:SUBLANE, :] = r[tm - SUBLANE:]
    else:
        r = act(r) if act else r
        if gated:
            r = g_ref[...].astype(F32) * r
        if added:
            r = p_ref[...].astype(F32) + r
        o_ref[...] = r.astype(o_ref.dtype)


def _proj(h, w3, col0, n, out_dtype, tm, tn, name, act=None, conv=None, gate=None, addend=None):
    seq, k = h.shape
    assert col0 % LANE == 0 and n % tn == 0
    extra, extra_specs = [], []
    scratch = [pltpu.VMEM((k, tn), F32), pltpu.VMEM((k, tn), BF16), pltpu.SemaphoreType.DMA((1,))]
    if conv:
        extra = list(conv)
        extra_specs = [pl.BlockSpec((conv[0].shape[0], tn), lambda j, i: (0, j)),
                       pl.BlockSpec((1, tn), lambda j, i: (0, j))]
        scratch.append(pltpu.VMEM((tm + SUBLANE, tn), F32))
    if gate is not None:
        g, g_col0 = gate
        assert g_col0 % tn == 0
        extra.append(g)
        extra_specs.append(pl.BlockSpec((tm, tn), lambda j, i: (i, g_col0 // tn + j)))
    if addend is not None:
        extra.append(addend)
        extra_specs.append(pl.BlockSpec((tm, tn), lambda j, i: (i, j)))
    return pl.pallas_call(
        functools.partial(_proj_body, col0=col0, n_j=n // tn, act=act, conv=bool(conv),
                          gated=gate is not None, added=addend is not None),
        out_shape=jax.ShapeDtypeStruct((seq, n), out_dtype),
        grid=(n // tn, seq // tm),
        in_specs=[pl.BlockSpec((tm, k), lambda j, i: (i, 0)), pl.BlockSpec(memory_space=pl.ANY)] + extra_specs,
        out_specs=pl.BlockSpec((tm, tn), lambda j, i: (i, j)),
        scratch_shapes=scratch,
        compiler_params=_params("arbitrary", "arbitrary"),
        name=name,
    )(h, w3, *extra)


def _lbar(lre, lim, log_step):
    step = jnp.exp(log_step)
    mag = jnp.exp(lre * step)
    return mag * jnp.cos(lim * step), mag * jnp.sin(lim * step)


def _s5prep_body(lre_ref, lim_ref, st_ref, btr_ref, bti_ref, ctr_ref, cti_ref,
                 wb_ref, wct_ref, w2_ref):
    ns = S5_TILE_STATE
    lre, lim = lre_ref[...], lim_ref[...]
    lbr, lbi = _lbar(lre, lim, st_ref[...])
    den = lre * lre + lim * lim
    nr, ni = lbr - 1.0, lbi
    cr = (nr * lre + ni * lim) / den
    ci = (ni * lre - nr * lim) / den
    row = jnp.right_shift(lax.broadcasted_iota(jnp.int32, (LANE, ns), 0), int(math.log2(S5_GROUP)))
    col = jnp.right_shift(lax.broadcasted_iota(jnp.int32, (LANE, ns), 1), int(math.log2(S5_STATE)))
    same = row == col
    wide = lambda ref: jnp.where(same, jnp.tile(ref[...], (1, ns // LANE)), 0.0)
    btr, bti, ctr, cti = wide(btr_ref), wide(bti_ref), wide(ctr_ref), wide(cti_ref)
    bbr = cr * btr - ci * bti
    bbi = cr * bti + ci * btr
    pows = [(jnp.ones_like(lbr), jnp.zeros_like(lbr))]
    for _ in range(S5_Q):
        pr, pi = pows[-1]
        pows.append((pr * lbr - pi * lbi, pr * lbi + pi * lbr))
    for s in range(S5_Q):
        pr, pi = pows[S5_Q - 1 - s]
        rows = slice(s * LANE, (s + 1) * LANE)
        wb_ref[rows, :ns] = (pr * bbr - pi * bbi).astype(BF16)
        wb_ref[rows, ns:] = (pr * bbi + pi * bbr).astype(BF16)
    for t in range(S5_Q):
        pr, pi = pows[t + 1]
        rows = slice(t * LANE, (t + 1) * LANE)
        wct_ref[rows, :ns] = (ctr * pr - cti * pi).astype(BF16)
        wct_ref[rows, ns:] = (-(ctr * pi + cti * pr)).astype(BF16)
    ccat = jnp.concatenate([ctr, -cti], axis=1).astype(BF16)
    kall = lax.dot_general(wb_ref[...], ccat, (((1,), (1,)), ((), ())),
                           preferred_element_type=F32)
    w2_ref[LANE:, :LANE] = jnp.zeros((LANE, LANE), BF16)
    for tau in range(S5_Q):
        s = S5_Q - 1 - tau
        k_tau = kall[s * LANE:(s + 1) * LANE, :].astype(BF16)
        w2_ref[:LANE, tau * LANE:(tau + 1) * LANE] = k_tau
        if tau + 1 < S5_Q:
            w2_ref[LANE:, (tau + 1) * LANE:(tau + 2) * LANE] = k_tau


def _s5_rows(lam_re, lam_im, log_step):
    row = lambda a: a.astype(F32).reshape(S5_TILES, 1, S5_TILE_STATE)
    return row(lam_re), row(lam_im), row(jnp.broadcast_to(log_step[:, None], lam_re.shape))


def _s5_prep(rows, b_re, b_im, c_re, c_im):
    ns = S5_TILE_STATE

    def tiled(a):
        a = a.astype(F32).reshape(S5_TILES, LANE, S5_STATE)
        return jnp.tile(a, (1, 1, LANE // S5_STATE))

    row_spec = pl.BlockSpec((None, 1, ns), lambda j: (j, 0, 0))
    mat_spec = pl.BlockSpec((None, LANE, LANE), lambda j: (j, 0, 0))
    return pl.pallas_call(
        _s5prep_body,
        out_shape=(jax.ShapeDtypeStruct((S5_TILES, S5_Q * LANE, 2 * ns), BF16),
                   jax.ShapeDtypeStruct((S5_TILES, S5_Q * LANE, 2 * ns), BF16),
                   jax.ShapeDtypeStruct((S5_TILES, 2 * LANE, S5_Q * LANE), BF16)),
        grid=(S5_TILES,),
        in_specs=[row_spec, row_spec, row_spec, mat_spec, mat_spec, mat_spec, mat_spec],
        out_specs=(pl.BlockSpec((None, S5_Q * LANE, 2 * ns), lambda j: (j, 0, 0)),
                   pl.BlockSpec((None, S5_Q * LANE, 2 * ns), lambda j: (j, 0, 0)),
                   pl.BlockSpec((None, 2 * LANE, S5_Q * LANE), lambda j: (j, 0, 0))),
        compiler_params=_params("arbitrary"),
        name="s5prep",
    )(*rows, tiled(b_re.swapaxes(1, 2)), tiled(b_im.swapaxes(1, 2)), tiled(c_re), tiled(c_im))


def _s5_body(u_hbm, lre_ref, lim_ref, st_ref, wb_ref, wct_ref, w2_ref, d_ref, o_ref, e_scr, f_scr, acc, ynat,
             ubuf, usem):
    ns = S5_TILE_STATE
    nc = e_scr.shape[0]
    half = nc // 2
    even_rows = lambda s: pl.ds(s, half, stride=2 * S5_Q)
    odd_rows = lambda s: pl.ds(S5_Q + s, half, stride=2 * S5_Q)
    j, n_tiles = pl.program_id(0), pl.num_programs(0)
    slot = j % 2

    def gather(tile, sl):
        lanes = pl.ds(pl.multiple_of(tile * LANE, LANE), LANE)
        return [pltpu.make_async_copy(u_hbm.at[:, hlf * S5_Q + s, lanes], ubuf.at[sl, s, pl.ds(hlf * half, half), :],
                                      usem.at[sl, s, hlf])
                for s in range(S5_Q) for hlf in range(2)]

    @pl.when(j == 0)
    def _():
        for k, cp in enumerate(gather(0, 0)):
            cp.start(priority=k % 2)

    @pl.when(j + 1 < n_tiles)
    def _():
        for k, cp in enumerate(gather(j + 1, 1 - slot)):
            cp.start(priority=k % 2)

    for cp in gather(j, slot):
        cp.wait()
    slabs = [ubuf[slot, s] for s in range(S5_Q)]
    sb = [x.astype(BF16) for x in slabs]

    e_scr[...] = jnp.dot(jnp.concatenate(sb, axis=1), wb_ref[...], preferred_element_type=F32)
    ar, ai = _lbar(lre_ref[...], lim_ref[...], st_ref[...])
    for _ in range(int(math.log2(S5_Q))):
        ar, ai = ar * ar - ai * ai, 2.0 * ar * ai

    ee_r, ee_i = e_scr[:half, :ns], e_scr[:half, ns:]
    f_scr[:, :ns] = ar * ee_r - ai * ee_i + e_scr[half:, :ns]
    f_scr[:, ns:] = ar * ee_i + ai * ee_r + e_scr[half:, ns:]
    a2r, a2i = ar * ar - ai * ai, 2.0 * ar * ai

    n_pairs = S5_Q // 2
    seg = half // n_pairs
    hr = hi = jnp.zeros((1, ns), F32)
    for p in range(n_pairs):
        for c in range(p * seg, (p + 1) * seg):
            er, ei = f_scr[c:c + 1, :ns], f_scr[c:c + 1, ns:]
            f_scr[c:c + 1, :ns] = hr
            f_scr[c:c + 1, ns:] = hi
            hr, hi = a2r * hr - a2i * hi + er, a2r * hi + a2i * hr + ei
        s = 2 * p
        pair = jnp.concatenate([sb[s], sb[s + 1]], axis=1)
        width = (S5_Q - s) * LANE
        toep = jnp.dot(pair, w2_ref[:, :width], preferred_element_type=F32)
        if p == 0:
            acc[...] = toep
        else:
            acc[:, s * LANE:] += toep

    g_r, g_i = f_scr[:, :ns], f_scr[:, ns:]
    ee_r, ee_i = e_scr[:half, :ns], e_scr[:half, ns:]
    e_scr[half:, :ns] = ar * g_r - ai * g_i + ee_r
    e_scr[half:, ns:] = ar * g_i + ai * g_r + ee_i
    e_scr[:half, :ns] = g_r
    e_scr[:half, ns:] = g_i
    acc[...] += lax.dot_general(e_scr[...].astype(BF16), wct_ref[...], (((1,), (1,)), ((), ())),
                                preferred_element_type=F32)
    d = d_ref[...]
    for t in range(S5_Q):
        y = jax.nn.gelu(acc[:, t * LANE:(t + 1) * LANE] + d * slabs[t])
        ynat[even_rows(t), :] = y[:half]
        ynat[odd_rows(t), :] = y[half:]
    o_ref[...] = ynat[...].astype(BF16)


def _s5(u, rows, wb, wct, w2, d_skip):
    seq = u.shape[0]
    nc, ns = seq // S5_Q, S5_TILE_STATE
    row_spec = pl.BlockSpec((None, 1, ns), lambda j: (j, 0, 0))
    return pl.pallas_call(
        _s5_body,
        out_shape=jax.ShapeDtypeStruct((seq, D_MODEL), BF16),
        grid=(S5_TILES,),
        in_specs=[pl.BlockSpec(memory_space=pl.ANY), row_spec, row_spec, row_spec,
                  pl.BlockSpec((None, S5_Q * LANE, 2 * ns), lambda j: (j, 0, 0)),
                  pl.BlockSpec((None, S5_Q * LANE, 2 * ns), lambda j: (j, 0, 0)),
                  pl.BlockSpec((None, 2 * LANE, S5_Q * LANE), lambda j: (j, 0, 0)),
                  pl.BlockSpec((1, LANE), lambda j: (0, j))],
        out_specs=pl.BlockSpec((seq, LANE), lambda j: (0, j)),
        scratch_shapes=[pltpu.VMEM((nc, 2 * ns), F32), pltpu.VMEM((nc // 2, 2 * ns), F32),
                        pltpu.VMEM((nc, S5_Q * LANE), F32), pltpu.VMEM((seq, LANE), F32),
                        pltpu.VMEM((2, S5_Q, nc, LANE), F32), pltpu.SemaphoreType.DMA((2, S5_Q, 2))],
        compiler_params=_params("arbitrary"),
        name="s5",
    )(u.reshape(nc // 2, 2 * S5_Q, D_MODEL), *rows, wb, wct, w2, d_skip.astype(F32).reshape(1, D_MODEL))


def _glu_body(a_ref, w_hbm, ba_ref, bb_ref, z_ref, o_ref, wf32, wbf, sems, *, n_j):
    tn = o_ref.shape[1]

    @pl.when(pl.program_id(1) == 0)
    def _():
        _fetch_cast_weights(w_hbm, (0, n_j * tn), wf32, wbf, sems, pl.program_id(0), n_j)

    g = jnp.dot(a_ref[...], wbf[...], preferred_element_type=F32)
    ga = g[:, :tn] + ba_ref[...]
    gb = g[:, tn:] + bb_ref[...]
    o_ref[...] = (ga * jax.nn.sigmoid(gb) * z_ref[...].astype(F32)).astype(BF16)


def _glu(ys5, w_glu3, b_glu, zs, tm, tn):
    seq = ys5.shape[0]
    nb = D_MODEL // tn
    return pl.pallas_call(
        functools.partial(_glu_body, n_j=nb),
        out_shape=jax.ShapeDtypeStruct((seq, D_MODEL), BF16),
        grid=(nb, seq // tm),
        in_specs=[pl.BlockSpec((tm, D_MODEL), lambda j, i: (i, 0)),
                  pl.BlockSpec(memory_space=pl.ANY),
                  pl.BlockSpec((1, tn), lambda j, i: (0, j)),
                  pl.BlockSpec((1, tn), lambda j, i: (0, nb + j)),
                  pl.BlockSpec((tm, tn), lambda j, i: (i, Z_S5 // tn + j))],
        out_specs=pl.BlockSpec((tm, tn), lambda j, i: (i, j)),
        scratch_shapes=[pltpu.VMEM((D_MODEL, 2 * tn), F32), pltpu.VMEM((D_MODEL, 2 * tn), BF16),
                        pltpu.SemaphoreType.DMA((2,))],
        compiler_params=_params("arbitrary", "arbitrary"),
        name="glu",
    )(ys5, w_glu3, b_glu, b_glu, zs)


def _split3(x):
    hi = x.astype(BF16)
    r1 = x - hi.astype(F32)
    mid = r1.astype(BF16)
    lo = (r1 - mid.astype(F32)).astype(BF16)
    return hi, mid, lo


def _split2(x):
    hi = x.astype(BF16)
    mid = (x - hi.astype(F32)).astype(BF16)
    return jnp.concatenate([hi, mid], axis=1)


def _dtprep_body(h_ref, w_ref, dtb_ref, alog_ref, tri_ref, arow_ref, acol_ref, ehm_ref, wbf):
    T = M2_CHUNK

    @pl.when(pl.program_id(0) == 0)
    def _():
        wbf[...] = w_ref[...].astype(BF16)

    raw = jnp.dot(h_ref[...].astype(F32), wbf[...].astype(F32), preferred_element_type=F32)
    tri = tri_ref[...]
    neg_a = -jnp.exp(alog_ref[...])
    pad1 = jnp.zeros((T - M2_HPG, T), F32)
    pad2 = jnp.zeros((T - 2 * M2_HPG, T), F32)
    for ci in range(M2_CHUNKS_PER_STEP):
        rows = slice(ci * T, (ci + 1) * T)
        dt = jax.nn.softplus(raw[rows, :].T + dtb_ref[...])
        hi, mid, lo = _split3(dt * neg_a)
        acum = (jnp.dot(hi, tri, preferred_element_type=F32) + jnp.dot(mid, tri, preferred_element_type=F32)
                + jnp.dot(lo, tri, preferred_element_type=F32))
        rem = acum[:, T - 1:T] - acum
        arow_ref[:, rows] = acum - jnp.log(dt)
        dte = dt * jnp.exp(rem)
        eac = jnp.exp(acum)
        for g in range(M2_GROUPS):
            hs = slice(g * M2_HPG, (g + 1) * M2_HPG)
            acol_ref[g, rows, :] = jnp.concatenate([acum[hs], pad1], axis=0).T
            ehm_ref[g, rows, :] = _split2(jnp.concatenate([dte[hs], eac[hs], pad2], axis=0).T)


def _dt_prep(h, w_in3, dt_bias, a_log):
    seq = h.shape[0]
    T, ts = M2_CHUNK, M2_CHUNK * M2_CHUNKS_PER_STEP
    lanes = lambda v: jnp.broadcast_to(v.astype(F32)[:, None], (M2_HEADS, T))
    tri = (jnp.arange(T)[:, None] <= jnp.arange(T)[None, :]).astype(BF16)
    return pl.pallas_call(
        _dtprep_body,
        out_shape=(jax.ShapeDtypeStruct((M2_HEADS, seq), F32),
                   jax.ShapeDtypeStruct((M2_GROUPS, seq, LANE), F32),
                   jax.ShapeDtypeStruct((M2_GROUPS, seq, 2 * LANE), BF16)),
        grid=(seq // ts,),
        in_specs=[pl.BlockSpec((ts, D_MODEL), lambda c: (c, 0)),
                  pl.BlockSpec((None, D_MODEL, M2_HEADS), lambda c: (0, 0, WIN_DT // M2_HEADS)),
                  pl.BlockSpec((M2_HEADS, T), lambda c: (0, 0)),
                  pl.BlockSpec((M2_HEADS, T), lambda c: (0, 0)),
                  pl.BlockSpec((T, T), lambda c: (0, 0))],
        out_specs=(pl.BlockSpec((M2_HEADS, ts), lambda c: (0, c)),
                   pl.BlockSpec((M2_GROUPS, ts, LANE), lambda c: (0, c, 0)),
                   pl.BlockSpec((M2_GROUPS, ts, 2 * LANE), lambda c: (0, c, 0))),
        scratch_shapes=[pltpu.VMEM((D_MODEL, M2_HEADS), BF16)],
        compiler_params=_params("arbitrary"),
        name="dtprep",
    )(h, w_in3, lanes(dt_bias), lanes(a_log), tri)


def _ssd_body(x_ref, b_ref, c_ref, z_ref, arow_ref, acol_ref, ehm_ref, exp_ref, dskip_ref, ng_ref, o_ref, state):
    @pl.when(pl.program_id(1) == 0)
    def _():
        state[...] = jnp.zeros_like(state)

    for ci in range(M2_CHUNKS_PER_STEP):
        rows = slice(ci * M2_CHUNK, (ci + 1) * M2_CHUNK)
        o_ref[rows, :] = _ssd_chunk(x_ref[rows, :], b_ref[rows, :], c_ref[rows, :], z_ref[rows, :],
                                    arow_ref[:, rows], acol_ref[rows, :], ehm_ref[rows, :],
                                    exp_ref, dskip_ref, ng_ref, state)


def _ssd_chunk(xb, bm_b, cm_b, zb, arow, acol, ehm, exp_ref, dskip_ref, ng_ref, state):
    T = M2_CHUNK
    xs = xb.astype(F32)
    dte_e = jnp.dot(ehm, exp_ref[0], preferred_element_type=F32)
    eac_e = jnp.dot(ehm, exp_ref[1], preferred_element_type=F32)

    scores = lax.dot_general(cm_b, bm_b, (((1,), (1,)), ((), ())), preferred_element_type=F32)
    y_off = jnp.dot(cm_b, state[...].astype(BF16), preferred_element_type=F32) * eac_e

    causal = (lax.broadcasted_iota(jnp.int32, (T, T), 0) >= lax.broadcasted_iota(jnp.int32, (T, T), 1))
    left = lax.broadcasted_iota(jnp.int32, (T, LANE), 1) < M2_HEADDIM
    zero = jnp.zeros((T, LANE), BF16)
    ys = []
    for p in range(M2_HPG // 2):
        lhs = []
        for h in (2 * p, 2 * p + 1):
            seg = jnp.broadcast_to(acol[:, h:h + 1], (T, T)) - jnp.broadcast_to(arow[h:h + 1, :], (T, T))
            lhs.append((scores * jnp.exp(jnp.where(causal, seg, -1e30))).astype(BF16))
        xp = xb[:, p * LANE:(p + 1) * LANE]
        rhs = jnp.concatenate([jnp.where(left, xp, zero), jnp.where(left, zero, xp)], axis=0)
        ys.append(jnp.dot(jnp.concatenate(lhs, axis=1), rhs, preferred_element_type=F32))
    y = jnp.concatenate(ys, axis=1) + y_off + dskip_ref[...] * xs

    xw = (xs * dte_e).astype(BF16)
    state[...] = (state[...] * eac_e[T - 1:T, :]
                  + jnp.dot(bm_b.astype(F32).T.astype(BF16), xw, preferred_element_type=F32))

    y = y * zb.astype(F32)
    return (_rms(y) * ng_ref[...]).astype(BF16)


def _ssd(xbc, zs, arow, acol, ehm, d_skip, norm_g):
    seq = xbc.shape[0]
    T, gw, n = M2_CHUNK * M2_CHUNKS_PER_STEP, M2_GW, M2_STATE
    bb = M2_D_INNER // n
    cb = bb + M2_GROUPS
    col = jnp.arange(2 * LANE)[:, None] % LANE
    head = jnp.arange(gw)[None, :] // M2_HEADDIM
    expand = jnp.stack([col == head, col == head + M2_HPG]).astype(BF16)
    return pl.pallas_call(
        _ssd_body,
        out_shape=jax.ShapeDtypeStruct((seq, M2_D_INNER), BF16),
        grid=(M2_GROUPS, seq // T),
        in_specs=[pl.BlockSpec((T, gw), lambda g, c: (c, g)),
                  pl.BlockSpec((T, n), lambda g, c: (c, bb + g)),
                  pl.BlockSpec((T, n), lambda g, c: (c, cb + g)),
                  pl.BlockSpec((T, gw), lambda g, c: (c, Z_M2 // gw + g)),
                  pl.BlockSpec((M2_HPG, T), lambda g, c: (g, c)),
                  pl.BlockSpec((None, T, LANE), lambda g, c: (g, c, 0)),
                  pl.BlockSpec((None, T, 2 * LANE), lambda g, c: (g, c, 0)),
                  pl.BlockSpec((2, 2 * LANE, gw), lambda g, c: (0, 0, 0)),
                  pl.BlockSpec((1, gw), lambda g, c: (0, g)),
                  pl.BlockSpec((1, gw), lambda g, c: (0, g))],
        out_specs=pl.BlockSpec((T, gw), lambda g, c: (c, g)),
        scratch_shapes=[pltpu.VMEM((n, gw), F32)],
        compiler_params=_params("arbitrary", "arbitrary"),
        name="ssd",
    )(xbc, xbc, xbc, zs, arow, acol, ehm, expand,
      jnp.repeat(d_skip.astype(F32), M2_HEADDIM).reshape(1, M2_D_INNER),
      norm_g.astype(F32).reshape(1, M2_D_INNER))


def _merge(y1, w_s5, y2, w_m2, gates, tm, tm_wide, tn_in, tn):
    part = _proj(y1, w_s5, 0, D_MODEL, BF16, tm, tn_in, "merge_s5", gate=(gates, 0))
    return _proj(y2, w_m2, 0, D_MODEL, BF16, tm_wide, tn, "merge_m2", gate=(gates, D_MODEL), addend=part)


def _out_body(a_ref, w_hbm, x_ref, gate_ref, o_ref, wf32, wbf, sems, *, n_j):
    @pl.when(pl.program_id(1) == 0)
    def _():
        _fetch_cast_weights(w_hbm, (0,), wf32, wbf, sems, pl.program_id(0), n_j)

    m = jnp.dot(a_ref[...], wbf[...], preferred_element_type=F32)
    o_ref[...] = x_ref[...] + gate_ref[...] * m


def _out_proj(merged, w_out3, x2, mod, tm, tn):
    seq = merged.shape[0]
    nb = D_MODEL // tn
    return pl.pallas_call(
        functools.partial(_out_body, n_j=nb),
        out_shape=jax.ShapeDtypeStruct((seq, D_MODEL), F32),
        grid=(nb, seq // tm),
        in_specs=[pl.BlockSpec((tm, D_MODEL), lambda j, i: (i, 0)),
                  pl.BlockSpec(memory_space=pl.ANY),
                  pl.BlockSpec((tm, tn), lambda j, i: (i, j)),
                  pl.BlockSpec((1, tn), lambda j, i: (0, 2 * nb + j))],
        out_specs=pl.BlockSpec((tm, tn), lambda j, i: (i, j)),
        scratch_shapes=[pltpu.VMEM((D_MODEL, tn), F32), pltpu.VMEM((D_MODEL, tn), BF16),
                        pltpu.SemaphoreType.DMA((1,))],
        compiler_params=_params("arbitrary", "arbitrary"),
        name="outproj",
    )(merged, w_out3, x2, mod)


def _fnorm_body(x_ref, g_ref, o_ref):
    o_ref[...] = _rms(x_ref[...]) * g_ref[...]


def _final_norm(x2, g, tm):
    seq = x2.shape[0]
    return pl.pallas_call(
        _fnorm_body,
        out_shape=jax.ShapeDtypeStruct((seq, D_MODEL), F32),
        grid=(seq // tm,),
        in_specs=[pl.BlockSpec((tm, D_MODEL), lambda i: (i, 0)),
                  pl.BlockSpec((1, D_MODEL), lambda i: (0, 0))],
        out_specs=pl.BlockSpec((tm, D_MODEL), lambda i: (i, 0)),
        compiler_params=_params("arbitrary"),
        name="fnorm",
    )(x2, g)


def _tiles(seq):
    return dict(tm=min(1024, seq), tm_merge=min(512, seq), tm_norm=min(512, seq), tn=512, tn_in=1024, tn_glu=256)


def kernel(x, c, w_ada, b_ada, norm_g, w_in, s5_lambda_re, s5_lambda_im, s5_log_step, s5_b_re, s5_b_im,
           s5_c_re, s5_c_im, s5_d, s5_w_glu, s5_b_glu, m2_conv_w, m2_conv_b, m2_dt_bias, m2_a_log, m2_d,
           m2_norm_g, w_br_s5, w_br_m2, w_out, final_g):
    bsz, seq, _ = x.shape
    assert bsz == 1 and w_ada.shape[0] == 1 and seq % (S5_Q * SUBLANE) == 0 and seq % (M2_CHUNK * M2_CHUNKS_PER_STEP) == 0
    tl = _tiles(seq)
    tm, tn = tl["tm"], tl["tn"]
    x2 = x.reshape(seq, D_MODEL)

    conv = (m2_conv_w[0].astype(F32), m2_conv_b[0].astype(F32).reshape(1, M2_CONV_DIM))

    mod = _ada_mod(c, w_ada[0], b_ada[0])
    h = _norm_mod(x2, norm_g[0].reshape(1, D_MODEL), mod, tl["tm_norm"])
    tn_in = tl["tn_in"]
    u = _proj(h, w_in, 0, D_MODEL, F32, tm, tn_in, "inproj_u")
    zs = _proj(h, w_in, WIN_S5Z, WIN_XBC - WIN_S5Z, BF16, tm, tn_in, "inproj_z", act=jax.nn.silu)
    xbc = _proj(h, w_in, WIN_XBC, M2_CONV_DIM, BF16, tm, tn_in, "inproj_x", conv=conv)
    gates = _proj(h, w_in, WIN_GATES, 2 * D_MODEL, BF16, tm, tn_in, "inproj_g", act=jax.nn.sigmoid)

    rows = _s5_rows(s5_lambda_re[0], s5_lambda_im[0], s5_log_step[0])
    wb, wct, w2t = _s5_prep(rows, s5_b_re[0], s5_b_im[0], s5_c_re[0], s5_c_im[0])
    ys5 = _s5(u, rows, wb, wct, w2t, s5_d[0])
    y1 = _glu(ys5, s5_w_glu, s5_b_glu[0].astype(F32).reshape(1, -1), zs, tm, tl["tn_glu"])

    arow, acol, ehm = _dt_prep(h, w_in, m2_dt_bias[0], m2_a_log[0])
    y2 = _ssd(xbc, zs, arow, acol, ehm, m2_d[0], m2_norm_g[0])

    merged = _merge(y1, w_br_s5, y2, w_br_m2, gates, tm, tl["tm_merge"], tn_in, tn)
    xn = _out_proj(merged, w_out, x2, mod, tm, tn)
    out = _final_norm(xn, final_g.astype(F32).reshape(1, D_MODEL), tl["tm_norm"])
    return out.reshape(bsz, seq, D_MODEL)
```
